```python
import jax
import jax.numpy as jnp
from jax import lax
import numpy as np

D_MODEL = 1024
BATCH = 32
SEQ = 2048
DEPTH = 4

GRID_W = 64
N_MIXERS = 3
HEAD_DIM = 64
D_FF = 4 * D_MODEL
D_PLE = 256
NA_HEADS = D_MODEL // HEAD_DIM
NA_KH_MAX = 8
NA_KW = 16
GQA_HEADS = D_MODEL // HEAD_DIM
GQA_KV_HEADS = GQA_HEADS // 4
GQA_GROUP = GQA_HEADS // GQA_KV_HEADS
Q_BLOCK = 128
ROPE_THETA = 10000.0
ML_HEADS = 8
ML_DV = D_MODEL // ML_HEADS
ML_DQK = ML_DV // 2
ML_CHUNK = 64
N_A = (DEPTH + N_MIXERS - 1) // N_MIXERS
N_B = (DEPTH + N_MIXERS - 2) // N_MIXERS
N_C = DEPTH // N_MIXERS
DN_ALPHA = (2 * DEPTH) ** 0.25
DN_BETA = (8 * DEPTH) ** -0.25
EPS = 1e-6

kernel_name = 'hybrid_natten_gqa_mlstm_encoder'


def layer_norm(x, g, b):
    xf = x.astype(jnp.float32)
    mu = jnp.mean(xf, -1, keepdims=True)
    var = jnp.mean(jnp.square(xf - mu), -1, keepdims=True)
    return ((xf - mu) * lax.rsqrt(var + EPS) * g.astype(jnp.float32) + b.astype(jnp.float32)).astype(x.dtype)


def rms_norm(x, g):
    xf = x.astype(jnp.float32)
    return xf * lax.rsqrt(jnp.mean(jnp.square(xf), -1, keepdims=True) + EPS) * g.astype(jnp.float32)


def neighborhood_attention(x, w_qkv, rpb, w_o):
    bsz, seq, _ = x.shape
    rows = seq // GRID_W
    kh = min(NA_KH_MAX, rows)
    qkv = (x @ w_qkv).reshape(bsz, rows, GRID_W, 3, NA_HEADS, HEAD_DIM)
    q = qkv[:, :, :, 0] * HEAD_DIM ** -0.5
    k = qkv[:, :, :, 1]
    v = qkv[:, :, :, 2]
    col = np.arange(GRID_W)
    c0 = np.clip(col - NA_KW // 2, 0, GRID_W - NA_KW)
    col_in = (col[None, :] >= c0[:, None]) & (col[None, :] < c0[:, None] + NA_KW)
    dc = np.clip(col[None, :] - col[:, None], 1 - NA_KW, NA_KW - 1) + NA_KW - 1

    def row_block(r):
        r0 = jnp.clip(r - kh // 2, 0, rows - kh)
        q_r = lax.dynamic_index_in_dim(q, r, axis=1, keepdims=False)
        k_b = lax.dynamic_slice_in_dim(k, r0, kh, axis=1)
        v_b = lax.dynamic_slice_in_dim(v, r0, kh, axis=1)
        s = jnp.einsum('bqhd,bakhd->bhqak', q_r, k_b, preferred_element_type=jnp.float32)
        dr = r0 + jnp.arange(kh) - r + NA_KH_MAX - 1
        bias = rpb[:, dr[None, :, None], dc[:, None, :]].astype(jnp.float32)
        s = jnp.where(col_in[:, None, :], s + bias, -jnp.inf)
        pr = jax.nn.softmax(s, axis=(-2, -1))
        return jnp.einsum('bhqak,bakhd->bqhd', pr.astype(v.dtype), v_b)

    out = lax.map(row_block, jnp.arange(rows))
    return jnp.moveaxis(out, 0, 1).reshape(bsz, seq, D_MODEL) @ w_o


def axial_rope_tables(seq):
    t = jnp.arange(seq)
    row = (t // GRID_W).astype(jnp.float32)
    col = (t % GRID_W).astype(jnp.float32)
    n_pairs = HEAD_DIM // 4
    inv = ROPE_THETA ** (-jnp.arange(n_pairs, dtype=jnp.float32) / n_pairs)
    ang = jnp.concatenate([row[:, None] * inv, col[:, None] * inv], -1)
    return jnp.cos(ang), jnp.sin(ang)


def apply_rope(x, cos, sin):
    shape = (x.shape[1],) + (1,) * (x.ndim - 3) + (cos.shape[-1],)
    c = cos.reshape(shape)
    s = sin.reshape(shape)
    x0, x1 = x[..., 0::2], x[..., 1::2]
    return jnp.stack([x0 * c - x1 * s, x0 * s + x1 * c], -1).reshape(x.shape)


def gqa_axial(x, w_qkv, q_norm, k_norm, w_o):
    bsz, seq, _ = x.shape
    kvd = GQA_KV_HEADS * HEAD_DIM
    qkv = x @ w_qkv
    q = qkv[..., :D_MODEL].reshape(bsz, seq, GQA_KV_HEADS, GQA_GROUP, HEAD_DIM)
    k = qkv[..., D_MODEL:D_MODEL + kvd].reshape(bsz, seq, GQA_KV_HEADS, HEAD_DIM)
    v = qkv[..., D_MODEL + kvd:].reshape(bsz, seq, GQA_KV_HEADS, HEAD_DIM)
    cos, sin = axial_rope_tables(seq)
    q = (apply_rope(rms_norm(q, q_norm), cos, sin) * HEAD_DIM ** -0.5).astype(x.dtype)
    k = apply_rope(rms_norm(k, k_norm), cos, sin).astype(x.dtype)
    n_blk = seq // Q_BLOCK
    qb = jnp.moveaxis(q.reshape(bsz, n_blk, Q_BLOCK, GQA_KV_HEADS, GQA_GROUP, HEAD_DIM), 1, 0)

    def block(q_blk):
        s = jnp.einsum('bqkgd,bskd->bkgqs', q_blk, k, preferred_element_type=jnp.float32)
        pr = jax.nn.softmax(s, axis=-1)
        return jnp.einsum('bkgqs,bskd->bqkgd', pr.astype(v.dtype), v)

    out = lax.map(block, qb)
    return jnp.moveaxis(out, 0, 1).reshape(bsz, seq, D_MODEL) @ w_o


def mlstm_scan(q, k, v, log_i, log_f):
    bsz, nh, seq, dqk = q.shape
    dv = v.shape[-1]
    nc, L = seq // ML_CHUNK, ML_CHUNK
    qc, kc, vc = [jnp.moveaxis(a.reshape(bsz, nh, nc, L, a.shape[-1]), 2, 0) for a in (q, k, v)]
    ic, fc = [jnp.moveaxis(a.reshape(bsz, nh, nc, L), 2, 0) for a in (log_i, log_f)]
    tri = jnp.tril(jnp.ones((L, L), bool))

    def step(carry, xs):
        C, n, m = carry
        q_, k_, v_, li, lf = xs
        b = jnp.cumsum(lf, axis=-1)
        d = jnp.where(tri, b[..., :, None] - b[..., None, :] + li[..., None, :], -jnp.inf)
        inter = b + m[..., None]
        m_t = jnp.maximum(jnp.max(d, -1), inter)
        w = jnp.exp(d - m_t[..., None])
        s_inter = jnp.exp(inter - m_t)
        qk = jnp.einsum('bhtd,bhsd->bhts', q_, k_) * w
        num = s_inter[..., None] * jnp.einsum('bhtd,bhdv->bhtv', q_, C) + jnp.einsum('bhts,bhsv->bhtv', qk, v_)
        den = s_inter * jnp.einsum('bhtd,bhd->bht', q_, n) + jnp.sum(qk, -1)
        h = num / jnp.maximum(jnp.abs(den), jnp.exp(-m_t))[..., None]
        b_last = b[..., -1]
        g = b_last[..., None] - b + li
        m_new = jnp.maximum(b_last + m, jnp.max(g, -1))
        decay = jnp.exp(b_last + m - m_new)
        wk = jnp.exp(g - m_new[..., None])[..., None] * k_
        C_new = decay[..., None, None] * C + jnp.einsum('bhsd,bhsv->bhdv', wk, v_)
        n_new = decay[..., None] * n + jnp.sum(wk, -2)
        return (C_new, n_new, m_new), h

    init = (jnp.zeros((bsz, nh, dqk, dv), jnp.float32), jnp.zeros((bsz, nh, dqk), jnp.float32),
            jnp.zeros((bsz, nh), jnp.float32))
    _, h = lax.scan(step, init, (qc, kc, vc, ic, fc))
    return jnp.moveaxis(h, 0, 2).reshape(bsz, nh, seq, dv)


def mlstm_mixer(x, w_in, b_gates, norm_g, w_o):
    bsz, seq, _ = x.shape
    qk_w = ML_HEADS * ML_DQK
    z = x @ w_in

    def heads(a, dh):
        return a.reshape(bsz, seq, ML_HEADS, dh).transpose(0, 2, 1, 3).astype(jnp.float32)

    q = heads(z[..., :qk_w], ML_DQK) * ML_DQK ** -0.5
    k = heads(z[..., qk_w:2 * qk_w], ML_DQK)
    v = heads(z[..., 2 * qk_w:2 * qk_w + D_MODEL], ML_DV)
    o = jax.nn.sigmoid(z[..., 2 * qk_w + D_MODEL:2 * qk_w + 2 * D_MODEL].astype(jnp.float32))
    gates = (z[..., 2 * qk_w + 2 * D_MODEL:].astype(jnp.float32) + b_gates.astype(jnp.float32))
    gates = gates.reshape(bsz, seq, 4, ML_HEADS).transpose(2, 0, 3, 1)
    i_fw, f_fw, i_bw, f_bw = gates[0], gates[1], gates[2], gates[3]
    h_fw = mlstm_scan(q, k, v, i_fw, jax.nn.log_sigmoid(f_fw))
    flip = lambda a: jnp.flip(a, axis=2)
    h_bw = flip(mlstm_scan(flip(q), flip(k), flip(v), flip(i_bw), flip(jax.nn.log_sigmoid(f_bw))))
    h = (h_fw + h_bw).transpose(0, 2, 1, 3)
    h = rms_norm(h, norm_g.reshape(ML_HEADS, ML_DV)).reshape(bsz, seq, D_MODEL)
    return (o * h).astype(x.dtype) @ w_o


def setup_inputs(seed: int = 0) -> dict:
    key = jax.random.key(seed)
    ks = jax.random.split(key, 24)
    f32 = jnp.float32

    def dense(k, shape, scale=1.0):
        return jax.random.normal(k, shape, f32) * (shape[-2] ** -0.5) * scale

    def gain(k, shape):
        return 1.0 + 0.02 * jax.random.normal(k, shape, f32)

    def small(k, shape):
        return 0.02 * jax.random.normal(k, shape, f32)

    x = jax.random.normal(ks[0], (BATCH, SEQ, D_MODEL), f32)
    p = jax.random.normal(ks[1], (DEPTH, BATCH, SEQ, D_PLE), f32)
    na_scale = jnp.concatenate([jnp.ones((2 * D_MODEL,), f32), jnp.full((D_MODEL,), DN_BETA, f32)])
    na_w_qkv = dense(ks[2], (N_A, D_MODEL, 3 * D_MODEL)) * na_scale
    na_rpb = small(ks[3], (N_A, NA_HEADS, 2 * NA_KH_MAX - 1, 2 * NA_KW - 1))
    na_w_o = dense(ks[4], (N_A, D_MODEL, D_MODEL), DN_BETA)
    kvd = GQA_KV_HEADS * HEAD_DIM
    gq_scale = jnp.concatenate([jnp.ones((D_MODEL + kvd,), f32), jnp.full((kvd,), DN_BETA, f32)])
    gq_w_qkv = dense(ks[5], (N_B, D_MODEL, D_MODEL + 2 * kvd)) * gq_scale
    gq_q_norm = gain(ks[6], (N_B, HEAD_DIM))
    gq_k_norm = gain(ks[7], (N_B, HEAD_DIM))
    gq_w_o = dense(ks[8], (N_B, D_MODEL, D_MODEL), DN_BETA)
    ml_cols = 2 * ML_HEADS * ML_DQK + 2 * D_MODEL + 4 * ML_HEADS
    ml_w_in = dense(ks[9], (N_C, D_MODEL, ml_cols))
    f_bias = jnp.linspace(3.0, 6.0, ML_HEADS, dtype=f32)
    zero_b = jnp.zeros((ML_HEADS,), f32)
    ml_b_gates = (0.1 * jax.random.normal(ks[10], (N_C, 4, ML_HEADS), f32)
                  + jnp.stack([zero_b, f_bias, zero_b, f_bias])).reshape(N_C, 4 * ML_HEADS)
    ml_norm_g = gain(ks[11], (N_C, D_MODEL))
    ml_w_o = dense(ks[12], (N_C, D_MODEL, D_MODEL), DN_BETA)
    ln1_g = gain(ks[13], (DEPTH, D_MODEL))
    ln1_b = small(ks[14], (DEPTH, D_MODEL))
    w_ff1 = dense(ks[15], (DEPTH, D_MODEL, D_FF))
    w_ff2 = dense(ks[16], (DEPTH, D_FF, D_MODEL), DN_BETA)
    ln2_g = gain(ks[17], (DEPTH, D_MODEL))
    ln2_b = small(ks[18], (DEPTH, D_MODEL))
    w_ple_gate = dense(ks[19], (DEPTH, D_MODEL, D_MODEL))
    w_ple_proj = dense(ks[20], (DEPTH, D_PLE, D_MODEL), DN_BETA)
    return {'x': x, 'p': p, 'na_w_qkv': na_w_qkv, 'na_rpb': na_rpb, 'na_w_o': na_w_o,
            'gq_w_qkv': gq_w_qkv, 'gq_q_norm': gq_q_norm, 'gq_k_norm': gq_k_norm, 'gq_w_o': gq_w_o,
            'ml_w_in': ml_w_in, 'ml_b_gates': ml_b_gates, 'ml_norm_g': ml_norm_g, 'ml_w_o': ml_w_o,
            'ln1_g': ln1_g, 'ln1_b': ln1_b, 'w_ff1': w_ff1, 'w_ff2': w_ff2, 'ln2_g': ln2_g, 'ln2_b': ln2_b,
            'w_ple_gate': w_ple_gate, 'w_ple_proj': w_ple_proj}


def reference(x, p, na_w_qkv, na_rpb, na_w_o, gq_w_qkv, gq_q_norm, gq_k_norm, gq_w_o,
              ml_w_in, ml_b_gates, ml_norm_g, ml_w_o, ln1_g, ln1_b, w_ff1, w_ff2, ln2_g, ln2_b,
              w_ple_gate, w_ple_proj):
    for i in range(DEPTH):
        kind, j = i % N_MIXERS, i // N_MIXERS
        if kind == 0:
            mix = neighborhood_attention(x, na_w_qkv[j], na_rpb[j], na_w_o[j])
        elif kind == 1:
            mix = gqa_axial(x, gq_w_qkv[j], gq_q_norm[j], gq_k_norm[j], gq_w_o[j])
        else:
            mix = mlstm_mixer(x, ml_w_in[j], ml_b_gates[j], ml_norm_g[j], ml_w_o[j])
        x = layer_norm(DN_ALPHA * x + mix, ln1_g[i], ln1_b[i])
        hid = jnp.square(jax.nn.relu(x @ w_ff1[i]))
        ple = jax.nn.sigmoid(x @ w_ple_gate[i]) * (p[i] @ w_ple_proj[i])
        x = layer_norm(DN_ALPHA * x + hid @ w_ff2[i] + ple, ln2_g[i], ln2_b[i])
    return x
```

```python
import functools

import jax
import jax.numpy as jnp
import numpy as np
from jax import lax
from jax.experimental import pallas as pl
from jax.experimental.pallas import tpu as pltpu

F32 = jnp.float32
BF16 = jnp.bfloat16

D_MODEL = 1024
DEPTH = 4
GRID_W = 64
HEAD_DIM = 64
D_FF = 4 * D_MODEL
D_PLE = 256
NA_HEADS = 16
NA_KH = 8
NA_KW = 16
GQA_KV_HEADS = 4
GQA_GROUP = 4
ROPE_THETA = 10000.0
ML_HEADS = 8
ML_DV = 128
ML_DQK = 64
ML_CHUNK = 64
DN_ALPHA = (2 * DEPTH) ** 0.25
EPS = 1e-6

V7X_VMEM_BYTES = 64 * 1024 * 1024
V7X_LANES = 128
SLAB = 2 * V7X_LANES

NT_DIMS = (((1,), (1,)), ((), ()))


def _vmem_limit(*byte_counts):
    est = int(sum(byte_counts) * 1.5) + (4 << 20)
    return min(est, V7X_VMEM_BYTES - (6 << 20))


def _nbytes(shape, dtype):
    return int(np.prod(shape)) * jnp.dtype(dtype).itemsize


def _const_spec(shape):
    nd = len(shape)
    return pl.BlockSpec(shape, lambda *_: (0,) * nd, pipeline_mode=pl.Buffered(1))


def _layer_norm(y, g, b):
    mu = jnp.mean(y, -1, keepdims=True)
    yc = y - mu
    var = jnp.mean(yc * yc, -1, keepdims=True)
    return yc * lax.rsqrt(var + EPS) * g + b


def _proj_kernel(x_ref, w_ref, o_ref, *, n_chunk):
    xb = x_ref[...].astype(BF16)
    for j in range(o_ref.shape[1] // n_chunk):
        sl = slice(j * n_chunk, (j + 1) * n_chunk)
        o_ref[:, sl] = jnp.dot(xb, w_ref[:, sl], preferred_element_type=F32).astype(BF16)


def _proj(x, w, tm=512, n_chunk=1024):
    t, d = x.shape
    n = w.shape[1]
    return pl.pallas_call(
        functools.partial(_proj_kernel, n_chunk=n_chunk),
        out_shape=jax.ShapeDtypeStruct((t, n), BF16),
        grid=(t // tm,),
        in_specs=[pl.BlockSpec((tm, d), lambda i: (i, 0)), _const_spec((d, n))],
        out_specs=pl.BlockSpec((tm, n), lambda i: (i, 0)),
        compiler_params=pltpu.CompilerParams(
            dimension_semantics=("parallel",),
            vmem_limit_bytes=_vmem_limit(2 * _nbytes((tm, d), x.dtype), _nbytes((d, n), BF16),
                                         2 * _nbytes((tm, n), BF16), _nbytes((tm, n_chunk), F32))),
        name="proj_plain",
    )(x, w)


def _na_kernel(q_ref, k_ref, v_ref, bias_ref, o_ref, vhat_ref, *, rows):
    lane_head = lax.broadcasted_iota(jnp.int32, (1, SLAB), 1) // HEAD_DIM
    v = v_ref[...]
    for h in range(4):
        vhat_ref[h] = jnp.where(lane_head == h, v, jnp.zeros_like(v))
    win = NA_KH * GRID_W

    def body(r, carry):
        r0 = jnp.clip(r - NA_KH // 2, 0, rows - NA_KH)
        delta = r - r0
        q = q_ref[pl.ds(pl.multiple_of(r * GRID_W, GRID_W), GRID_W), :]
        kwin = k_ref[pl.ds(pl.multiple_of(r0 * GRID_W, GRID_W), win), :]
        qs = jnp.concatenate([jnp.where(lane_head == h, q, jnp.zeros_like(q)) for h in range(4)], axis=0)
        s = lax.dot_general(qs, kwin, NT_DIMS, preferred_element_type=F32) * (HEAD_DIM ** -0.5)
        s = s + bias_ref[0, delta]
        m = jnp.max(s, -1, keepdims=True)
        e = jnp.exp(s - m)
        p = (e * (1.0 / jnp.sum(e, -1, keepdims=True))).astype(BF16)
        acc = jnp.zeros((GRID_W, SLAB), F32)
        for h in range(4):
            vwin = vhat_ref[h, pl.ds(pl.multiple_of(r0 * GRID_W, GRID_W), win), :]
            acc = acc + jnp.dot(p[h * GRID_W:(h + 1) * GRID_W], vwin, preferred_element_type=F32)
        o_ref[pl.ds(pl.multiple_of(r * GRID_W, GRID_W), GRID_W), :] = acc.astype(BF16)
        return carry

    lax.fori_loop(0, rows, body, 0, unroll=2)


def _na_bias_table(rpb):
    col = np.arange(GRID_W)
    c0 = np.clip(col - NA_KW // 2, 0, GRID_W - NA_KW)
    col_in = (col[None, :] >= c0[:, None]) & (col[None, :] < c0[:, None] + NA_KW)
    dc = np.clip(col[None, :] - col[:, None], 1 - NA_KW, NA_KW - 1) + NA_KW - 1
    delta = np.arange(NA_KH)
    a = np.arange(NA_KH)
    dr = a[None, :] - delta[:, None] + NA_KH - 1
    b = rpb[:, dr[:, None, :, None], dc[None, :, None, :]].astype(F32)
    b = jnp.where(col_in[None, None, :, None, :], b, -jnp.inf)
    b = b.reshape(NA_HEADS // 4, 4, NA_KH, GRID_W, NA_KH * GRID_W)
    return b.transpose(0, 2, 1, 3, 4).reshape(NA_HEADS // 4, NA_KH, 4 * GRID_W, NA_KH * GRID_W)


def _na_attention(qkv, bias, bsz, seq):
    n_slab = D_MODEL // SLAB
    rows = seq // GRID_W
    blk = (seq, SLAB)
    return pl.pallas_call(
        functools.partial(_na_kernel, rows=rows),
        out_shape=jax.ShapeDtypeStruct((bsz * seq, D_MODEL), BF16),
        grid=(n_slab, bsz),
        in_specs=[pl.BlockSpec(blk, lambda s, b: (b, s)),
                  pl.BlockSpec(blk, lambda s, b: (b, n_slab + s)),
                  pl.BlockSpec(blk, lambda s, b: (b, 2 * n_slab + s)),
                  pl.BlockSpec((1,) + bias.shape[1:], lambda s, b: (s, 0, 0, 0))],
        out_specs=pl.BlockSpec(blk, lambda s, b: (b, s)),
        scratch_shapes=[pltpu.VMEM((4, seq, SLAB), BF16)],
        compiler_params=pltpu.CompilerParams(
            dimension_semantics=("parallel", "parallel"),
            vmem_limit_bytes=_vmem_limit(8 * _nbytes(blk, BF16), 2 * _nbytes(bias.shape[1:], F32),
                                         4 * _nbytes(blk, BF16))),
        name="na_attention",
    )(qkv, qkv, qkv, bias)


def _gqa_proj_kernel(x_ref, w_ref, cos_ref, sin_ref, gq_ref, gk_ref, ones_ref, o_ref):
    xb = x_ref[...]
    cos = cos_ref[...]
    sin = sin_ref[...]
    ones = ones_ref[...]
    n_norm = 2 * D_MODEL // (2 * SLAB)
    for pair in range(n_norm):
        z = jnp.dot(xb, w_ref[:, pair * 2 * SLAB:(pair + 1) * 2 * SLAB], preferred_element_type=F32)
        halves = [(z[:, j * SLAB:j * SLAB + V7X_LANES], z[:, j * SLAB + V7X_LANES:(j + 1) * SLAB]) for j in range(2)]
        ss = jnp.concatenate([a * a + b * b for a, b in halves], axis=1)
        hi = ss.astype(BF16)
        lo = (ss - hi.astype(F32)).astype(BF16)
        ms = (jnp.dot(hi, ones, preferred_element_type=F32)
              + jnp.dot(lo, ones, preferred_element_type=F32)) * (1.0 / HEAD_DIM)
        rs = lax.rsqrt(ms + EPS)
        is_q = pair < n_norm // 2
        g_ref = gq_ref if is_q else gk_ref
        for j, (a, b) in enumerate(halves):
            r = rs[:, j * V7X_LANES:(j + 1) * V7X_LANES]
            an = a * r * g_ref[:, :V7X_LANES]
            bn = b * r * g_ref[:, V7X_LANES:]
            oa = an * cos - bn * sin
            ob = an * sin + bn * cos
            if is_q:
                oa = oa * (HEAD_DIM ** -0.5)
                ob = ob * (HEAD_DIM ** -0.5)
            c0 = (pair * 2 + j) * SLAB
            o_ref[:, c0:c0 + V7X_LANES] = oa.astype(BF16)
            o_ref[:, c0 + V7X_LANES:c0 + SLAB] = ob.astype(BF16)
    c0 = 2 * D_MODEL
    o_ref[:, c0:] = jnp.dot(xb, w_ref[:, c0:], preferred_element_type=F32).astype(BF16)


def _gqa_layout():
    kvd = GQA_KV_HEADS * HEAD_DIM
    half = HEAD_DIM // 2
    q_cols, k_cols, v_cols = [], [], []
    for g in range(GQA_KV_HEADS):
        for parity in range(2):
            for h in range(GQA_GROUP):
                q_cols += [(g * GQA_GROUP + h) * HEAD_DIM + 2 * i + parity for i in range(half)]
                k_cols += [D_MODEL + g * HEAD_DIM + 2 * i + parity for i in range(half)]
        for h in range(GQA_GROUP):
            v_cols += [D_MODEL + kvd + g * HEAD_DIM + d for d in range(HEAD_DIM)]
    cols = np.asarray(q_cols + k_cols + v_cols, np.int32)
    gain_idx = np.asarray([2 * i + parity for parity in range(2) for _ in range(GQA_GROUP) for i in range(half)], np.int32)
    blk = np.arange(SLAB) // half
    ones = (blk[:, None] == blk[None, :]).astype(np.float32)
    return cols, gain_idx, ones


def _rope_tables(seq):
    t = jnp.arange(seq)
    row = (t // GRID_W).astype(F32)
    col = (t % GRID_W).astype(F32)
    n_pairs = HEAD_DIM // 4
    inv = ROPE_THETA ** (-jnp.arange(n_pairs, dtype=F32) / n_pairs)
    ang = jnp.concatenate([row[:, None] * inv, col[:, None] * inv], -1)
    return jnp.tile(jnp.cos(ang), (1, GQA_GROUP)), jnp.tile(jnp.sin(ang), (1, GQA_GROUP))


def _gqa_proj(x16, w, cos4, sin4, gq, gk, ones, seq, tm=256):
    t, d = x16.shape
    n = w.shape[1]
    per_seq = seq // tm
    return pl.pallas_call(
        _gqa_proj_kernel,
        out_shape=jax.ShapeDtypeStruct((t, n), BF16),
        grid=(t // tm,),
        in_specs=[pl.BlockSpec((tm, d), lambda i: (i, 0)), _const_spec((d, n)),
                  pl.BlockSpec((tm, V7X_LANES), lambda i: (i % per_seq, 0)),
                  pl.BlockSpec((tm, V7X_LANES), lambda i: (i % per_seq, 0)),
                  _const_spec((1, SLAB)), _const_spec((1, SLAB)), _const_spec((SLAB, SLAB))],
        out_specs=pl.BlockSpec((tm, n), lambda i: (i, 0)),
        compiler_params=pltpu.CompilerParams(
            dimension_semantics=("parallel",),
            vmem_limit_bytes=_vmem_limit(2 * _nbytes((tm, d), BF16), _nbytes((d, n), BF16),
                                         2 * _nbytes((tm, n), BF16), 6 * _nbytes((tm, 2 * SLAB), F32))),
        name="gqa_proj",
    )(x16, w, cos4, sin4, gq, gk, ones)


def _gqa_attn_kernel(q_ref, k_ref, v_ref, o_ref, vhat_ref):
    lane = lax.broadcasted_iota(jnp.int32, (1, SLAB), 1)

    @pl.when(pl.program_id(2) == 0)
    def _():
        v = v_ref[...]
        for h in range(GQA_GROUP):
            vhat_ref[h] = jnp.where(lane // HEAD_DIM == h, v, jnp.zeros_like(v))

    q = q_ref[...]
    k = k_ref[...]
    q_head = (lane % V7X_LANES) // (HEAD_DIM // 2)
    out = jnp.zeros(o_ref.shape, F32)
    for h in range(GQA_GROUP):
        qm = jnp.where(q_head == h, q, jnp.zeros_like(q))
        s = lax.dot_general(qm, k, NT_DIMS, preferred_element_type=F32)
        m = jnp.max(s, -1, keepdims=True)
        e = jnp.exp(s - m)
        inv_l = 1.0 / jnp.sum(e, -1, keepdims=True)
        out = out + jnp.dot(e.astype(BF16), vhat_ref[h], preferred_element_type=F32) * inv_l
    o_ref[...] = out.astype(BF16)


def _gqa_attention(qkv, bsz, seq, tq=512):
    n_slab = D_MODEL // SLAB
    nq = seq // tq
    return pl.pallas_call(
        _gqa_attn_kernel,
        out_shape=jax.ShapeDtypeStruct((bsz * seq, D_MODEL), BF16),
        grid=(bsz, n_slab, nq),
        in_specs=[pl.BlockSpec((tq, SLAB), lambda b, g, i: (b * nq + i, g)),
                  pl.BlockSpec((seq, SLAB), lambda b, g, i: (b, n_slab + g)),
                  pl.BlockSpec((seq, SLAB), lambda b, g, i: (b, 2 * n_slab + g))],
        out_specs=pl.BlockSpec((tq, SLAB), lambda b, g, i: (b * nq + i, g)),
        scratch_shapes=[pltpu.VMEM((GQA_GROUP, seq, SLAB), BF16)],
        compiler_params=pltpu.CompilerParams(
            dimension_semantics=("parallel", "parallel", "arbitrary"),
            vmem_limit_bytes=_vmem_limit(4 * _nbytes((tq, SLAB), BF16), 8 * _nbytes((seq, SLAB), BF16),
                                         3 * _nbytes((tq, seq), F32))),
        name="gqa_attention",
    )(qkv, qkv, qkv)


def _ml_proj_kernel(x_ref, w_ref, wkt_ref, wgt_ref, bg_ref, q_ref, k_ref, v_ref, o_ref, kt_ref, gt_ref):
    xb = x_ref[...]
    qk_w = ML_HEADS * ML_DQK
    q_ref[...] = (jnp.dot(xb, w_ref[:, :qk_w], preferred_element_type=F32) * (ML_DQK ** -0.5)).astype(BF16)
    k_ref[...] = jnp.dot(xb, w_ref[:, qk_w:2 * qk_w], preferred_element_type=F32).astype(BF16)
    v_ref[...] = jnp.dot(xb, w_ref[:, 2 * qk_w:2 * qk_w + D_MODEL], preferred_element_type=F32).astype(BF16)
    o_ref[...] = jax.nn.sigmoid(jnp.dot(xb, w_ref[:, 2 * qk_w + D_MODEL:], preferred_element_type=F32))
    kt = lax.dot_general(wkt_ref[...], xb, NT_DIMS, preferred_element_type=F32)
    for j in range(kt_ref.shape[1]):
        kt_ref[0, j] = kt[:, j * ML_CHUNK:(j + 1) * ML_CHUNK]
    gt_ref[0] = lax.dot_general(wgt_ref[...], xb, NT_DIMS, preferred_element_type=F32) + bg_ref[...]


def _ml_proj(x16, w, wkt, wgt, bg, bsz, seq, tm=512):
    t, d = x16.shape
    qk_w = ML_HEADS * ML_DQK
    per_seq = seq // tm
    n_gate = wgt.shape[0]
    row = lambda i: (i, 0)
    return pl.pallas_call(
        _ml_proj_kernel,
        out_shape=(jax.ShapeDtypeStruct((t, qk_w), BF16), jax.ShapeDtypeStruct((t, qk_w), BF16),
                   jax.ShapeDtypeStruct((t, D_MODEL), BF16), jax.ShapeDtypeStruct((t, D_MODEL), F32),
                   jax.ShapeDtypeStruct((bsz, seq // ML_CHUNK, qk_w, ML_CHUNK), F32),
                   jax.ShapeDtypeStruct((bsz, n_gate, seq), F32)),
        grid=(t // tm,),
        in_specs=[pl.BlockSpec((tm, d), row), _const_spec(w.shape), _const_spec(wkt.shape),
                  _const_spec(wgt.shape), _const_spec(bg.shape)],
        out_specs=(pl.BlockSpec((tm, qk_w), row), pl.BlockSpec((tm, qk_w), row),
                   pl.BlockSpec((tm, D_MODEL), row), pl.BlockSpec((tm, D_MODEL), row),
                   pl.BlockSpec((1, tm // ML_CHUNK, qk_w, ML_CHUNK), lambda i: (i // per_seq, i % per_seq, 0, 0)),
                   pl.BlockSpec((1, n_gate, tm), lambda i: (i // per_seq, 0, i % per_seq))),
        compiler_params=pltpu.CompilerParams(
            dimension_semantics=("parallel",),
            vmem_limit_bytes=_vmem_limit(2 * _nbytes((tm, d), BF16), _nbytes(w.shape, BF16), _nbytes(wkt.shape, BF16),
                                         2 * _nbytes((tm, 2 * qk_w + D_MODEL), BF16), 2 * _nbytes((tm, D_MODEL), F32),
                                         4 * _nbytes((qk_w, tm), F32), 2 * _nbytes((tm, D_MODEL), F32))),
        name="mlstm_proj",
    )(x16, w, wkt, wgt, bg)


def _log_sigmoid(x):
    return jnp.minimum(x, 0.0) - jnp.log1p(jnp.exp(-jnp.abs(x)))


def _mlstm_chunk(qc, kc, ktc, vc, li_row, f_row, c_state, n_state, m_state, tri, tri_t, eye):
    lf_row = _log_sigmoid(f_row)
    lf_col = jnp.sum(jnp.where(eye, lf_row, 0.0), axis=1, keepdims=True)
    li_col = jnp.sum(jnp.where(eye, li_row, 0.0), axis=1, keepdims=True)
    b_col = jnp.sum(jnp.where(tri, lf_row, 0.0), axis=1, keepdims=True)
    b_row = jnp.sum(jnp.where(tri_t, lf_col, 0.0), axis=0, keepdims=True)
    a_row = li_row - b_row
    cmax = jnp.max(jnp.where(tri, a_row, -jnp.inf), axis=1, keepdims=True)
    g_col = jnp.maximum(m_state, cmax)
    w = jnp.where(tri, jnp.exp(a_row - g_col), 0.0)
    s_inter = jnp.exp(m_state - g_col)
    qk = lax.dot_general(qc, kc, NT_DIMS, preferred_element_type=F32) * w
    c_ext = jnp.concatenate([c_state, n_state], axis=1).astype(BF16)
    q_c = jnp.dot(qc, c_ext, preferred_element_type=F32)
    num = s_inter * q_c[:, :ML_DV] + jnp.dot(qk.astype(BF16), vc, preferred_element_type=F32)
    den = s_inter * q_c[:, ML_DV:] + jnp.sum(qk, axis=1, keepdims=True)
    h = num / jnp.maximum(jnp.abs(den), jnp.exp(-(b_col + g_col)))
    g_end = jnp.maximum(m_state, jnp.max(a_row, axis=1, keepdims=True))
    m_new = jnp.sum(lf_row, axis=1, keepdims=True) + g_end
    decay = jnp.exp(m_state - g_end)
    wkt = jnp.exp(a_row - g_end) * ktc
    c_new = decay * c_state + jnp.dot(wkt.astype(BF16), vc, preferred_element_type=F32)
    n_new = decay * n_state + jnp.sum(wkt, axis=1, keepdims=True)
    return h, c_new, n_new, m_new


def _mlstm_kernel(q_ref, k_ref, kt_ref, v_ref, o_ref, g_ref, ng_ref, out_ref, hfw_ref, hbw_ref, *, n_chunk):
    pair = pl.program_id(1)
    t_idx = lax.broadcasted_iota(jnp.int32, (ML_CHUNK, ML_CHUNK), 0)
    s_idx = lax.broadcasted_iota(jnp.int32, (ML_CHUNK, ML_CHUNK), 1)
    eye = t_idx == s_idx
    lower = s_idx <= t_idx
    upper = s_idx >= t_idx
    masks = ((lower, upper), (upper, lower))

    def body(c, carry):
        new_carry = []
        for direction in range(2):
            chunk = c if direction == 0 else n_chunk - 1 - c
            rows = pl.ds(pl.multiple_of(chunk * ML_CHUNK, ML_CHUNK), ML_CHUNK)
            tri, tri_t = masks[direction]
            for hh in range(2):
                head = pair * 2 + hh
                c_state, n_state, m_state = carry[direction * 2 + hh]
                qc = q_ref[rows, hh * ML_DQK:(hh + 1) * ML_DQK]
                kc = k_ref[rows, hh * ML_DQK:(hh + 1) * ML_DQK]
                vc = v_ref[rows, hh * ML_DV:(hh + 1) * ML_DV]
                ktc = kt_ref[0, chunk, hh * ML_DQK:(hh + 1) * ML_DQK, :]
                li_row = g_ref[0, direction * 2 * ML_HEADS + head, pl.ds(chunk, 1), :]
                f_row = g_ref[0, (direction * 2 + 1) * ML_HEADS + head, pl.ds(chunk, 1), :]
                h, c_new, n_new, m_new = _mlstm_chunk(qc, kc, ktc, vc, li_row, f_row, c_state, n_state, m_state,
                                                       tri, tri_t, eye)
                dst = hfw_ref if direction == 0 else hbw_ref
                dst[rows, hh * ML_DV:(hh + 1) * ML_DV] = h
                new_carry.append((c_new, n_new, m_new))
        return tuple(new_carry)

    init = tuple((jnp.zeros((ML_DQK, ML_DV), F32), jnp.zeros((ML_DQK, ML_DV), F32), jnp.zeros((1, 1), F32))
                 for _ in range(4))
    lax.fori_loop(0, n_chunk, body, init)

    for hh in range(2):
        cols = slice(hh * ML_DV, (hh + 1) * ML_DV)
        h = hfw_ref[:, cols] + hbw_ref[:, cols]
        ms = jnp.mean(h * h, -1, keepdims=True)
        hn = h * lax.rsqrt(ms + EPS) * ng_ref[:, cols]
        out_ref[:, cols] = (o_ref[:, cols] * hn).astype(BF16)


def _mlstm(q, k, kt, v, o, gates, norm_g, bsz, seq):
    n_pair = ML_HEADS // 2
    n_chunk = seq // ML_CHUNK
    pair_w = 2 * ML_DV
    return pl.pallas_call(
        functools.partial(_mlstm_kernel, n_chunk=n_chunk),
        out_shape=jax.ShapeDtypeStruct((bsz * seq, D_MODEL), BF16),
        grid=(bsz, n_pair),
        in_specs=[pl.BlockSpec((seq, 2 * ML_DQK), lambda b, p: (b, p)),
                  pl.BlockSpec((seq, 2 * ML_DQK), lambda b, p: (b, p)),
                  pl.BlockSpec((1, n_chunk, 2 * ML_DQK, ML_CHUNK), lambda b, p: (b, 0, p, 0)),
                  pl.BlockSpec((seq, pair_w), lambda b, p: (b, p)),
                  pl.BlockSpec((seq, pair_w), lambda b, p: (b, p)),
                  pl.BlockSpec((1,) + gates.shape[1:], lambda b, p: (b, 0, 0, 0)),
                  pl.BlockSpec((1, pair_w), lambda b, p: (0, p))],
        out_specs=pl.BlockSpec((seq, pair_w), lambda b, p: (b, p)),
        scratch_shapes=[pltpu.VMEM((seq, pair_w), F32), pltpu.VMEM((seq, pair_w), F32)],
        compiler_params=pltpu.CompilerParams(
            dimension_semantics=("parallel", "parallel"),
            vmem_limit_bytes=_vmem_limit(4 * _nbytes((seq, 2 * ML_DQK), BF16), 2 * _nbytes((seq, 2 * ML_DQK), F32),
                                         4 * _nbytes((seq, pair_w), BF16), 2 * _nbytes((seq, pair_w), F32),
                                         2 * _nbytes(gates.shape[1:], F32), 4 * _nbytes((seq, pair_w), F32))),
        name="mlstm_scan",
    )(q, k, kt, v, o, gates, norm_g)


def _out_ln_kernel(mix_ref, x_ref, w_ref, g_ref, b_ref, o32_ref, o16_ref):
    y = jnp.dot(mix_ref[...], w_ref[...], preferred_element_type=F32) + DN_ALPHA * x_ref[...]
    z = _layer_norm(y, g_ref[...], b_ref[...])
    o32_ref[...] = z
    o16_ref[...] = z.astype(BF16)


def _out_ln(mix, x, w, g, b, tm=512):
    t, d = x.shape
    row = lambda i: (i, 0)
    return pl.pallas_call(
        _out_ln_kernel,
        out_shape=(jax.ShapeDtypeStruct((t, d), F32), jax.ShapeDtypeStruct((t, d), BF16)),
        grid=(t // tm,),
        in_specs=[pl.BlockSpec((tm, d), row), pl.BlockSpec((tm, d), row), _const_spec((d, d)),
                  _const_spec((1, d)), _const_spec((1, d))],
        out_specs=(pl.BlockSpec((tm, d), row), pl.BlockSpec((tm, d), row)),
        compiler_params=pltpu.CompilerParams(
            dimension_semantics=("parallel",),
            vmem_limit_bytes=_vmem_limit(4 * _nbytes((tm, d), BF16), 4 * _nbytes((tm, d), F32), _nbytes((d, d), BF16),
                                         3 * _nbytes((tm, d), F32))),
        name="out_proj_ln",
    )(mix, x, w, g, b)


def _ffn_kernel(x16_ref, x32_ref, p_ref, w1_ref, w2_ref, wg_ref, wp_ref, g_ref, b_ref, o32_ref, o16_ref, *, ff_chunk):
    xb = x16_ref[...]
    acc = jnp.zeros(x32_ref.shape, F32)
    for c in range(w1_ref.shape[1] // ff_chunk):
        sl = slice(c * ff_chunk, (c + 1) * ff_chunk)
        h = jnp.maximum(jnp.dot(xb, w1_ref[:, sl], preferred_element_type=F32), 0.0)
        acc = acc + jnp.dot((h * h).astype(BF16), w2_ref[sl, :], preferred_element_type=F32)
    gate = jax.nn.sigmoid(jnp.dot(xb, wg_ref[...], preferred_element_type=F32))
    ple = gate * jnp.dot(p_ref[...].astype(BF16), wp_ref[...], preferred_element_type=F32)
    z = _layer_norm(DN_ALPHA * x32_ref[...] + acc + ple, g_ref[...], b_ref[...])
    o32_ref[...] = z
    o16_ref[...] = z.astype(BF16)


def _ffn(x16, x32, p, w1, w2, wg, wp, g, b, tm=512, ff_chunk=1024):
    t, d = x32.shape
    row = lambda i: (i, 0)
    return pl.pallas_call(
        functools.partial(_ffn_kernel, ff_chunk=ff_chunk),
        out_shape=(jax.ShapeDtypeStruct((t, d), F32), jax.ShapeDtypeStruct((t, d), BF16)),
        grid=(t // tm,),
        in_specs=[pl.BlockSpec((tm, d), row), pl.BlockSpec((tm, d), row), pl.BlockSpec((tm, p.shape[1]), row),
                  _const_spec(w1.shape), _const_spec(w2.shape), _const_spec(wg.shape), _const_spec(wp.shape),
                  _const_spec((1, d)), _const_spec((1, d))],
        out_specs=(pl.BlockSpec((tm, d), row), pl.BlockSpec((tm, d), row)),
        compiler_params=pltpu.CompilerParams(
            dimension_semantics=("parallel",),
            vmem_limit_bytes=_vmem_limit(4 * _nbytes((tm, d), BF16), 4 * _nbytes((tm, d), F32),
                                         2 * _nbytes((tm, p.shape[1]), F32), _nbytes(w1.shape, BF16),
                                         _nbytes(w2.shape, BF16), _nbytes(wg.shape, BF16), _nbytes(wp.shape, BF16),
                                         2 * _nbytes((tm, ff_chunk), F32), 3 * _nbytes((tm, d), F32))),
        name="ffn_ple_ln",
    )(x16, x32, p, w1, w2, wg, wp, g, b)


def kernel(x, p, na_w_qkv, na_rpb, na_w_o, gq_w_qkv, gq_q_norm, gq_k_norm, gq_w_o, ml_w_in, ml_b_gates, ml_norm_g,
           ml_w_o, ln1_g, ln1_b, w_ff1, w_ff2, ln2_g, ln2_b, w_ple_gate, w_ple_proj):
    bsz, seq, d = x.shape
    assert d == D_MODEL and seq % (NA_KH * GRID_W) == 0 and p.shape == (DEPTH, bsz, seq, D_PLE)
    t = bsz * seq
    x32 = x.reshape(t, d)
    x16 = None
    gq_cols, gq_gain_idx, gq_ones = _gqa_layout()
    qk_w = ML_HEADS * ML_DQK
    for i in range(DEPTH):
        kind, j = i % 3, i // 3
        if kind == 0:
            qkv = _proj(x32 if x16 is None else x16, na_w_qkv[j].astype(BF16))
            mix = _na_attention(qkv, _na_bias_table(na_rpb[j]), bsz, seq)
            w_o = na_w_o[j]
        elif kind == 1:
            w = jnp.take(gq_w_qkv[j], gq_cols, axis=1).astype(BF16)
            cos4, sin4 = _rope_tables(seq)
            gq = gq_q_norm[j][gq_gain_idx][None, :]
            gk = gq_k_norm[j][gq_gain_idx][None, :]
            qkv = _gqa_proj(x16, w, cos4, sin4, gq, gk, jnp.asarray(gq_ones, BF16), seq)
            mix = _gqa_attention(qkv, bsz, seq)
            w_o = gq_w_o[j]
        else:
            w_in = ml_w_in[j]
            n_main = 2 * qk_w + 2 * D_MODEL
            wkt = w_in[:, qk_w:2 * qk_w].T.astype(BF16)
            wgt = w_in[:, n_main:].T.astype(BF16)
            q, k, v, o, kt, gt = _ml_proj(x16, w_in[:, :n_main].astype(BF16), wkt, wgt, ml_b_gates[j][:, None], bsz, seq)
            gates = gt.reshape(bsz, gt.shape[1], seq // ML_CHUNK, ML_CHUNK)
            mix = _mlstm(q, k, kt, v, o, gates, ml_norm_g[j][None, :], bsz, seq)
            w_o = ml_w_o[j]
        x32, x16 = _out_ln(mix, x32, w_o.astype(BF16), ln1_g[i][None, :], ln1_b[i][None, :])
        x32, x16 = _ffn(x16, x32, p[i].reshape(t, D_PLE), w_ff1[i].astype(BF16), w_ff2[i].astype(BF16),
                        w_ple_gate[i].astype(BF16), w_ple_proj[i].astype(BF16), ln2_g[i][None, :], ln2_b[i][None, :])
    return x32.reshape(bsz, seq, d)
```

```python
import functools

import jax
import jax.numpy as jnp
import numpy as np
from jax import lax
from jax.experimental import pallas as pl
from jax.experimental.pallas import tpu as pltpu

F32 = jnp.float32
BF16 = jnp.bfloat16

D_MODEL = 1024
DEPTH = 4
GRID_W = 64
HEAD_DIM = 64
D_FF = 4 * D_MODEL
D_PLE = 256
NA_HEADS = 16
NA_KH = 8
NA_KW = 16
GQA_KV_HEADS = 4
GQA_GROUP = 4
ROPE_THETA = 10000.0
ML_HEADS = 8
ML_DV = 128
ML_DQK = 64
ML_CHUNK = 64
DN_ALPHA = (2 * DEPTH) ** 0.25
EPS = 1e-6

V7X_VMEM_BYTES = 64 * 1024 * 1024
V7X_LANES = 128
SLAB = 2 * V7X_LANES

NT_DIMS = (((1,), (1,)), ((), ()))


def _vmem_limit(*byte_counts):
    est = int(sum(byte_counts) * 1.5) + (4 << 20)
    return min(est, V7X_VMEM_BYTES - (6 << 20))


def _nbytes(shape, dtype):
    return int(np.prod(shape)) * jnp.dtype(dtype).itemsize


def _const_spec(shape):
    nd = len(shape)
    return pl.BlockSpec(shape, lambda *_: (0,) * nd, pipeline_mode=pl.Buffered(1))


def _layer_norm(y, g, b):
    mu = jnp.mean(y, -1, keepdims=True)
    yc = y - mu
    var = jnp.mean(yc * yc, -1, keepdims=True)
    return yc * lax.rsqrt(var + EPS) * g + b


def _proj_kernel(x_ref, w_ref, o_ref, *, n_chunk):
    xb = x_ref[...].astype(BF16)
    for j in range(o_ref.shape[1] // n_chunk):
        sl = slice(j * n_chunk, (j + 1) * n_chunk)
        o_ref[:, sl] = jnp.dot(xb, w_ref[:, sl], preferred_element_type=F32).astype(BF16)


def _proj(x, w, tm=512, n_chunk=1024):
    t, d = x.shape
    n = w.shape[1]
    return pl.pallas_call(
        functools.partial(_proj_kernel, n_chunk=n_chunk),
        out_shape=jax.ShapeDtypeStruct((t, n), BF16),
        grid=(t // tm,),
        in_specs=[pl.BlockSpec((tm, d), lambda i: (i, 0)), _const_spec((d, n))],
        out_specs=pl.BlockSpec((tm, n), lambda i: (i, 0)),
        compiler_params=pltpu.CompilerParams(
            dimension_semantics=("parallel",),
            vmem_limit_bytes=_vmem_limit(2 * _nbytes((tm, d), x.dtype), _nbytes((d, n), BF16),
                                         2 * _nbytes((tm, n), BF16), _nbytes((tm, n_chunk), F32))),
        name="proj_plain",
    )(x, w)


def _na_kernel(q_ref, k_ref, v_ref, bias_ref, o_ref, s_ref, p_ref, *, rows):
    lane_head = lax.broadcasted_iota(jnp.int32, (1, SLAB), 1) // HEAD_DIM
    win = NA_KH * GRID_W

    def window(r):
        r = jnp.clip(r, 0, rows - 1)
        r0 = jnp.clip(r - NA_KH // 2, 0, rows - NA_KH)
        return pl.multiple_of(r * GRID_W, GRID_W), pl.multiple_of(r0 * GRID_W, GRID_W), r - r0

    def scores(r, slot):
        q0, k0, delta = window(r)
        q = q_ref[pl.ds(q0, GRID_W), :]
        qs = jnp.concatenate([jnp.where(lane_head == h, q, jnp.zeros_like(q)) for h in range(4)], axis=0)
        s = lax.dot_general(qs, k_ref[pl.ds(k0, win), :], NT_DIMS, preferred_element_type=F32)
        s_ref[slot] = s * (HEAD_DIM ** -0.5) + bias_ref[0, delta]

    def softmax(slot):
        s = s_ref[slot]
        e = jnp.exp(s - jnp.max(s, -1, keepdims=True))
        p_ref[slot] = (e * (1.0 / jnp.sum(e, -1, keepdims=True))).astype(BF16)

    def weighted_values(r, slot):
        q0, k0, _ = window(r)
        pv = jnp.dot(p_ref[slot], v_ref[pl.ds(k0, win), :], preferred_element_type=F32)
        acc = jnp.zeros((GRID_W, SLAB), F32)
        for h in range(4):
            acc = jnp.where(lane_head == h, pv[h * GRID_W:(h + 1) * GRID_W], acc)
        o_ref[pl.ds(q0, GRID_W), :] = acc.astype(BF16)

    scores(0, 0)
    p_ref[1] = jnp.zeros(p_ref.shape[1:], BF16)

    def body(j, carry):
        r = 2 * j
        scores(r + 1, 1)
        softmax(0)
        weighted_values(r - 1, 1)
        scores(r + 2, 0)
        softmax(1)
        weighted_values(r, 0)
        return carry

    lax.fori_loop(0, rows // 2 + 1, body, 0)


def _na_bias_table(rpb):
    col = np.arange(GRID_W)
    c0 = np.clip(col - NA_KW // 2, 0, GRID_W - NA_KW)
    col_in = (col[None, :] >= c0[:, None]) & (col[None, :] < c0[:, None] + NA_KW)
    dc = np.clip(col[None, :] - col[:, None], 1 - NA_KW, NA_KW - 1) + NA_KW - 1
    rpb = rpb.astype(F32)
    by_col = jnp.zeros(rpb.shape[:2] + dc.shape, F32)
    for c in range(2 * NA_KW - 1):
        by_col = jnp.where(dc[None, None] == c, rpb[:, :, c][:, :, None, None], by_col)
    by_col = jnp.where(col_in[None, None], by_col, -jnp.inf)
    per_delta = [by_col[:, NA_KH - 1 - dl:2 * NA_KH - 1 - dl].transpose(0, 2, 1, 3)
                 .reshape(NA_HEADS, GRID_W, NA_KH * GRID_W) for dl in range(NA_KH)]
    b = jnp.stack(per_delta, axis=1).reshape(NA_HEADS // 4, 4, NA_KH, GRID_W, NA_KH * GRID_W)
    return b.transpose(0, 2, 1, 3, 4).reshape(NA_HEADS // 4, NA_KH, 4 * GRID_W, NA_KH * GRID_W)


def _na_attention(qkv, bias, bsz, seq):
    n_slab = D_MODEL // SLAB
    rows = seq // GRID_W
    blk = (seq, SLAB)
    return pl.pallas_call(
        functools.partial(_na_kernel, rows=rows),
        out_shape=jax.ShapeDtypeStruct((bsz * seq, D_MODEL), BF16),
        grid=(n_slab, bsz),
        in_specs=[pl.BlockSpec(blk, lambda s, b: (b, s)),
                  pl.BlockSpec(blk, lambda s, b: (b, n_slab + s)),
                  pl.BlockSpec(blk, lambda s, b: (b, 2 * n_slab + s)),
                  pl.BlockSpec((1,) + bias.shape[1:], lambda s, b: (s, 0, 0, 0))],
        out_specs=pl.BlockSpec(blk, lambda s, b: (b, s)),
        scratch_shapes=[pltpu.VMEM((2, 4 * GRID_W, NA_KH * GRID_W), F32),
                        pltpu.VMEM((2, 4 * GRID_W, NA_KH * GRID_W), BF16)],
        compiler_params=pltpu.CompilerParams(
            dimension_semantics=("parallel", "parallel"),
            vmem_limit_bytes=_vmem_limit(8 * _nbytes(blk, BF16), 2 * _nbytes(bias.shape[1:], F32),
                                         6 * _nbytes((4 * GRID_W, NA_KH * GRID_W), F32))),
        name="na_attention",
    )(qkv, qkv, qkv, bias)


def _gqa_proj_kernel(x_ref, w_ref, cos_ref, sin_ref, gq_ref, gk_ref, ones_ref, o_ref):
    xb = x_ref[...]
    cos = cos_ref[...]
    sin = sin_ref[...]
    ones = ones_ref[...]
    n_norm = 2 * D_MODEL // (2 * SLAB)
    for pair in range(n_norm):
        z = jnp.dot(xb, w_ref[:, pair * 2 * SLAB:(pair + 1) * 2 * SLAB], preferred_element_type=F32)
        halves = [(z[:, j * SLAB:j * SLAB + V7X_LANES], z[:, j * SLAB + V7X_LANES:(j + 1) * SLAB]) for j in range(2)]
        ss = jnp.concatenate([a * a + b * b for a, b in halves], axis=1)
        hi = ss.astype(BF16)
        lo = (ss - hi.astype(F32)).astype(BF16)
        ms = (jnp.dot(hi, ones, preferred_element_type=F32)
              + jnp.dot(lo, ones, preferred_element_type=F32)) * (1.0 / HEAD_DIM)
        rs = lax.rsqrt(ms + EPS)
        is_q = pair < n_norm // 2
        g_ref = gq_ref if is_q else gk_ref
        for j, (a, b) in enumerate(halves):
            r = rs[:, j * V7X_LANES:(j + 1) * V7X_LANES]
            an = a * r * g_ref[:, :V7X_LANES]
            bn = b * r * g_ref[:, V7X_LANES:]
            oa = an * cos - bn * sin
            ob = an * sin + bn * cos
            if is_q:
                oa = oa * (HEAD_DIM ** -0.5)
                ob = ob * (HEAD_DIM ** -0.5)
            c0 = (pair * 2 + j) * SLAB
            o_ref[:, c0:c0 + V7X_LANES] = oa.astype(BF16)
            o_ref[:, c0 + V7X_LANES:c0 + SLAB] = ob.astype(BF16)
    c0 = 2 * D_MODEL
    o_ref[:, c0:] = jnp.dot(xb, w_ref[:, c0:], preferred_element_type=F32).astype(BF16)


def _gqa_weight_layout(w):
    d = w.shape[0]
    kvd = GQA_KV_HEADS * HEAD_DIM
    half = HEAD_DIM // 2
    wq = w[:, :D_MODEL].reshape(d, GQA_KV_HEADS, GQA_GROUP, half, 2).transpose(0, 1, 4, 2, 3)
    wk = w[:, D_MODEL:D_MODEL + kvd].reshape(d, GQA_KV_HEADS, half, 2).transpose(0, 1, 3, 2)
    wk = jnp.broadcast_to(wk[:, :, :, None, :], (d, GQA_KV_HEADS, 2, GQA_GROUP, half))
    wv = w[:, D_MODEL + kvd:].reshape(d, GQA_KV_HEADS, 1, HEAD_DIM)
    wv = jnp.broadcast_to(wv, (d, GQA_KV_HEADS, GQA_GROUP, HEAD_DIM))
    return jnp.concatenate([wq.reshape(d, D_MODEL), wk.reshape(d, D_MODEL), wv.reshape(d, D_MODEL)], axis=1)


def _gqa_gain_layout(g):
    half = HEAD_DIM // 2
    return jnp.broadcast_to(g.reshape(half, 2).T[:, None, :], (2, GQA_GROUP, half)).reshape(1, SLAB)


def _group_sum_matrix():
    blk = np.arange(SLAB) // (HEAD_DIM // 2)
    return jnp.asarray(blk[:, None] == blk[None, :], BF16)


def _rope_tables(seq):
    t = jnp.arange(seq)
    row = (t // GRID_W).astype(F32)
    col = (t % GRID_W).astype(F32)
    n_pairs = HEAD_DIM // 4
    inv = ROPE_THETA ** (-jnp.arange(n_pairs, dtype=F32) / n_pairs)
    ang = jnp.concatenate([row[:, None] * inv, col[:, None] * inv], -1)
    return jnp.tile(jnp.cos(ang), (1, GQA_GROUP)), jnp.tile(jnp.sin(ang), (1, GQA_GROUP))


def _gqa_proj(x16, w, cos4, sin4, gq, gk, ones, seq, tm=256):
    t, d = x16.shape
    n = w.shape[1]
    per_seq = seq // tm
    return pl.pallas_call(
        _gqa_proj_kernel,
        out_shape=jax.ShapeDtypeStruct((t, n), BF16),
        grid=(t // tm,),
        in_specs=[pl.BlockSpec((tm, d), lambda i: (i, 0)), _const_spec((d, n)),
                  pl.BlockSpec((tm, V7X_LANES), lambda i: (i % per_seq, 0)),
                  pl.BlockSpec((tm, V7X_LANES), lambda i: (i % per_seq, 0)),
                  _const_spec((1, SLAB)), _const_spec((1, SLAB)), _const_spec((SLAB, SLAB))],
        out_specs=pl.BlockSpec((tm, n), lambda i: (i, 0)),
        compiler_params=pltpu.CompilerParams(
            dimension_semantics=("parallel",),
            vmem_limit_bytes=_vmem_limit(2 * _nbytes((tm, d), BF16), _nbytes((d, n), BF16),
                                         2 * _nbytes((tm, n), BF16), 6 * _nbytes((tm, 2 * SLAB), F32))),
        name="gqa_proj",
    )(x16, w, cos4, sin4, gq, gk, ones)


def _gqa_attn_kernel(q_ref, k_ref, v_ref, o_ref):
    lane = lax.broadcasted_iota(jnp.int32, (1, SLAB), 1)
    q = q_ref[...]
    k = k_ref[...]
    v = v_ref[...]
    q_head = (lane % V7X_LANES) // (HEAD_DIM // 2)
    out = jnp.zeros(o_ref.shape, F32)
    for h in range(GQA_GROUP):
        qm = jnp.where(q_head == h, q, jnp.zeros_like(q))
        s = lax.dot_general(qm, k, NT_DIMS, preferred_element_type=F32)
        m = jnp.max(s, -1, keepdims=True)
        e = jnp.exp(s - m)
        inv_l = 1.0 / jnp.sum(e, -1, keepdims=True)
        pv = jnp.dot(e.astype(BF16), v, preferred_element_type=F32) * inv_l
        out = jnp.where(lane // HEAD_DIM == h, pv, out)
    o_ref[...] = out.astype(BF16)


def _gqa_attention(qkv, bsz, seq, tq=512):
    n_slab = D_MODEL // SLAB
    nq = seq // tq
    return pl.pallas_call(
        _gqa_attn_kernel,
        out_shape=jax.ShapeDtypeStruct((bsz * seq, D_MODEL), BF16),
        grid=(bsz, n_slab, nq),
        in_specs=[pl.BlockSpec((tq, SLAB), lambda b, g, i: (b * nq + i, g)),
                  pl.BlockSpec((seq, SLAB), lambda b, g, i: (b, n_slab + g)),
                  pl.BlockSpec((seq, SLAB), lambda b, g, i: (b, 2 * n_slab + g))],
        out_specs=pl.BlockSpec((tq, SLAB), lambda b, g, i: (b * nq + i, g)),
        compiler_params=pltpu.CompilerParams(
            dimension_semantics=("parallel", "parallel", "parallel"),
            vmem_limit_bytes=_vmem_limit(4 * _nbytes((tq, SLAB), BF16), 8 * _nbytes((seq, SLAB), BF16),
                                         3 * _nbytes((tq, seq), F32))),
        name="gqa_attention",
    )(qkv, qkv, qkv)


def _ml_proj_kernel(x_ref, w_ref, wkt_ref, wgt_ref, bg_ref, q_ref, k_ref, v_ref, o_ref, kt_ref, gt_ref):
    xb = x_ref[...]
    qk_w = ML_HEADS * ML_DQK
    q_ref[...] = (jnp.dot(xb, w_ref[:, :qk_w], preferred_element_type=F32) * (ML_DQK ** -0.5)).astype(BF16)
    k_ref[...] = jnp.dot(xb, w_ref[:, qk_w:2 * qk_w], preferred_element_type=F32).astype(BF16)
    v_ref[...] = jnp.dot(xb, w_ref[:, 2 * qk_w:2 * qk_w + D_MODEL], preferred_element_type=F32).astype(BF16)
    o_ref[...] = jax.nn.sigmoid(jnp.dot(xb, w_ref[:, 2 * qk_w + D_MODEL:], preferred_element_type=F32))
    kt = lax.dot_general(wkt_ref[...], xb, NT_DIMS, preferred_element_type=F32)
    for j in range(kt_ref.shape[1]):
        kt_ref[0, j] = kt[:, j * ML_CHUNK:(j + 1) * ML_CHUNK]
    gt_ref[0] = lax.dot_general(wgt_ref[...], xb, NT_DIMS, preferred_element_type=F32) + bg_ref[...]


def _ml_proj(x16, w, wkt, wgt, bg, bsz, seq, tm=512):
    t, d = x16.shape
    qk_w = ML_HEADS * ML_DQK
    per_seq = seq // tm
    n_gate = wgt.shape[0]
    row = lambda i: (i, 0)
    return pl.pallas_call(
        _ml_proj_kernel,
        out_shape=(jax.ShapeDtypeStruct((t, qk_w), BF16), jax.ShapeDtypeStruct((t, qk_w), BF16),
                   jax.ShapeDtypeStruct((t, D_MODEL), BF16), jax.ShapeDtypeStruct((t, D_MODEL), F32),
                   jax.ShapeDtypeStruct((bsz, seq // ML_CHUNK, qk_w, ML_CHUNK), F32),
                   jax.ShapeDtypeStruct((bsz, n_gate, seq), F32)),
        grid=(t // tm,),
        in_specs=[pl.BlockSpec((tm, d), row), _const_spec(w.shape), _const_spec(wkt.shape),
                  _const_spec(wgt.shape), _const_spec(bg.shape)],
        out_specs=(pl.BlockSpec((tm, qk_w), row), pl.BlockSpec((tm, qk_w), row),
                   pl.BlockSpec((tm, D_MODEL), row), pl.BlockSpec((tm, D_MODEL), row),
                   pl.BlockSpec((1, tm // ML_CHUNK, qk_w, ML_CHUNK), lambda i: (i // per_seq, i % per_seq, 0, 0)),
                   pl.BlockSpec((1, n_gate, tm), lambda i: (i // per_seq, 0, i % per_seq))),
        compiler_params=pltpu.CompilerParams(
            dimension_semantics=("parallel",),
            vmem_limit_bytes=_vmem_limit(2 * _nbytes((tm, d), BF16), _nbytes(w.shape, BF16), _nbytes(wkt.shape, BF16),
                                         2 * _nbytes((tm, 2 * qk_w + D_MODEL), BF16), 2 * _nbytes((tm, D_MODEL), F32),
                                         4 * _nbytes((qk_w, tm), F32), 2 * _nbytes((tm, D_MODEL), F32))),
        name="mlstm_proj",
    )(x16, w, wkt, wgt, bg)


def _log_sigmoid(x):
    return jnp.minimum(x, 0.0) - jnp.log1p(jnp.exp(-jnp.abs(x)))


def _exact_dot(x, m):
    hi = x.astype(BF16)
    r1 = x - hi.astype(F32)
    mid = r1.astype(BF16)
    lo = (r1 - mid.astype(F32)).astype(BF16)
    return (jnp.dot(hi, m, preferred_element_type=F32) + jnp.dot(mid, m, preferred_element_type=F32)
            + jnp.dot(lo, m, preferred_element_type=F32))


def _mlstm_chunk(qc, kc, ktc, vc, li_row, lf_row, b_row, c_state, n_state, m_state, tri):
    b_col = jnp.sum(jnp.where(tri, lf_row, 0.0), axis=1, keepdims=True)
    a_row = li_row - b_row
    cmax = jnp.max(jnp.where(tri, a_row, -jnp.inf), axis=1, keepdims=True)
    g_col = jnp.maximum(m_state, cmax)
    w = jnp.where(tri, jnp.exp(a_row - g_col), 0.0)
    s_inter = jnp.exp(m_state - g_col)
    qk = lax.dot_general(qc, kc, NT_DIMS, preferred_element_type=F32) * w
    c_ext = jnp.concatenate([c_state, n_state], axis=1).astype(BF16)
    q_c = jnp.dot(qc, c_ext, preferred_element_type=F32)
    num = s_inter * q_c[:, :ML_DV] + jnp.dot(qk.astype(BF16), vc, preferred_element_type=F32)
    den = s_inter * q_c[:, ML_DV:] + jnp.sum(qk, axis=1, keepdims=True)
    h = num / jnp.maximum(jnp.abs(den), jnp.exp(-(b_col + g_col)))
    g_end = jnp.maximum(m_state, jnp.max(a_row, axis=1, keepdims=True))
    m_new = jnp.sum(lf_row, axis=1, keepdims=True) + g_end
    decay = jnp.exp(m_state - g_end)
    wkt = jnp.exp(a_row - g_end) * ktc
    c_new = decay * c_state + jnp.dot(wkt.astype(BF16), vc, preferred_element_type=F32)
    n_new = decay * n_state + jnp.sum(wkt, axis=1, keepdims=True)
    return h, c_new, n_new, m_new


def _mlstm_kernel(q_ref, k_ref, kt_ref, v_ref, o_ref, g_ref, ng_ref, out_ref, hfw_ref, hbw_ref, lf_ref, brow_ref,
                  *, n_chunk, unroll):
    pair = pl.program_id(1)
    t_idx = lax.broadcasted_iota(jnp.int32, (ML_CHUNK, ML_CHUNK), 0)
    s_idx = lax.broadcasted_iota(jnp.int32, (ML_CHUNK, ML_CHUNK), 1)
    masks = (s_idx <= t_idx, s_idx >= t_idx)

    for direction in range(2):
        cum = jnp.where(masks[1 - direction], 1.0, 0.0).astype(BF16)
        for hh in range(2):
            lf = _log_sigmoid(g_ref[0, (direction * 2 + 1) * ML_HEADS + pair * 2 + hh])
            lf_ref[direction * 2 + hh] = lf
            brow_ref[direction * 2 + hh] = _exact_dot(lf, cum)

    def body(c, carry):
        new_carry = []
        for direction in range(2):
            chunk = c if direction == 0 else n_chunk - 1 - c
            rows = pl.ds(pl.multiple_of(chunk * ML_CHUNK, ML_CHUNK), ML_CHUNK)
            for hh in range(2):
                chain = direction * 2 + hh
                c_state, n_state, m_state = carry[chain]
                qc = q_ref[rows, hh * ML_DQK:(hh + 1) * ML_DQK]
                kc = k_ref[rows, hh * ML_DQK:(hh + 1) * ML_DQK]
                vc = v_ref[rows, hh * ML_DV:(hh + 1) * ML_DV]
                ktc = kt_ref[0, chunk, hh * ML_DQK:(hh + 1) * ML_DQK, :]
                li_row = g_ref[0, direction * 2 * ML_HEADS + pair * 2 + hh, pl.ds(chunk, 1), :]
                lf_row = lf_ref[chain, pl.ds(chunk, 1), :]
                b_row = brow_ref[chain, pl.ds(chunk, 1), :]
                h, c_new, n_new, m_new = _mlstm_chunk(qc, kc, ktc, vc, li_row, lf_row, b_row, c_state, n_state, m_state,
                                                       masks[direction])
                dst = hfw_ref if direction == 0 else hbw_ref
                dst[rows, hh * ML_DV:(hh + 1) * ML_DV] = h
                new_carry.append((c_new, n_new, m_new))
        return tuple(new_carry)

    init = tuple((jnp.zeros((ML_DQK, ML_DV), F32), jnp.zeros((ML_DQK, ML_DV), F32), jnp.zeros((1, 1), F32))
                 for _ in range(4))
    lax.fori_loop(0, n_chunk, body, init, unroll=unroll)

    for hh in range(2):
        cols = slice(hh * ML_DV, (hh + 1) * ML_DV)
        h = hfw_ref[:, cols] + hbw_ref[:, cols]
        ms = jnp.mean(h * h, -1, keepdims=True)
        hn = h * lax.rsqrt(ms + EPS) * ng_ref[:, cols]
        out_ref[:, cols] = (o_ref[:, cols] * hn).astype(BF16)


def _mlstm(q, k, kt, v, o, gates, norm_g, bsz, seq, unroll=2):
    n_pair = ML_HEADS // 2
    n_chunk = seq // ML_CHUNK
    pair_w = 2 * ML_DV
    return pl.pallas_call(
        functools.partial(_mlstm_kernel, n_chunk=n_chunk, unroll=unroll),
        out_shape=jax.ShapeDtypeStruct((bsz * seq, D_MODEL), BF16),
        grid=(bsz, n_pair),
        in_specs=[pl.BlockSpec((seq, 2 * ML_DQK), lambda b, p: (b, p)),
                  pl.BlockSpec((seq, 2 * ML_DQK), lambda b, p: (b, p)),
                  pl.BlockSpec((1, n_chunk, 2 * ML_DQK, ML_CHUNK), lambda b, p: (b, 0, p, 0)),
                  pl.BlockSpec((seq, pair_w), lambda b, p: (b, p)),
                  pl.BlockSpec((seq, pair_w), lambda b, p: (b, p)),
                  pl.BlockSpec((1,) + gates.shape[1:], lambda b, p: (b, 0, 0, 0)),
                  pl.BlockSpec((1, pair_w), lambda b, p: (0, p))],
        out_specs=pl.BlockSpec((seq, pair_w), lambda b, p: (b, p)),
        scratch_shapes=[pltpu.VMEM((seq, pair_w), F32), pltpu.VMEM((seq, pair_w), F32),
                        pltpu.VMEM((4, n_chunk, ML_CHUNK), F32), pltpu.VMEM((4, n_chunk, ML_CHUNK), F32)],
        compiler_params=pltpu.CompilerParams(
            dimension_semantics=("parallel", "parallel"),
            vmem_limit_bytes=_vmem_limit(4 * _nbytes((seq, 2 * ML_DQK), BF16), 2 * _nbytes((seq, 2 * ML_DQK), F32),
                                         4 * _nbytes((seq, pair_w), BF16), 2 * _nbytes((seq, pair_w), F32),
                                         2 * _nbytes(gates.shape[1:], F32), 4 * _nbytes((seq, pair_w), F32))),
        name="mlstm_scan",
    )(q, k, kt, v, o, gates, norm_g)


def _block_tail_kernel(mix_ref, x_ref, p_ref, wo_ref, g1_ref, b1_ref, w1_ref, w2_ref, wg_ref, wp_ref, g2_ref, b2_ref,
                       o32_ref, o16_ref, *, ff_chunk):
    y = jnp.dot(mix_ref[...], wo_ref[...], preferred_element_type=F32) + DN_ALPHA * x_ref[...]
    x1 = _layer_norm(y, g1_ref[...], b1_ref[...])
    xb = x1.astype(BF16)
    acc = jnp.zeros(x_ref.shape, F32)
    for c in range(w1_ref.shape[1] // ff_chunk):
        sl = slice(c * ff_chunk, (c + 1) * ff_chunk)
        h = jnp.maximum(jnp.dot(xb, w1_ref[:, sl], preferred_element_type=F32), 0.0)
        acc = acc + jnp.dot((h * h).astype(BF16), w2_ref[sl, :], preferred_element_type=F32)
    gate = jax.nn.sigmoid(jnp.dot(xb, wg_ref[...], preferred_element_type=F32))
    ple = gate * jnp.dot(p_ref[...].astype(BF16), wp_ref[...], preferred_element_type=F32)
    z = _layer_norm(DN_ALPHA * x1 + acc + ple, g2_ref[...], b2_ref[...])
    o32_ref[...] = z
    o16_ref[...] = z.astype(BF16)


def _block_tail(mix, x32, p, wo, g1, b1, w1, w2, wg, wp, g2, b2, tm=512, ff_chunk=1024):
    t, d = x32.shape
    row = lambda i: (i, 0)
    vec = _const_spec((1, d))
    return pl.pallas_call(
        functools.partial(_block_tail_kernel, ff_chunk=ff_chunk),
        out_shape=(jax.ShapeDtypeStruct((t, d), F32), jax.ShapeDtypeStruct((t, d), BF16)),
        grid=(t // tm,),
        in_specs=[pl.BlockSpec((tm, d), row), pl.BlockSpec((tm, d), row), pl.BlockSpec((tm, p.shape[1]), row),
                  _const_spec(wo.shape), vec, vec, _const_spec(w1.shape), _const_spec(w2.shape),
                  _const_spec(wg.shape), _const_spec(wp.shape), vec, vec],
        out_specs=(pl.BlockSpec((tm, d), row), pl.BlockSpec((tm, d), row)),
        compiler_params=pltpu.CompilerParams(
            dimension_semantics=("parallel",),
            vmem_limit_bytes=_vmem_limit(4 * _nbytes((tm, d), BF16), 4 * _nbytes((tm, d), F32),
                                         2 * _nbytes((tm, p.shape[1]), F32), _nbytes(wo.shape, BF16),
                                         _nbytes(w1.shape, BF16), _nbytes(w2.shape, BF16), _nbytes(wg.shape, BF16),
                                         _nbytes(wp.shape, BF16), 2 * _nbytes((tm, ff_chunk), F32),
                                         4 * _nbytes((tm, d), F32))),
        name="block_tail",
    )(mix, x32, p, wo, g1, b1, w1, w2, wg, wp, g2, b2)


def kernel(x, p, na_w_qkv, na_rpb, na_w_o, gq_w_qkv, gq_q_norm, gq_k_norm, gq_w_o, ml_w_in, ml_b_gates, ml_norm_g,
           ml_w_o, ln1_g, ln1_b, w_ff1, w_ff2, ln2_g, ln2_b, w_ple_gate, w_ple_proj):
    bsz, seq, d = x.shape
    assert d == D_MODEL and seq % (NA_KH * GRID_W) == 0 and p.shape == (DEPTH, bsz, seq, D_PLE)
    t = bsz * seq
    x32 = x.reshape(t, d)
    x16 = None
    qk_w = ML_HEADS * ML_DQK
    for i in range(DEPTH):
        kind, j = i % 3, i // 3
        if kind == 0:
            qkv = _proj(x32 if x16 is None else x16, na_w_qkv[j].astype(BF16))
            mix = _na_attention(qkv, _na_bias_table(na_rpb[j]), bsz, seq)
            w_o = na_w_o[j]
        elif kind == 1:
            cos4, sin4 = _rope_tables(seq)
            qkv = _gqa_proj(x16, _gqa_weight_layout(gq_w_qkv[j]).astype(BF16), cos4, sin4,
                            _gqa_gain_layout(gq_q_norm[j]), _gqa_gain_layout(gq_k_norm[j]), _group_sum_matrix(), seq)
            mix = _gqa_attention(qkv, bsz, seq)
            w_o = gq_w_o[j]
        else:
            w_in = ml_w_in[j]
            n_main = 2 * qk_w + 2 * D_MODEL
            wkt = w_in[:, qk_w:2 * qk_w].T.astype(BF16)
            wgt = w_in[:, n_main:].T.astype(BF16)
            q, k, v, o, kt, gt = _ml_proj(x16, w_in[:, :n_main].astype(BF16), wkt, wgt, ml_b_gates[j][:, None], bsz, seq)
            gates = gt.reshape(bsz, gt.shape[1], seq // ML_CHUNK, ML_CHUNK)
            mix = _mlstm(q, k, kt, v, o, gates, ml_norm_g[j][None, :], bsz, seq)
            w_o = ml_w_o[j]
        x32, x16 = _block_tail(mix, x32, p[i].reshape(t, D_PLE), w_o.astype(BF16), ln1_g[i][None, :], ln1_b[i][None, :],
                               w_ff1[i].astype(BF16), w_ff2[i].astype(BF16), w_ple_gate[i].astype(BF16),
                               w_ple_proj[i].astype(BF16), ln2_g[i][None, :], ln2_b[i][None, :])
    return x32.reshape(bsz, seq, d)
```

```python
import functools

import jax
import jax.numpy as jnp
import numpy as np
from jax import lax
from jax.experimental import pallas as pl
from jax.experimental.pallas import tpu as pltpu

F32 = jnp.float32
BF16 = jnp.bfloat16

D_MODEL = 1024
DEPTH = 4
GRID_W = 64
HEAD_DIM = 64
D_FF = 4 * D_MODEL
D_PLE = 256
NA_HEADS = 16
NA_KH = 8
NA_KW = 16
GQA_KV_HEADS = 4
GQA_GROUP = 4
ROPE_THETA = 10000.0
ML_HEADS = 8
ML_DV = 128
ML_DQK = 64
ML_CHUNK = 64
ML_BLOCK = 256
DN_ALPHA = (2 * DEPTH) ** 0.25
EPS = 1e-6
LOG2E = 1.4426950408889634

V7X_VMEM_BYTES = 64 * 1024 * 1024
V7X_LANES = 128
SLAB = 2 * V7X_LANES

NT_DIMS = (((1,), (1,)), ((), ()))


def _vmem_limit(*byte_counts):
    est = int(sum(byte_counts) * 1.5) + (4 << 20)
    return min(est, V7X_VMEM_BYTES - (6 << 20))


def _nbytes(shape, dtype):
    return int(np.prod(shape)) * jnp.dtype(dtype).itemsize


def _const_spec(shape):
    nd = len(shape)
    return pl.BlockSpec(shape, lambda *_: (0,) * nd, pipeline_mode=pl.Buffered(1))


def _layer_norm(y, g, b):
    mu = jnp.mean(y, -1, keepdims=True)
    yc = y - mu
    var = jnp.mean(yc * yc, -1, keepdims=True)
    return yc * lax.rsqrt(var + EPS) * g + b


def _proj_kernel(x_ref, w_ref, o_ref, *, n_chunk):
    xb = x_ref[...].astype(BF16)
    for j in range(o_ref.shape[1] // n_chunk):
        sl = slice(j * n_chunk, (j + 1) * n_chunk)
        o_ref[:, sl] = jnp.dot(xb, w_ref[:, sl], preferred_element_type=F32).astype(BF16)


def _proj(x, w, tm=512, n_chunk=1024):
    t, d = x.shape
    n = w.shape[1]
    return pl.pallas_call(
        functools.partial(_proj_kernel, n_chunk=n_chunk),
        out_shape=jax.ShapeDtypeStruct((t, n), BF16),
        grid=(t // tm,),
        in_specs=[pl.BlockSpec((tm, d), lambda i: (i, 0)), _const_spec((d, n))],
        out_specs=pl.BlockSpec((tm, n), lambda i: (i, 0)),
        compiler_params=pltpu.CompilerParams(
            dimension_semantics=("parallel",),
            vmem_limit_bytes=_vmem_limit(2 * _nbytes((tm, d), x.dtype), _nbytes((d, n), BF16),
                                         2 * _nbytes((tm, n), BF16), _nbytes((tm, n_chunk), F32))),
        name="proj_plain",
    )(x, w)


def _na_kernel(q_ref, k_ref, v_ref, bias_ref, o_ref, s_ref, p_ref, *, rows, group):
    lane_head = lax.broadcasted_iota(jnp.int32, (1, SLAB), 1) // HEAD_DIM
    win = NA_KH * GRID_W

    def window(r):
        r = jnp.clip(r, 0, rows - 1)
        r0 = jnp.clip(r - NA_KH // 2, 0, rows - NA_KH)
        return pl.multiple_of(r * GRID_W, GRID_W), pl.multiple_of(r0 * GRID_W, GRID_W), r - r0

    def scores(r, slot):
        q0, k0, delta = window(r)
        q = q_ref[pl.ds(q0, GRID_W), :]
        qs = jnp.concatenate([jnp.where(lane_head == h, q, jnp.zeros_like(q)) for h in range(4)], axis=0)
        s = lax.dot_general(qs, k_ref[pl.ds(k0, win), :], NT_DIMS, preferred_element_type=F32)
        s_ref[slot] = s * (HEAD_DIM ** -0.5 * LOG2E) + bias_ref[0, delta]

    def softmax(slot):
        s = s_ref[slot]
        e = jnp.exp2(s - jnp.max(s, -1, keepdims=True))
        p_ref[slot] = (e * (1.0 / jnp.sum(e, -1, keepdims=True))).astype(BF16)

    def weighted_values(r, slot):
        q0, k0, _ = window(r)
        pv = jnp.dot(p_ref[slot], v_ref[pl.ds(k0, win), :], preferred_element_type=F32)
        acc = jnp.zeros((GRID_W, SLAB), F32)
        for h in range(4):
            acc = jnp.where(lane_head == h, pv[h * GRID_W:(h + 1) * GRID_W], acc)
        o_ref[pl.ds(q0, GRID_W), :] = acc.astype(BF16)

    def step(g, bank):
        for t in range(group):
            scores((g + 1) * group + t, (1 - bank) * group + t)
            softmax(bank * group + t)
            weighted_values((g - 1) * group + t, (1 - bank) * group + t)

    for t in range(group):
        scores(t, t)
        p_ref[group + t] = jnp.zeros(p_ref.shape[1:], BF16)

    def body(j, carry):
        step(2 * j, 0)
        step(2 * j + 1, 1)
        return carry

    n_step = rows // group
    lax.fori_loop(0, n_step // 2, body, 0)
    for t in range(group):
        weighted_values(rows - group + t, ((n_step - 1) % 2) * group + t)


def _na_bias_table(rpb):
    col = np.arange(GRID_W)
    c0 = np.clip(col - NA_KW // 2, 0, GRID_W - NA_KW)
    col_in = (col[None, :] >= c0[:, None]) & (col[None, :] < c0[:, None] + NA_KW)
    dc = np.clip(col[None, :] - col[:, None], 1 - NA_KW, NA_KW - 1) + NA_KW - 1
    rpb = rpb.astype(F32)
    by_col = jnp.zeros(rpb.shape[:2] + dc.shape, F32)
    for c in range(2 * NA_KW - 1):
        by_col = jnp.where(dc[None, None] == c, rpb[:, :, c][:, :, None, None], by_col)
    by_col = jnp.where(col_in[None, None], by_col * LOG2E, -jnp.inf)
    per_delta = [by_col[:, NA_KH - 1 - dl:2 * NA_KH - 1 - dl].transpose(0, 2, 1, 3)
                 .reshape(NA_HEADS, GRID_W, NA_KH * GRID_W) for dl in range(NA_KH)]
    b = jnp.stack(per_delta, axis=1).reshape(NA_HEADS // 4, 4, NA_KH, GRID_W, NA_KH * GRID_W)
    return b.transpose(0, 2, 1, 3, 4).reshape(NA_HEADS // 4, NA_KH, 4 * GRID_W, NA_KH * GRID_W)


def _na_attention(qkv, bias, bsz, seq, group=2):
    n_slab = D_MODEL // SLAB
    rows = seq // GRID_W
    assert rows % (2 * group) == 0
    blk = (seq, SLAB)
    tile = (4 * GRID_W, NA_KH * GRID_W)
    return pl.pallas_call(
        functools.partial(_na_kernel, rows=rows, group=group),
        out_shape=jax.ShapeDtypeStruct((bsz * seq, D_MODEL), BF16),
        grid=(n_slab, bsz),
        in_specs=[pl.BlockSpec(blk, lambda s, b: (b, s)),
                  pl.BlockSpec(blk, lambda s, b: (b, n_slab + s)),
                  pl.BlockSpec(blk, lambda s, b: (b, 2 * n_slab + s)),
                  pl.BlockSpec((1,) + bias.shape[1:], lambda s, b: (s, 0, 0, 0))],
        out_specs=pl.BlockSpec(blk, lambda s, b: (b, s)),
        scratch_shapes=[pltpu.VMEM((2 * group,) + tile, F32), pltpu.VMEM((2 * group,) + tile, BF16)],
        compiler_params=pltpu.CompilerParams(
            dimension_semantics=("parallel", "parallel"),
            vmem_limit_bytes=_vmem_limit(8 * _nbytes(blk, BF16), 2 * _nbytes(bias.shape[1:], F32),
                                         (3 * group + 4) * _nbytes(tile, F32))),
        name="na_attention",
    )(qkv, qkv, qkv, bias)


def _gqa_proj_kernel(x_ref, w_ref, cos_ref, sin_ref, gq_ref, gk_ref, ones_ref, o_ref):
    xb = x_ref[...]
    cos = cos_ref[...]
    sin = sin_ref[...]
    ones = ones_ref[...]
    n_norm = 2 * D_MODEL // (2 * SLAB)
    for pair in range(n_norm):
        z = jnp.dot(xb, w_ref[:, pair * 2 * SLAB:(pair + 1) * 2 * SLAB], preferred_element_type=F32)
        halves = [(z[:, j * SLAB:j * SLAB + V7X_LANES], z[:, j * SLAB + V7X_LANES:(j + 1) * SLAB]) for j in range(2)]
        ss = jnp.concatenate([a * a + b * b for a, b in halves], axis=1)
        hi = ss.astype(BF16)
        lo = (ss - hi.astype(F32)).astype(BF16)
        ms = (jnp.dot(hi, ones, preferred_element_type=F32)
              + jnp.dot(lo, ones, preferred_element_type=F32)) * (1.0 / HEAD_DIM)
        rs = lax.rsqrt(ms + EPS)
        is_q = pair < n_norm // 2
        g_ref = gq_ref if is_q else gk_ref
        for j, (a, b) in enumerate(halves):
            r = rs[:, j * V7X_LANES:(j + 1) * V7X_LANES]
            an = a * r * g_ref[:, :V7X_LANES]
            bn = b * r * g_ref[:, V7X_LANES:]
            oa = an * cos - bn * sin
            ob = an * sin + bn * cos
            if is_q:
                oa = oa * (HEAD_DIM ** -0.5 * LOG2E)
                ob = ob * (HEAD_DIM ** -0.5 * LOG2E)
            c0 = (pair * 2 + j) * SLAB
            o_ref[:, c0:c0 + V7X_LANES] = oa.astype(BF16)
            o_ref[:, c0 + V7X_LANES:c0 + SLAB] = ob.astype(BF16)
    c0 = 2 * D_MODEL
    o_ref[:, c0:] = jnp.dot(xb, w_ref[:, c0:], preferred_element_type=F32).astype(BF16)


def _gqa_weight_layout(w):
    d = w.shape[0]
    kvd = GQA_KV_HEADS * HEAD_DIM
    half = HEAD_DIM // 2
    wq = w[:, :D_MODEL].reshape(d, GQA_KV_HEADS, GQA_GROUP, half, 2).transpose(0, 1, 4, 2, 3)
    wk = w[:, D_MODEL:D_MODEL + kvd].reshape(d, GQA_KV_HEADS, half, 2).transpose(0, 1, 3, 2)
    wk = jnp.broadcast_to(wk[:, :, :, None, :], (d, GQA_KV_HEADS, 2, GQA_GROUP, half))
    wv = w[:, D_MODEL + kvd:].reshape(d, GQA_KV_HEADS, 1, HEAD_DIM)
    wv = jnp.broadcast_to(wv, (d, GQA_KV_HEADS, GQA_GROUP, HEAD_DIM))
    return jnp.concatenate([wq.reshape(d, D_MODEL), wk.reshape(d, D_MODEL), wv.reshape(d, D_MODEL)], axis=1)


def _gqa_gain_layout(g):
    half = HEAD_DIM // 2
    return jnp.broadcast_to(g.reshape(half, 2).T[:, None, :], (2, GQA_GROUP, half)).reshape(1, SLAB)


def _group_sum_matrix():
    blk = np.arange(SLAB) // (HEAD_DIM // 2)
    return jnp.asarray(blk[:, None] == blk[None, :], BF16)


def _rope_tables(seq):
    t = jnp.arange(seq)
    row = (t // GRID_W).astype(F32)
    col = (t % GRID_W).astype(F32)
    n_pairs = HEAD_DIM // 4
    inv = ROPE_THETA ** (-jnp.arange(n_pairs, dtype=F32) / n_pairs)
    ang = jnp.concatenate([row[:, None] * inv, col[:, None] * inv], -1)
    return jnp.tile(jnp.cos(ang), (1, GQA_GROUP)), jnp.tile(jnp.sin(ang), (1, GQA_GROUP))


def _gqa_proj(x16, w, cos4, sin4, gq, gk, ones, seq, tm=256):
    t, d = x16.shape
    n = w.shape[1]
    per_seq = seq // tm
    return pl.pallas_call(
        _gqa_proj_kernel,
        out_shape=jax.ShapeDtypeStruct((t, n), BF16),
        grid=(t // tm,),
        in_specs=[pl.BlockSpec((tm, d), lambda i: (i, 0)), _const_spec((d, n)),
                  pl.BlockSpec((tm, V7X_LANES), lambda i: (i % per_seq, 0)),
                  pl.BlockSpec((tm, V7X_LANES), lambda i: (i % per_seq, 0)),
                  _const_spec((1, SLAB)), _const_spec((1, SLAB)), _const_spec((SLAB, SLAB))],
        out_specs=pl.BlockSpec((tm, n), lambda i: (i, 0)),
        compiler_params=pltpu.CompilerParams(
            dimension_semantics=("parallel",),
            vmem_limit_bytes=_vmem_limit(2 * _nbytes((tm, d), BF16), _nbytes((d, n), BF16),
                                         2 * _nbytes((tm, n), BF16), 6 * _nbytes((tm, 2 * SLAB), F32))),
        name="gqa_proj",
    )(x16, w, cos4, sin4, gq, gk, ones)


def _gqa_attn_kernel(q_ref, k_ref, v_ref, o_ref):
    lane = lax.broadcasted_iota(jnp.int32, (1, SLAB), 1)
    q = q_ref[...]
    k = k_ref[...]
    v = v_ref[...]
    q_head = (lane % V7X_LANES) // (HEAD_DIM // 2)
    out = jnp.zeros(o_ref.shape, F32)
    for h in range(GQA_GROUP):
        qm = jnp.where(q_head == h, q, jnp.zeros_like(q))
        s = lax.dot_general(qm, k, NT_DIMS, preferred_element_type=F32)
        e = jnp.exp2(s - jnp.max(s, -1, keepdims=True))
        inv_l = 1.0 / jnp.sum(e, -1, keepdims=True)
        pv = jnp.dot(e.astype(BF16), v, preferred_element_type=F32) * inv_l
        out = jnp.where(lane // HEAD_DIM == h, pv, out)
    o_ref[...] = out.astype(BF16)


def _gqa_attention(qkv, bsz, seq, tq=512):
    n_slab = D_MODEL // SLAB
    nq = seq // tq
    return pl.pallas_call(
        _gqa_attn_kernel,
        out_shape=jax.ShapeDtypeStruct((bsz * seq, D_MODEL), BF16),
        grid=(bsz, n_slab, nq),
        in_specs=[pl.BlockSpec((tq, SLAB), lambda b, g, i: (b * nq + i, g)),
                  pl.BlockSpec((seq, SLAB), lambda b, g, i: (b, n_slab + g)),
                  pl.BlockSpec((seq, SLAB), lambda b, g, i: (b, 2 * n_slab + g))],
        out_specs=pl.BlockSpec((tq, SLAB), lambda b, g, i: (b * nq + i, g)),
        compiler_params=pltpu.CompilerParams(
            dimension_semantics=("parallel", "parallel", "parallel"),
            vmem_limit_bytes=_vmem_limit(4 * _nbytes((tq, SLAB), BF16), 8 * _nbytes((seq, SLAB), BF16),
                                         3 * _nbytes((tq, seq), F32))),
        name="gqa_attention",
    )(qkv, qkv, qkv)


def _ml_proj_kernel(x_ref, w_ref, wkt_ref, wgt_ref, bg_ref, q_ref, k_ref, v_ref, o_ref, kt_ref, gt_ref):
    xb = x_ref[...]
    qk_w = ML_HEADS * ML_DQK
    q_ref[...] = (jnp.dot(xb, w_ref[:, :qk_w], preferred_element_type=F32) * (ML_DQK ** -0.5)).astype(BF16)
    k_ref[...] = jnp.dot(xb, w_ref[:, qk_w:2 * qk_w], preferred_element_type=F32).astype(BF16)
    v_ref[...] = jnp.dot(xb, w_ref[:, 2 * qk_w:2 * qk_w + D_MODEL], preferred_element_type=F32).astype(BF16)
    o_ref[...] = jax.nn.sigmoid(jnp.dot(xb, w_ref[:, 2 * qk_w + D_MODEL:], preferred_element_type=F32))
    kt = lax.dot_general(wkt_ref[...], xb, NT_DIMS, preferred_element_type=F32)
    for j in range(kt_ref.shape[1]):
        kt_ref[0, j] = kt[:, j * ML_BLOCK:(j + 1) * ML_BLOCK]
    gt_ref[0] = lax.dot_general(wgt_ref[...], xb, NT_DIMS, preferred_element_type=F32) + bg_ref[...]


def _ml_proj(x16, w, wkt, wgt, bg, bsz, seq, tm=512):
    t, d = x16.shape
    qk_w = ML_HEADS * ML_DQK
    per_seq = seq // tm
    n_gate = wgt.shape[0]
    row = lambda i: (i, 0)
    return pl.pallas_call(
        _ml_proj_kernel,
        out_shape=(jax.ShapeDtypeStruct((t, qk_w), BF16), jax.ShapeDtypeStruct((t, qk_w), BF16),
                   jax.ShapeDtypeStruct((t, D_MODEL), BF16), jax.ShapeDtypeStruct((t, D_MODEL), F32),
                   jax.ShapeDtypeStruct((bsz, seq // ML_BLOCK, qk_w, ML_BLOCK), F32),
                   jax.ShapeDtypeStruct((bsz, n_gate, seq), F32)),
        grid=(t // tm,),
        in_specs=[pl.BlockSpec((tm, d), row), _const_spec(w.shape), _const_spec(wkt.shape),
                  _const_spec(wgt.shape), _const_spec(bg.shape)],
        out_specs=(pl.BlockSpec((tm, qk_w), row), pl.BlockSpec((tm, qk_w), row),
                   pl.BlockSpec((tm, D_MODEL), row), pl.BlockSpec((tm, D_MODEL), row),
                   pl.BlockSpec((1, tm // ML_BLOCK, qk_w, ML_BLOCK), lambda i: (i // per_seq, i % per_seq, 0, 0)),
                   pl.BlockSpec((1, n_gate, tm), lambda i: (i // per_seq, 0, i % per_seq))),
        compiler_params=pltpu.CompilerParams(
            dimension_semantics=("parallel",),
            vmem_limit_bytes=_vmem_limit(2 * _nbytes((tm, d), BF16), _nbytes(w.shape, BF16), _nbytes(wkt.shape, BF16),
                                         2 * _nbytes((tm, 2 * qk_w + D_MODEL), BF16), 2 * _nbytes((tm, D_MODEL), F32),
                                         4 * _nbytes((qk_w, tm), F32), 2 * _nbytes((tm, D_MODEL), F32))),
        name="mlstm_proj",
    )(x16, w, wkt, wgt, bg)


def _log_sigmoid(x):
    return jnp.minimum(x, 0.0) - jnp.log1p(jnp.exp(-jnp.abs(x)))


def _exact_dot(x, m):
    hi = x.astype(BF16)
    r1 = x - hi.astype(F32)
    mid = r1.astype(BF16)
    lo = (r1 - mid.astype(F32)).astype(BF16)
    return (jnp.dot(hi, m, preferred_element_type=F32) + jnp.dot(mid, m, preferred_element_type=F32)
            + jnp.dot(lo, m, preferred_element_type=F32))


def _lane_cummax(x, lane_pos, reverse):
    shift = 1
    while shift < ML_CHUNK:
        if reverse:
            moved, ok = pltpu.roll(x, x.shape[1] - shift, axis=1), lane_pos < ML_CHUNK - shift
        else:
            moved, ok = pltpu.roll(x, shift, axis=1), lane_pos >= shift
        x = jnp.maximum(x, jnp.where(ok, moved, -jnp.inf))
        shift *= 2
    return x


def _mlstm_block(q, k, kt, v, a_row, ge_row, decay_row, m_row, g_col, b_col, c_state, n_state, vis, same, row_chunk,
                 reverse):
    per = ML_BLOCK // ML_CHUNK
    g_rep = jnp.broadcast_to(g_col, (ML_BLOCK, ML_DV))
    b_rep = jnp.broadcast_to(b_col, (ML_BLOCK, ML_DV))
    m_rep = jnp.broadcast_to(m_row[:, (per - 1) * ML_CHUNK:(per - 1) * ML_CHUNK + 1], g_rep.shape)
    for i in range(per - 2, -1, -1):
        m_rep = jnp.where(row_chunk == i, m_row[:, i * ML_CHUNK:i * ML_CHUNK + 1], m_rep)
    s_inter = jnp.exp(m_rep - g_rep)
    floor_rep = jnp.exp(-(b_rep + g_rep))
    v_ext = jnp.concatenate([v, jnp.ones_like(v)], axis=1)
    w = jnp.where(vis, jnp.exp(a_row - jnp.concatenate([g_rep, g_rep], axis=1)), 0.0)
    qk = lax.dot_general(q, k, NT_DIMS, preferred_element_type=F32) * w
    intra = jnp.dot(qk.astype(BF16), v_ext, preferred_element_type=F32)
    wkt = jnp.exp(a_row - ge_row) * kt
    wkt4 = jnp.where(same, jnp.concatenate([wkt] * per, axis=0), 0.0)
    delta = jnp.dot(wkt4.astype(BF16), v_ext, preferred_element_type=F32)
    starts = [None] * per
    state = jnp.concatenate([c_state, n_state], axis=1)
    for i in (range(per - 1, -1, -1) if reverse else range(per)):
        starts[i] = state
        state = decay_row[:, i * ML_CHUNK:i * ML_CHUNK + 1] * state + delta[i * ML_DQK:(i + 1) * ML_DQK]
    q4 = jnp.where(same, jnp.concatenate([q] * per, axis=1), jnp.zeros((), q.dtype))
    inter = jnp.dot(q4, jnp.concatenate(starts, axis=0).astype(BF16), preferred_element_type=F32)
    num = s_inter * inter[:, :ML_DV] + intra[:, :ML_DV]
    den = s_inter * inter[:, ML_DV:] + intra[:, ML_DV:]
    h = num / jnp.maximum(jnp.abs(den), floor_rep)
    return h, state[:, :ML_DV], state[:, ML_DV:]


def _mlstm_kernel(q_ref, k_ref, kt_ref, v_ref, o_ref, g_ref, ng_ref, out_ref, hfw_ref, hbw_ref, row_ref, col_ref,
                  *, n_block):
    pair = pl.program_id(1)
    per = ML_BLOCK // ML_CHUNK
    t_idx = lax.broadcasted_iota(jnp.int32, (ML_BLOCK, ML_BLOCK), 0)
    s_idx = lax.broadcasted_iota(jnp.int32, (ML_BLOCK, ML_BLOCK), 1)
    same = (t_idx // ML_CHUNK) == (s_idx // ML_CHUNK)
    masks = (same & (s_idx <= t_idx), same & (s_idx >= t_idx))
    lane = lax.broadcasted_iota(jnp.int32, (1, ML_BLOCK), 1)
    lane_chunk, lane_pos = lane // ML_CHUNK, lane % ML_CHUNK
    blk_row = lax.broadcasted_iota(jnp.int32, (n_block, 1), 0)
    row_chunk = lax.broadcasted_iota(jnp.int32, (ML_BLOCK, ML_DV), 0) // ML_CHUNK

    per_query = []
    for direction in range(2):
        cum = jnp.where(masks[1 - direction], 1.0, 0.0).astype(BF16)
        for hh in range(2):
            chain = direction * 2 + hh
            head = pair * 2 + hh
            lf = _log_sigmoid(g_ref[0, (direction * 2 + 1) * ML_HEADS + head])
            b = _exact_dot(lf, cum)
            a = g_ref[0, direction * 2 * ML_HEADS + head] - b
            a_max = [jnp.max(jnp.where(lane_chunk == i, a, -jnp.inf), axis=1, keepdims=True) for i in range(per)]
            f_sum = [jnp.sum(jnp.where(lane_chunk == i, lf, 0.0), axis=1, keepdims=True) for i in range(per)]
            m = jnp.zeros((1, 1), F32)
            m_row = jnp.zeros(a.shape, F32)
            ge_row = jnp.zeros(a.shape, F32)
            n_chunk = n_block * per
            for c in (range(n_chunk - 1, -1, -1) if direction else range(n_chunk)):
                blk, i = divmod(c, per)
                g_end = jnp.maximum(m, a_max[i][blk:blk + 1])
                here = (blk_row == blk) & (lane_chunk == i)
                m_row = jnp.where(here, m, m_row)
                ge_row = jnp.where(here, g_end, ge_row)
                m = f_sum[i][blk:blk + 1] + g_end
            g_row = jnp.maximum(m_row, _lane_cummax(a, lane_pos, reverse=bool(direction)))
            row_ref[chain, 0] = a
            row_ref[chain, 1] = ge_row
            row_ref[chain, 2] = jnp.exp(m_row - ge_row)
            row_ref[chain, 3] = m_row
            per_query += [g_row, b]
    flat = [jnp.concatenate([x[blk:blk + 1] for blk in range(n_block)], axis=1) for x in per_query]
    col_ref[...] = jnp.concatenate(flat, axis=0).T

    def body(step, carry):
        new_carry = []
        for direction in range(2):
            blk = step if direction == 0 else n_block - 1 - step
            rows = pl.ds(pl.multiple_of(blk * ML_BLOCK, ML_BLOCK), ML_BLOCK)
            for hh in range(2):
                chain = direction * 2 + hh
                c_state, n_state = carry[chain]
                h, c_new, n_new = _mlstm_block(
                    q_ref[rows, hh * ML_DQK:(hh + 1) * ML_DQK], k_ref[rows, hh * ML_DQK:(hh + 1) * ML_DQK],
                    kt_ref[0, blk, hh * ML_DQK:(hh + 1) * ML_DQK, :], v_ref[rows, hh * ML_DV:(hh + 1) * ML_DV],
                    *[row_ref[chain, j, pl.ds(blk, 1), :] for j in range(4)],
                    col_ref[rows, 2 * chain:2 * chain + 1], col_ref[rows, 2 * chain + 1:2 * chain + 2],
                    c_state, n_state, masks[direction], same, row_chunk, reverse=bool(direction))
                dst = hfw_ref if direction == 0 else hbw_ref
                dst[rows, hh * ML_DV:(hh + 1) * ML_DV] = h
                new_carry.append((c_new, n_new))
        return tuple(new_carry)

    init = tuple((jnp.zeros((ML_DQK, ML_DV), F32), jnp.zeros((ML_DQK, ML_DV), F32)) for _ in range(4))
    lax.fori_loop(0, n_block, body, init)

    for hh in range(2):
        cols = slice(hh * ML_DV, (hh + 1) * ML_DV)
        h = hfw_ref[:, cols] + hbw_ref[:, cols]
        ms = jnp.mean(h * h, -1, keepdims=True)
        hn = h * lax.rsqrt(ms + EPS) * ng_ref[:, cols]
        out_ref[:, cols] = (o_ref[:, cols] * hn).astype(BF16)


def _mlstm(q, k, kt, v, o, gates, norm_g, bsz, seq):
    n_pair = ML_HEADS // 2
    n_block = seq // ML_BLOCK
    pair_w = 2 * ML_DV
    return pl.pallas_call(
        functools.partial(_mlstm_kernel, n_block=n_block),
        out_shape=jax.ShapeDtypeStruct((bsz * seq, D_MODEL), BF16),
        grid=(bsz, n_pair),
        in_specs=[pl.BlockSpec((seq, 2 * ML_DQK), lambda b, p: (b, p)),
                  pl.BlockSpec((seq, 2 * ML_DQK), lambda b, p: (b, p)),
                  pl.BlockSpec((1, n_block, 2 * ML_DQK, ML_BLOCK), lambda b, p: (b, 0, p, 0)),
                  pl.BlockSpec((seq, pair_w), lambda b, p: (b, p)),
                  pl.BlockSpec((seq, pair_w), lambda b, p: (b, p)),
                  pl.BlockSpec((1,) + gates.shape[1:], lambda b, p: (b, 0, 0, 0)),
                  pl.BlockSpec((1, pair_w), lambda b, p: (0, p))],
        out_specs=pl.BlockSpec((seq, pair_w), lambda b, p: (b, p)),
        scratch_shapes=[pltpu.VMEM((seq, pair_w), F32), pltpu.VMEM((seq, pair_w), F32),
                        pltpu.VMEM((4, 4, n_block, ML_BLOCK), F32), pltpu.VMEM((seq, 8), F32)],
        compiler_params=pltpu.CompilerParams(
            dimension_semantics=("parallel", "parallel"),
            vmem_limit_bytes=_vmem_limit(4 * _nbytes((seq, 2 * ML_DQK), BF16), 2 * _nbytes((seq, 2 * ML_DQK), F32),
                                         4 * _nbytes((seq, pair_w), BF16), 2 * _nbytes((seq, pair_w), F32),
                                         2 * _nbytes(gates.shape[1:], F32), 2 * _nbytes((seq, pair_w), F32))),
        name="mlstm_scan",
    )(q, k, kt, v, o, gates, norm_g)


def _block_tail_kernel(mix_ref, x_ref, p_ref, wo_ref, g1_ref, b1_ref, w1_ref, w2_ref, wg_ref, wp_ref, g2_ref, b2_ref,
                       o32_ref, o16_ref, *, ff_chunk):
    y = jnp.dot(mix_ref[...], wo_ref[...], preferred_element_type=F32) + DN_ALPHA * x_ref[...]
    x1 = _layer_norm(y, g1_ref[...], b1_ref[...])
    xb = x1.astype(BF16)
    acc = jnp.zeros(x_ref.shape, F32)
    for c in range(w1_ref.shape[1] // ff_chunk):
        sl = slice(c * ff_chunk, (c + 1) * ff_chunk)
        h = jnp.maximum(jnp.dot(xb, w1_ref[:, sl], preferred_element_type=F32), 0.0)
        acc = acc + jnp.dot((h * h).astype(BF16), w2_ref[sl, :], preferred_element_type=F32)
    gate = jax.nn.sigmoid(jnp.dot(xb, wg_ref[...], preferred_element_type=F32))
    ple = gate * jnp.dot(p_ref[...].astype(BF16), wp_ref[...], preferred_element_type=F32)
    z = _layer_norm(DN_ALPHA * x1 + acc + ple, g2_ref[...], b2_ref[...])
    o32_ref[...] = z
    o16_ref[...] = z.astype(BF16)


def _block_tail(mix, x32, p, wo, g1, b1, w1, w2, wg, wp, g2, b2, tm=512, ff_chunk=1024):
    t, d = x32.shape
    row = lambda i: (i, 0)
    vec = _const_spec((1, d))
    return pl.pallas_call(
        functools.partial(_block_tail_kernel, ff_chunk=ff_chunk),
        out_shape=(jax.ShapeDtypeStruct((t, d), F32), jax.ShapeDtypeStruct((t, d), BF16)),
        grid=(t // tm,),
        in_specs=[pl.BlockSpec((tm, d), row), pl.BlockSpec((tm, d), row), pl.BlockSpec((tm, p.shape[1]), row),
                  _const_spec(wo.shape), vec, vec, _const_spec(w1.shape), _const_spec(w2.shape),
                  _const_spec(wg.shape), _const_spec(wp.shape), vec, vec],
        out_specs=(pl.BlockSpec((tm, d), row), pl.BlockSpec((tm, d), row)),
        compiler_params=pltpu.CompilerParams(
            dimension_semantics=("parallel",),
            vmem_limit_bytes=_vmem_limit(4 * _nbytes((tm, d), BF16), 4 * _nbytes((tm, d), F32),
                                         2 * _nbytes((tm, p.shape[1]), F32), _nbytes(wo.shape, BF16),
                                         _nbytes(w1.shape, BF16), _nbytes(w2.shape, BF16), _nbytes(wg.shape, BF16),
                                         _nbytes(wp.shape, BF16), 2 * _nbytes((tm, ff_chunk), F32),
                                         4 * _nbytes((tm, d), F32))),
        name="block_tail",
    )(mix, x32, p, wo, g1, b1, w1, w2, wg, wp, g2, b2)


def kernel(x, p, na_w_qkv, na_rpb, na_w_o, gq_w_qkv, gq_q_norm, gq_k_norm, gq_w_o, ml_w_in, ml_b_gates, ml_norm_g,
           ml_w_o, ln1_g, ln1_b, w_ff1, w_ff2, ln2_g, ln2_b, w_ple_gate, w_ple_proj):
    bsz, seq, d = x.shape
    assert d == D_MODEL and seq % (NA_KH * GRID_W) == 0 and p.shape == (DEPTH, bsz, seq, D_PLE)
    t = bsz * seq
    x32 = x.reshape(t, d)
    x16 = None
    qk_w = ML_HEADS * ML_DQK
    for i in range(DEPTH):
        kind, j = i % 3, i // 3
        if kind == 0:
            qkv = _proj(x32 if x16 is None else x16, na_w_qkv[j].astype(BF16))
            mix = _na_attention(qkv, _na_bias_table(na_rpb[j]), bsz, seq)
            w_o = na_w_o[j]
        elif kind == 1:
            cos4, sin4 = _rope_tables(seq)
            qkv = _gqa_proj(x16, _gqa_weight_layout(gq_w_qkv[j]).astype(BF16), cos4, sin4,
                            _gqa_gain_layout(gq_q_norm[j]), _gqa_gain_layout(gq_k_norm[j]), _group_sum_matrix(), seq)
            mix = _gqa_attention(qkv, bsz, seq)
            w_o = gq_w_o[j]
        else:
            w_in = ml_w_in[j]
            n_main = 2 * qk_w + 2 * D_MODEL
            wkt = w_in[:, qk_w:2 * qk_w].T.astype(BF16)
            wgt = w_in[:, n_main:].T.astype(BF16)
            q, k, v, o, kt, gt = _ml_proj(x16, w_in[:, :n_main].astype(BF16), wkt, wgt, ml_b_gates[j][:, None], bsz, seq)
            gates = gt.reshape(bsz, gt.shape[1], seq // ML_BLOCK, ML_BLOCK)
            mix = _mlstm(q, k, kt, v, o, gates, ml_norm_g[j][None, :], bsz, seq)
            w_o = ml_w_o[j]
        x32, x16 = _block_tail(mix, x32, p[i].reshape(t, D_PLE), w_o.astype(BF16), ln1_g[i][None, :], ln1_b[i][None, :],
                               w_ff1[i].astype(BF16), w_ff2[i].astype(BF16), w_ple_gate[i].astype(BF16),
                               w_ple_proj[i].astype(BF16), ln2_g[i][None, :], ln2_b[i][None, :])
    return x32.reshape(bsz, seq, d)
```

```python
import functools

import jax
import jax.numpy as jnp
import numpy as np
from jax import lax
from jax.experimental import pallas as pl
from jax.experimental.pallas import tpu as pltpu

F32 = jnp.float32
BF16 = jnp.bfloat16

D_MODEL = 1024
DEPTH = 4
GRID_W = 64
HEAD_DIM = 64
D_FF = 4 * D_MODEL
D_PLE = 256
NA_HEADS = 16
NA_KH = 8
NA_KW = 16
GQA_KV_HEADS = 4
GQA_GROUP = 4
ROPE_THETA = 10000.0
ML_HEADS = 8
ML_DV = 128
ML_DQK = 64
ML_CHUNK = 64
ML_BLOCK = 256
DN_ALPHA = (2 * DEPTH) ** 0.25
EPS = 1e-6
LOG2E = 1.4426950408889634

V7X_VMEM_BYTES = 64 * 1024 * 1024
V7X_LANES = 128
SLAB = 2 * V7X_LANES

NT_DIMS = (((1,), (1,)), ((), ()))


def _vmem_limit(*byte_counts):
    est = int(sum(byte_counts) * 1.5) + (4 << 20)
    return min(est, V7X_VMEM_BYTES - (6 << 20))


def _nbytes(shape, dtype):
    return int(np.prod(shape)) * jnp.dtype(dtype).itemsize


def _const_spec(shape):
    nd = len(shape)
    return pl.BlockSpec(shape, lambda *_: (0,) * nd, pipeline_mode=pl.Buffered(1))


def _layer_norm(y, g, b):
    mu = jnp.mean(y, -1, keepdims=True)
    yc = y - mu
    var = jnp.mean(yc * yc, -1, keepdims=True)
    return yc * lax.rsqrt(var + EPS) * g + b


def _proj_kernel(x_ref, w_ref, o_ref, *, n_chunk):
    xb = x_ref[...].astype(BF16)
    for j in range(o_ref.shape[1] // n_chunk):
        sl = slice(j * n_chunk, (j + 1) * n_chunk)
        o_ref[:, sl] = jnp.dot(xb, w_ref[:, sl], preferred_element_type=F32).astype(BF16)


def _proj(x, w, tm=512, n_chunk=1024):
    t, d = x.shape
    n = w.shape[1]
    return pl.pallas_call(
        functools.partial(_proj_kernel, n_chunk=n_chunk),
        out_shape=jax.ShapeDtypeStruct((t, n), BF16),
        grid=(t // tm,),
        in_specs=[pl.BlockSpec((tm, d), lambda i: (i, 0)), _const_spec((d, n))],
        out_specs=pl.BlockSpec((tm, n), lambda i: (i, 0)),
        compiler_params=pltpu.CompilerParams(
            dimension_semantics=("parallel",),
            vmem_limit_bytes=_vmem_limit(2 * _nbytes((tm, d), x.dtype), _nbytes((d, n), BF16),
                                         2 * _nbytes((tm, n), BF16), _nbytes((tm, n_chunk), F32))),
        name="proj_plain",
    )(x, w)


def _na_kernel(q_ref, k_ref, v_ref, bias_ref, o_ref, s_ref, p_ref, *, rows, group):
    lane_head = lax.broadcasted_iota(jnp.int32, (1, SLAB), 1) // HEAD_DIM
    win = NA_KH * GRID_W

    def window(r):
        r = jnp.clip(r, 0, rows - 1)
        r0 = jnp.clip(r - NA_KH // 2, 0, rows - NA_KH)
        return pl.multiple_of(r * GRID_W, GRID_W), pl.multiple_of(r0 * GRID_W, GRID_W), r - r0

    def scores(r, slot):
        q0, k0, delta = window(r)
        q = q_ref[pl.ds(q0, GRID_W), :]
        qs = jnp.concatenate([jnp.where(lane_head == h, q, jnp.zeros_like(q)) for h in range(4)], axis=0)
        s = lax.dot_general(qs, k_ref[pl.ds(k0, win), :], NT_DIMS, preferred_element_type=F32)
        s_ref[slot] = s * (HEAD_DIM ** -0.5 * LOG2E) + bias_ref[0, delta]

    def softmax(slot):
        s = s_ref[slot]
        e = jnp.exp2(s - jnp.max(s, -1, keepdims=True))
        p_ref[slot] = (e * (1.0 / jnp.sum(e, -1, keepdims=True))).astype(BF16)

    def weighted_values(r, slot):
        q0, k0, _ = window(r)
        pv = jnp.dot(p_ref[slot], v_ref[pl.ds(k0, win), :], preferred_element_type=F32)
        acc = jnp.zeros((GRID_W, SLAB), F32)
        for h in range(4):
            acc = jnp.where(lane_head == h, pv[h * GRID_W:(h + 1) * GRID_W], acc)
        o_ref[pl.ds(q0, GRID_W), :] = acc.astype(BF16)

    def step(g, bank):
        for t in range(group):
            weighted_values((g - 1) * group + t, (1 - bank) * group + t)
        for t in range(group):
            scores((g + 1) * group + t, (1 - bank) * group + t)
        for t in range(group):
            softmax(bank * group + t)

    for t in range(group):
        scores(t, t)
        p_ref[group + t] = jnp.zeros(p_ref.shape[1:], BF16)

    def body(j, carry):
        step(2 * j, 0)
        step(2 * j + 1, 1)
        return carry

    n_step = rows // group
    lax.fori_loop(0, n_step // 2, body, 0)
    for t in range(group):
        weighted_values(rows - group + t, ((n_step - 1) % 2) * group + t)


def _na_bias_table(rpb):
    col = np.arange(GRID_W)
    c0 = np.clip(col - NA_KW // 2, 0, GRID_W - NA_KW)
    col_in = (col[None, :] >= c0[:, None]) & (col[None, :] < c0[:, None] + NA_KW)
    dc = np.clip(col[None, :] - col[:, None], 1 - NA_KW, NA_KW - 1) + NA_KW - 1
    rpb = rpb.astype(F32)
    by_col = jnp.zeros(rpb.shape[:2] + dc.shape, F32)
    for c in range(2 * NA_KW - 1):
        by_col = jnp.where(dc[None, None] == c, rpb[:, :, c][:, :, None, None], by_col)
    by_col = jnp.where(col_in[None, None], by_col * LOG2E, -jnp.inf)
    per_delta = [by_col[:, NA_KH - 1 - dl:2 * NA_KH - 1 - dl].transpose(0, 2, 1, 3)
                 .reshape(NA_HEADS, GRID_W, NA_KH * GRID_W) for dl in range(NA_KH)]
    b = jnp.stack(per_delta, axis=1).reshape(NA_HEADS // 4, 4, NA_KH, GRID_W, NA_KH * GRID_W)
    return b.transpose(0, 2, 1, 3, 4).reshape(NA_HEADS // 4, NA_KH, 4 * GRID_W, NA_KH * GRID_W)


def _na_attention(qkv, bias, bsz, seq, group=2):
    n_slab = D_MODEL // SLAB
    rows = seq // GRID_W
    assert rows % (2 * group) == 0
    blk = (seq, SLAB)
    tile = (4 * GRID_W, NA_KH * GRID_W)
    return pl.pallas_call(
        functools.partial(_na_kernel, rows=rows, group=group),
        out_shape=jax.ShapeDtypeStruct((bsz * seq, D_MODEL), BF16),
        grid=(n_slab, bsz),
        in_specs=[pl.BlockSpec(blk, lambda s, b: (b, s)),
                  pl.BlockSpec(blk, lambda s, b: (b, n_slab + s)),
                  pl.BlockSpec(blk, lambda s, b: (b, 2 * n_slab + s)),
                  pl.BlockSpec((1,) + bias.shape[1:], lambda s, b: (s, 0, 0, 0))],
        out_specs=pl.BlockSpec(blk, lambda s, b: (b, s)),
        scratch_shapes=[pltpu.VMEM((2 * group,) + tile, F32), pltpu.VMEM((2 * group,) + tile, BF16)],
        compiler_params=pltpu.CompilerParams(
            dimension_semantics=("parallel", "parallel"),
            vmem_limit_bytes=_vmem_limit(8 * _nbytes(blk, BF16), 2 * _nbytes(bias.shape[1:], F32),
                                         (3 * group + 4) * _nbytes(tile, F32))),
        name="na_attention",
    )(qkv, qkv, qkv, bias)


def _gqa_proj_kernel(x_ref, w_ref, cos_ref, sin_ref, gq_ref, gk_ref, ones_ref, o_ref):
    xb = x_ref[...]
    cos = cos_ref[...]
    sin = sin_ref[...]
    ones = ones_ref[...]
    n_norm = 2 * D_MODEL // (2 * SLAB)

    def project(pair):
        return jnp.dot(xb, w_ref[:, pair * 2 * SLAB:(pair + 1) * 2 * SLAB], preferred_element_type=F32)

    z_next = project(0)
    for pair in range(n_norm):
        z = z_next
        if pair + 1 < n_norm:
            z_next = project(pair + 1)
        else:
            c0 = 2 * D_MODEL
            o_ref[:, c0:] = jnp.dot(xb, w_ref[:, c0:], preferred_element_type=F32).astype(BF16)
        halves = [(z[:, j * SLAB:j * SLAB + V7X_LANES], z[:, j * SLAB + V7X_LANES:(j + 1) * SLAB]) for j in range(2)]
        ss = jnp.concatenate([a * a + b * b for a, b in halves], axis=1)
        hi = ss.astype(BF16)
        lo = (ss - hi.astype(F32)).astype(BF16)
        ms = (jnp.dot(hi, ones, preferred_element_type=F32)
              + jnp.dot(lo, ones, preferred_element_type=F32)) * (1.0 / HEAD_DIM)
        rs = lax.rsqrt(ms + EPS)
        is_q = pair < n_norm // 2
        g_ref = gq_ref if is_q else gk_ref
        for j, (a, b) in enumerate(halves):
            r = rs[:, j * V7X_LANES:(j + 1) * V7X_LANES]
            an = a * r * g_ref[:, :V7X_LANES]
            bn = b * r * g_ref[:, V7X_LANES:]
            oa = an * cos - bn * sin
            ob = an * sin + bn * cos
            if is_q:
                oa = oa * (HEAD_DIM ** -0.5 * LOG2E)
                ob = ob * (HEAD_DIM ** -0.5 * LOG2E)
            c0 = (pair * 2 + j) * SLAB
            o_ref[:, c0:c0 + V7X_LANES] = oa.astype(BF16)
            o_ref[:, c0 + V7X_LANES:c0 + SLAB] = ob.astype(BF16)


def _gqa_weight_layout(w):
    d = w.shape[0]
    kvd = GQA_KV_HEADS * HEAD_DIM
    half = HEAD_DIM // 2
    wq = w[:, :D_MODEL].reshape(d, GQA_KV_HEADS, GQA_GROUP, half, 2).transpose(0, 1, 4, 2, 3)
    wk = w[:, D_MODEL:D_MODEL + kvd].reshape(d, GQA_KV_HEADS, half, 2).transpose(0, 1, 3, 2)
    wk = jnp.broadcast_to(wk[:, :, :, None, :], (d, GQA_KV_HEADS, 2, GQA_GROUP, half))
    wv = w[:, D_MODEL + kvd:].reshape(d, GQA_KV_HEADS, 1, HEAD_DIM)
    wv = jnp.broadcast_to(wv, (d, GQA_KV_HEADS, GQA_GROUP, HEAD_DIM))
    return jnp.concatenate([wq.reshape(d, D_MODEL), wk.reshape(d, D_MODEL), wv.reshape(d, D_MODEL)], axis=1)


def _gqa_gain_layout(g):
    half = HEAD_DIM // 2
    return jnp.broadcast_to(g.reshape(half, 2).T[:, None, :], (2, GQA_GROUP, half)).reshape(1, SLAB)


def _group_sum_matrix():
    blk = np.arange(SLAB) // (HEAD_DIM // 2)
    return jnp.asarray(blk[:, None] == blk[None, :], BF16)


def _rope_tables(seq):
    t = jnp.arange(seq)
    row = (t // GRID_W).astype(F32)
    col = (t % GRID_W).astype(F32)
    n_pairs = HEAD_DIM // 4
    inv = ROPE_THETA ** (-jnp.arange(n_pairs, dtype=F32) / n_pairs)
    ang = jnp.concatenate([row[:, None] * inv, col[:, None] * inv], -1)
    return jnp.tile(jnp.cos(ang), (1, GQA_GROUP)), jnp.tile(jnp.sin(ang), (1, GQA_GROUP))


def _gqa_proj(x16, w, cos4, sin4, gq, gk, ones, seq, tm=512):
    t, d = x16.shape
    n = w.shape[1]
    per_seq = seq // tm
    return pl.pallas_call(
        _gqa_proj_kernel,
        out_shape=jax.ShapeDtypeStruct((t, n), BF16),
        grid=(t // tm,),
        in_specs=[pl.BlockSpec((tm, d), lambda i: (i, 0)), _const_spec((d, n)),
                  pl.BlockSpec((tm, V7X_LANES), lambda i: (i % per_seq, 0)),
                  pl.BlockSpec((tm, V7X_LANES), lambda i: (i % per_seq, 0)),
                  _const_spec((1, SLAB)), _const_spec((1, SLAB)), _const_spec((SLAB, SLAB))],
        out_specs=pl.BlockSpec((tm, n), lambda i: (i, 0)),
        compiler_params=pltpu.CompilerParams(
            dimension_semantics=("parallel",),
            vmem_limit_bytes=_vmem_limit(2 * _nbytes((tm, d), BF16), _nbytes((d, n), BF16),
                                         2 * _nbytes((tm, n), BF16), 6 * _nbytes((tm, 2 * SLAB), F32))),
        name="gqa_proj",
    )(x16, w, cos4, sin4, gq, gk, ones)


def _gqa_attn_kernel(q_ref, k_ref, v_ref, o_ref):
    lane = lax.broadcasted_iota(jnp.int32, (1, SLAB), 1)
    q = q_ref[...]
    q_head = (lane % V7X_LANES) // (HEAD_DIM // 2)

    def scores(h):
        qm = jnp.where(q_head == h, q, jnp.zeros_like(q))
        return lax.dot_general(qm, k_ref[...], NT_DIMS, preferred_element_type=F32)

    def softmax(s):
        e = jnp.exp2(s - jnp.max(s, -1, keepdims=True))
        return e.astype(BF16), 1.0 / jnp.sum(e, -1, keepdims=True)

    def keep_head(out, h, e, inv_l):
        pv = jnp.dot(e, v_ref[...], preferred_element_type=F32) * inv_l
        return jnp.where(lane // HEAD_DIM == h, pv, out)

    out = jnp.zeros(o_ref.shape, F32)
    s_next = scores(0)
    prev = None
    for h in range(GQA_GROUP):
        s = s_next
        if h + 1 < GQA_GROUP:
            s_next = scores(h + 1)
        cur = softmax(s)
        if prev is not None:
            out = keep_head(out, h - 1, *prev)
        prev = cur
    out = keep_head(out, GQA_GROUP - 1, *prev)
    o_ref[...] = out.astype(BF16)


def _gqa_attention(qkv, bsz, seq, tq=1024):
    n_slab = D_MODEL // SLAB
    nq = seq // tq
    return pl.pallas_call(
        _gqa_attn_kernel,
        out_shape=jax.ShapeDtypeStruct((bsz * seq, D_MODEL), BF16),
        grid=(bsz, n_slab, nq),
        in_specs=[pl.BlockSpec((tq, SLAB), lambda b, g, i: (b * nq + i, g)),
                  pl.BlockSpec((seq, SLAB), lambda b, g, i: (b, n_slab + g)),
                  pl.BlockSpec((seq, SLAB), lambda b, g, i: (b, 2 * n_slab + g))],
        out_specs=pl.BlockSpec((tq, SLAB), lambda b, g, i: (b * nq + i, g)),
        compiler_params=pltpu.CompilerParams(
            dimension_semantics=("parallel", "parallel", "parallel"),
            vmem_limit_bytes=_vmem_limit(4 * _nbytes((tq, SLAB), BF16), 8 * _nbytes((seq, SLAB), BF16),
                                         3 * _nbytes((tq, seq), F32))),
        name="gqa_attention",
    )(qkv, qkv, qkv)


def _ml_proj_kernel(x_ref, w_ref, wkt_ref, wgt_ref, bg_ref, q_ref, k_ref, v_ref, o_ref, kt_ref, gt_ref):
    xb = x_ref[...]
    qk_w = ML_HEADS * ML_DQK
    q_ref[...] = (jnp.dot(xb, w_ref[:, :qk_w], preferred_element_type=F32) * (ML_DQK ** -0.5)).astype(BF16)
    k_ref[...] = jnp.dot(xb, w_ref[:, qk_w:2 * qk_w], preferred_element_type=F32).astype(BF16)
    v_ref[...] = jnp.dot(xb, w_ref[:, 2 * qk_w:2 * qk_w + D_MODEL], preferred_element_type=F32).astype(BF16)
    o_ref[...] = jax.nn.sigmoid(jnp.dot(xb, w_ref[:, 2 * qk_w + D_MODEL:], preferred_element_type=F32))
    kt = lax.dot_general(wkt_ref[...], xb, NT_DIMS, preferred_element_type=F32)
    for j in range(kt_ref.shape[1]):
        kt_ref[0, j] = kt[:, j * ML_BLOCK:(j + 1) * ML_BLOCK]
    gt_ref[0] = lax.dot_general(wgt_ref[...], xb, NT_DIMS, preferred_element_type=F32) + bg_ref[...]


def _ml_proj(x16, w, wkt, wgt, bg, bsz, seq, tm=512):
    t, d = x16.shape
    qk_w = ML_HEADS * ML_DQK
    per_seq = seq // tm
    n_gate = wgt.shape[0]
    row = lambda i: (i, 0)
    return pl.pallas_call(
        _ml_proj_kernel,
        out_shape=(jax.ShapeDtypeStruct((t, qk_w), BF16), jax.ShapeDtypeStruct((t, qk_w), BF16),
                   jax.ShapeDtypeStruct((t, D_MODEL), BF16), jax.ShapeDtypeStruct((t, D_MODEL), F32),
                   jax.ShapeDtypeStruct((bsz, seq // ML_BLOCK, qk_w, ML_BLOCK), F32),
                   jax.ShapeDtypeStruct((bsz, n_gate, seq), F32)),
        grid=(t // tm,),
        in_specs=[pl.BlockSpec((tm, d), row), _const_spec(w.shape), _const_spec(wkt.shape),
                  _const_spec(wgt.shape), _const_spec(bg.shape)],
        out_specs=(pl.BlockSpec((tm, qk_w), row), pl.BlockSpec((tm, qk_w), row),
                   pl.BlockSpec((tm, D_MODEL), row), pl.BlockSpec((tm, D_MODEL), row),
                   pl.BlockSpec((1, tm // ML_BLOCK, qk_w, ML_BLOCK), lambda i: (i // per_seq, i % per_seq, 0, 0)),
                   pl.BlockSpec((1, n_gate, tm), lambda i: (i // per_seq, 0, i % per_seq))),
        compiler_params=pltpu.CompilerParams(
            dimension_semantics=("parallel",),
            vmem_limit_bytes=_vmem_limit(2 * _nbytes((tm, d), BF16), _nbytes(w.shape, BF16), _nbytes(wkt.shape, BF16),
                                         2 * _nbytes((tm, 2 * qk_w + D_MODEL), BF16), 2 * _nbytes((tm, D_MODEL), F32),
                                         4 * _nbytes((qk_w, tm), F32), 2 * _nbytes((tm, D_MODEL), F32))),
        name="mlstm_proj",
    )(x16, w, wkt, wgt, bg)


def _log_sigmoid(x):
    return jnp.minimum(x, 0.0) - jnp.log1p(jnp.exp(-jnp.abs(x)))


def _exact_dot(x, m):
    hi = x.astype(BF16)
    r1 = x - hi.astype(F32)
    mid = r1.astype(BF16)
    lo = (r1 - mid.astype(F32)).astype(BF16)
    return (jnp.dot(hi, m, preferred_element_type=F32) + jnp.dot(mid, m, preferred_element_type=F32)
            + jnp.dot(lo, m, preferred_element_type=F32))


def _lane_cummax(x, lane_pos, reverse):
    shift = 1
    while shift < ML_CHUNK:
        if reverse:
            moved, ok = pltpu.roll(x, x.shape[1] - shift, axis=1), lane_pos < ML_CHUNK - shift
        else:
            moved, ok = pltpu.roll(x, shift, axis=1), lane_pos >= shift
        x = jnp.maximum(x, jnp.where(ok, moved, -jnp.inf))
        shift *= 2
    return x


def _mlstm_block(q, k, kt, v, a_row, ge_row, decay_row, m_row, g_col, b_col, c_state, n_state, vis, same, row_chunk,
                 reverse):
    per = ML_BLOCK // ML_CHUNK
    v_ext = jnp.concatenate([v, jnp.ones_like(v)], axis=1)
    wkt = jnp.exp(a_row - ge_row) * kt
    wkt4 = jnp.where(same, jnp.concatenate([wkt] * per, axis=0), 0.0)
    delta = jnp.dot(wkt4.astype(BF16), v_ext, preferred_element_type=F32)
    yield
    g_rep = jnp.broadcast_to(g_col, (ML_BLOCK, ML_DV))
    b_rep = jnp.broadcast_to(b_col, (ML_BLOCK, ML_DV))
    m_rep = jnp.broadcast_to(m_row[:, (per - 1) * ML_CHUNK:(per - 1) * ML_CHUNK + 1], g_rep.shape)
    for i in range(per - 2, -1, -1):
        m_rep = jnp.where(row_chunk == i, m_row[:, i * ML_CHUNK:i * ML_CHUNK + 1], m_rep)
    s_inter = jnp.exp(m_rep - g_rep)
    floor_rep = jnp.exp(-(b_rep + g_rep))
    w = jnp.where(vis, jnp.exp(a_row - jnp.concatenate([g_rep, g_rep], axis=1)), 0.0)
    qk = lax.dot_general(q, k, NT_DIMS, preferred_element_type=F32) * w
    yield
    intra = jnp.dot(qk.astype(BF16), v_ext, preferred_element_type=F32)
    starts = [None] * per
    state = jnp.concatenate([c_state, n_state], axis=1)
    for i in (range(per - 1, -1, -1) if reverse else range(per)):
        starts[i] = state
        state = decay_row[:, i * ML_CHUNK:i * ML_CHUNK + 1] * state + delta[i * ML_DQK:(i + 1) * ML_DQK]
    yield
    q4 = jnp.where(same, jnp.concatenate([q] * per, axis=1), jnp.zeros((), q.dtype))
    inter = jnp.dot(q4, jnp.concatenate(starts, axis=0).astype(BF16), preferred_element_type=F32)
    num = s_inter * inter[:, :ML_DV] + intra[:, :ML_DV]
    den = s_inter * inter[:, ML_DV:] + intra[:, ML_DV:]
    h = num / jnp.maximum(jnp.abs(den), floor_rep)
    yield h, state[:, :ML_DV], state[:, ML_DV:]


def _mlstm_kernel(q_ref, k_ref, kt_ref, v_ref, o_ref, g_ref, ng_ref, out_ref, hfw_ref, hbw_ref, row_ref, col_ref,
                  *, n_block):
    pair = pl.program_id(1)
    per = ML_BLOCK // ML_CHUNK
    t_idx = lax.broadcasted_iota(jnp.int32, (ML_BLOCK, ML_BLOCK), 0)
    s_idx = lax.broadcasted_iota(jnp.int32, (ML_BLOCK, ML_BLOCK), 1)
    same = (t_idx // ML_CHUNK) == (s_idx // ML_CHUNK)
    masks = (same & (s_idx <= t_idx), same & (s_idx >= t_idx))
    lane = lax.broadcasted_iota(jnp.int32, (1, ML_BLOCK), 1)
    lane_chunk, lane_pos = lane // ML_CHUNK, lane % ML_CHUNK
    blk_row = lax.broadcasted_iota(jnp.int32, (n_block, 1), 0)
    row_chunk = lax.broadcasted_iota(jnp.int32, (ML_BLOCK, ML_DV), 0) // ML_CHUNK

    per_query = []
    for direction in range(2):
        cum = jnp.where(masks[1 - direction], 1.0, 0.0).astype(BF16)
        for hh in range(2):
            chain = direction * 2 + hh
            head = pair * 2 + hh
            lf = _log_sigmoid(g_ref[0, (direction * 2 + 1) * ML_HEADS + head])
            b = _exact_dot(lf, cum)
            a = g_ref[0, direction * 2 * ML_HEADS + head] - b
            a_max = [jnp.max(jnp.where(lane_chunk == i, a, -jnp.inf), axis=1, keepdims=True) for i in range(per)]
            f_sum = [jnp.sum(jnp.where(lane_chunk == i, lf, 0.0), axis=1, keepdims=True) for i in range(per)]
            m = jnp.zeros((1, 1), F32)
            m_row = jnp.zeros(a.shape, F32)
            ge_row = jnp.zeros(a.shape, F32)
            n_chunk = n_block * per
            for c in (range(n_chunk - 1, -1, -1) if direction else range(n_chunk)):
                blk, i = divmod(c, per)
                g_end = jnp.maximum(m, a_max[i][blk:blk + 1])
                here = (blk_row == blk) & (lane_chunk == i)
                m_row = jnp.where(here, m, m_row)
                ge_row = jnp.where(here, g_end, ge_row)
                m = f_sum[i][blk:blk + 1] + g_end
            g_row = jnp.maximum(m_row, _lane_cummax(a, lane_pos, reverse=bool(direction)))
            row_ref[chain, 0] = a
            row_ref[chain, 1] = ge_row
            row_ref[chain, 2] = jnp.exp(m_row - ge_row)
            row_ref[chain, 3] = m_row
            per_query += [g_row, b]
    flat = [jnp.concatenate([x[blk:blk + 1] for blk in range(n_block)], axis=1) for x in per_query]
    col_ref[...] = jnp.concatenate(flat, axis=0).T

    def body(step, carry):
        chains = []
        for direction in range(2):
            blk = step if direction == 0 else n_block - 1 - step
            rows = pl.ds(pl.multiple_of(blk * ML_BLOCK, ML_BLOCK), ML_BLOCK)
            for hh in range(2):
                chain = direction * 2 + hh
                c_state, n_state = carry[chain]
                stages = _mlstm_block(
                    q_ref[rows, hh * ML_DQK:(hh + 1) * ML_DQK], k_ref[rows, hh * ML_DQK:(hh + 1) * ML_DQK],
                    kt_ref[0, blk, hh * ML_DQK:(hh + 1) * ML_DQK, :], v_ref[rows, hh * ML_DV:(hh + 1) * ML_DV],
                    *[row_ref[chain, j, pl.ds(blk, 1), :] for j in range(4)],
                    col_ref[rows, 2 * chain:2 * chain + 1], col_ref[rows, 2 * chain + 1:2 * chain + 2],
                    c_state, n_state, masks[direction], same, row_chunk, reverse=bool(direction))
                chains.append((stages, hfw_ref if direction == 0 else hbw_ref, rows, hh))
        for _ in range(3):
            for stages, _, _, _ in chains:
                next(stages)
        new_carry = []
        for stages, dst, rows, hh in chains:
            h, c_new, n_new = next(stages)
            dst[rows, hh * ML_DV:(hh + 1) * ML_DV] = h
            new_carry.append((c_new, n_new))
        return tuple(new_carry)

    init = tuple((jnp.zeros((ML_DQK, ML_DV), F32), jnp.zeros((ML_DQK, ML_DV), F32)) for _ in range(4))
    lax.fori_loop(0, n_block, body, init)

    for hh in range(2):
        cols = slice(hh * ML_DV, (hh + 1) * ML_DV)
        h = hfw_ref[:, cols] + hbw_ref[:, cols]
        ms = jnp.mean(h * h, -1, keepdims=True)
        hn = h * lax.rsqrt(ms + EPS) * ng_ref[:, cols]
        out_ref[:, cols] = (o_ref[:, cols] * hn).astype(BF16)


def _mlstm(q, k, kt, v, o, gates, norm_g, bsz, seq):
    n_pair = ML_HEADS // 2
    n_block = seq // ML_BLOCK
    pair_w = 2 * ML_DV
    return pl.pallas_call(
        functools.partial(_mlstm_kernel, n_block=n_block),
        out_shape=jax.ShapeDtypeStruct((bsz * seq, D_MODEL), BF16),
        grid=(bsz, n_pair),
        in_specs=[pl.BlockSpec((seq, 2 * ML_DQK), lambda b, p: (b, p)),
                  pl.BlockSpec((seq, 2 * ML_DQK), lambda b, p: (b, p)),
                  pl.BlockSpec((1, n_block, 2 * ML_DQK, ML_BLOCK), lambda b, p: (b, 0, p, 0)),
                  pl.BlockSpec((seq, pair_w), lambda b, p: (b, p)),
                  pl.BlockSpec((seq, pair_w), lambda b, p: (b, p)),
                  pl.BlockSpec((1,) + gates.shape[1:], lambda b, p: (b, 0, 0, 0)),
                  pl.BlockSpec((1, pair_w), lambda b, p: (0, p))],
        out_specs=pl.BlockSpec((seq, pair_w), lambda b, p: (b, p)),
        scratch_shapes=[pltpu.VMEM((seq, pair_w), F32), pltpu.VMEM((seq, pair_w), F32),
                        pltpu.VMEM((4, 4, n_block, ML_BLOCK), F32), pltpu.VMEM((seq, 8), F32)],
        compiler_params=pltpu.CompilerParams(
            dimension_semantics=("parallel", "parallel"),
            vmem_limit_bytes=_vmem_limit(4 * _nbytes((seq, 2 * ML_DQK), BF16), 2 * _nbytes((seq, 2 * ML_DQK), F32),
                                         4 * _nbytes((seq, pair_w), BF16), 2 * _nbytes((seq, pair_w), F32),
                                         2 * _nbytes(gates.shape[1:], F32), 2 * _nbytes((seq, pair_w), F32))),
        name="mlstm_scan",
    )(q, k, kt, v, o, gates, norm_g)


def _block_tail_kernel(mix_ref, x_ref, p_ref, wo_ref, g1_ref, b1_ref, w1_ref, w2_ref, wg_ref, wp_ref, g2_ref, b2_ref,
                       o32_ref, o16_ref, *, ff_chunk, parts):
    tr = x_ref.shape[0] // parts
    rows = [slice(i * tr, (i + 1) * tr) for i in range(parts)]

    def out_proj(r):
        return jnp.dot(mix_ref[r, :], wo_ref[...], preferred_element_type=F32) + DN_ALPHA * x_ref[r, :]

    def ffn_chunk(xb, c):
        sl = slice(c * ff_chunk, (c + 1) * ff_chunk)
        h = jnp.maximum(jnp.dot(xb, w1_ref[:, sl], preferred_element_type=F32), 0.0)
        return jnp.dot((h * h).astype(BF16), w2_ref[sl, :], preferred_element_type=F32)

    def finish(x1, acc, r):
        z = _layer_norm(DN_ALPHA * x1 + acc, g2_ref[...], b2_ref[...])
        o32_ref[r, :] = z
        o16_ref[r, :] = z.astype(BF16)

    y = out_proj(rows[0])
    done = None
    for i in range(parts):
        y_next = out_proj(rows[i + 1]) if i + 1 < parts else None
        x1 = _layer_norm(y, g1_ref[...], b1_ref[...])
        xb = x1.astype(BF16)
        gate = jax.nn.sigmoid(jnp.dot(xb, wg_ref[...], preferred_element_type=F32))
        acc = gate * jnp.dot(p_ref[rows[i], :].astype(BF16), wp_ref[...], preferred_element_type=F32)
        for c in range(w1_ref.shape[1] // ff_chunk):
            acc = acc + ffn_chunk(xb, c)
            if c == 0 and done is not None:
                finish(*done)
        done = (x1, acc, rows[i])
        y = y_next
    finish(*done)


def _block_tail(mix, x32, p, layer, wo, g1, b1, w1, w2, wg, wp, g2, b2, tm=512, ff_chunk=1024, parts=2):
    t, d = x32.shape
    row = lambda i: (i, 0)
    vec = _const_spec((1, d))
    return pl.pallas_call(
        functools.partial(_block_tail_kernel, ff_chunk=ff_chunk, parts=parts),
        out_shape=(jax.ShapeDtypeStruct((t, d), F32), jax.ShapeDtypeStruct((t, d), BF16)),
        grid=(t // tm,),
        in_specs=[pl.BlockSpec((tm, d), row), pl.BlockSpec((tm, d), row),
                  pl.BlockSpec((None, tm, p.shape[2]), lambda i: (layer, i, 0)),
                  _const_spec(wo.shape), vec, vec, _const_spec(w1.shape), _const_spec(w2.shape),
                  _const_spec(wg.shape), _const_spec(wp.shape), vec, vec],
        out_specs=(pl.BlockSpec((tm, d), row), pl.BlockSpec((tm, d), row)),
        compiler_params=pltpu.CompilerParams(
            dimension_semantics=("parallel",),
            vmem_limit_bytes=_vmem_limit(4 * _nbytes((tm, d), BF16), 4 * _nbytes((tm, d), F32),
                                         2 * _nbytes((tm, p.shape[2]), F32), _nbytes(wo.shape, BF16),
                                         _nbytes(w1.shape, BF16), _nbytes(w2.shape, BF16), _nbytes(wg.shape, BF16),
                                         _nbytes(wp.shape, BF16), 2 * _nbytes((tm, ff_chunk), F32),
                                         4 * _nbytes((tm, d), F32))),
        name="block_tail",
    )(mix, x32, p, wo, g1, b1, w1, w2, wg, wp, g2, b2)


def kernel(x, p, na_w_qkv, na_rpb, na_w_o, gq_w_qkv, gq_q_norm, gq_k_norm, gq_w_o, ml_w_in, ml_b_gates, ml_norm_g,
           ml_w_o, ln1_g, ln1_b, w_ff1, w_ff2, ln2_g, ln2_b, w_ple_gate, w_ple_proj):
    bsz, seq, d = x.shape
    assert d == D_MODEL and seq % (NA_KH * GRID_W) == 0 and p.shape == (DEPTH, bsz, seq, D_PLE)
    t = bsz * seq
    x32 = x.reshape(t, d)
    p_tok = p.reshape(DEPTH, t, D_PLE)
    x16 = None
    qk_w = ML_HEADS * ML_DQK
    for i in range(DEPTH):
        kind, j = i % 3, i // 3
        if kind == 0:
            qkv = _proj(x32 if x16 is None else x16, na_w_qkv[j].astype(BF16))
            mix = _na_attention(qkv, _na_bias_table(na_rpb[j]), bsz, seq)
            w_o = na_w_o[j]
        elif kind == 1:
            cos4, sin4 = _rope_tables(seq)
            qkv = _gqa_proj(x16, _gqa_weight_layout(gq_w_qkv[j]).astype(BF16), cos4, sin4,
                            _gqa_gain_layout(gq_q_norm[j]), _gqa_gain_layout(gq_k_norm[j]), _group_sum_matrix(), seq)
            mix = _gqa_attention(qkv, bsz, seq)
            w_o = gq_w_o[j]
        else:
            w_in = ml_w_in[j]
            n_main = 2 * qk_w + 2 * D_MODEL
            wkt = w_in[:, qk_w:2 * qk_w].T.astype(BF16)
            wgt = w_in[:, n_main:].T.astype(BF16)
            q, k, v, o, kt, gt = _ml_proj(x16, w_in[:, :n_main].astype(BF16), wkt, wgt, ml_b_gates[j][:, None], bsz, seq)
            gates = gt.reshape(bsz, gt.shape[1], seq // ML_BLOCK, ML_BLOCK)
            mix = _mlstm(q, k, kt, v, o, gates, ml_norm_g[j][None, :], bsz, seq)
            w_o = ml_w_o[j]
        x32, x16 = _block_tail(mix, x32, p_tok, i, w_o.astype(BF16), ln1_g[i][None, :], ln1_b[i][None, :],
                               w_ff1[i].astype(BF16), w_ff2[i].astype(BF16), w_ple_gate[i].astype(BF16),
                               w_ple_proj[i].astype(BF16), ln2_g[i][None, :], ln2_b[i][None, :])
    return x32.reshape(bsz, seq, d)
```

```python
import functools

import jax
import jax.numpy as jnp
import numpy as np
from jax import lax
from jax.experimental import pallas as pl
from jax.experimental.pallas import tpu as pltpu

F32 = jnp.float32
BF16 = jnp.bfloat16

D_MODEL = 1024
DEPTH = 4
GRID_W = 64
HEAD_DIM = 64
D_FF = 4 * D_MODEL
D_PLE = 256
NA_HEADS = 16
NA_KH = 8
NA_KW = 16
GQA_KV_HEADS = 4
GQA_GROUP = 4
ROPE_THETA = 10000.0
ML_HEADS = 8
ML_DV = 128
ML_DQK = 64
ML_CHUNK = 64
ML_BLOCK = 256
DN_ALPHA = (2 * DEPTH) ** 0.25
EPS = 1e-6
LOG2E = 1.4426950408889634

V7X_VMEM_BYTES = 64 * 1024 * 1024
V7X_LANES = 128
SLAB = 2 * V7X_LANES

NT_DIMS = (((1,), (1,)), ((), ()))


def _vmem_limit(*byte_counts):
    est = int(sum(byte_counts) * 1.5) + (4 << 20)
    return min(est, V7X_VMEM_BYTES - (6 << 20))


def _nbytes(shape, dtype):
    return int(np.prod(shape)) * jnp.dtype(dtype).itemsize


def _const_spec(shape):
    nd = len(shape)
    return pl.BlockSpec(shape, lambda *_: (0,) * nd, pipeline_mode=pl.Buffered(1))


def _layer_norm(y, g, b):
    mu = jnp.mean(y, -1, keepdims=True)
    yc = y - mu
    var = jnp.mean(yc * yc, -1, keepdims=True)
    return yc * lax.rsqrt(var + EPS) * g + b


def _proj_kernel(x_ref, w_ref, o_ref, *, n_chunk, first_scale):
    xb = x_ref[...].astype(BF16)
    for j in range(o_ref.shape[1] // n_chunk):
        sl = slice(j * n_chunk, (j + 1) * n_chunk)
        y = jnp.dot(xb, w_ref[:, sl], preferred_element_type=F32)
        if j == 0 and first_scale != 1.0:
            y = y * first_scale
        o_ref[:, sl] = y.astype(BF16)


def _proj(x, w, tm=512, n_chunk=1024, first_scale=1.0):
    t, d = x.shape
    n = w.shape[1]
    return pl.pallas_call(
        functools.partial(_proj_kernel, n_chunk=n_chunk, first_scale=first_scale),
        out_shape=jax.ShapeDtypeStruct((t, n), BF16),
        grid=(t // tm,),
        in_specs=[pl.BlockSpec((tm, d), lambda i: (i, 0)), _const_spec((d, n))],
        out_specs=pl.BlockSpec((tm, n), lambda i: (i, 0)),
        compiler_params=pltpu.CompilerParams(
            dimension_semantics=("parallel",),
            vmem_limit_bytes=_vmem_limit(2 * _nbytes((tm, d), x.dtype), _nbytes((d, n), BF16),
                                         2 * _nbytes((tm, n), BF16), _nbytes((tm, n_chunk), F32))),
        name="proj_plain",
    )(x, w)


def _na_kernel(q_ref, k_ref, v_ref, bias_ref, o_ref, s_ref, p_ref, *, rows, group):
    lane_head = lax.broadcasted_iota(jnp.int32, (1, SLAB), 1) // HEAD_DIM
    win = NA_KH * GRID_W

    def window(r):
        r = jnp.clip(r, 0, rows - 1)
        r0 = jnp.clip(r - NA_KH // 2, 0, rows - NA_KH)
        return pl.multiple_of(r * GRID_W, GRID_W), pl.multiple_of(r0 * GRID_W, GRID_W), r - r0

    def scores(r, slot):
        q0, k0, delta = window(r)
        q = q_ref[pl.ds(q0, GRID_W), :]
        qs = jnp.concatenate([jnp.where(lane_head == h, q, jnp.zeros_like(q)) for h in range(4)], axis=0)
        s = lax.dot_general(qs, k_ref[pl.ds(k0, win), :], NT_DIMS, preferred_element_type=F32)
        s_ref[slot] = s + bias_ref[0, delta]

    def softmax(slot):
        s = s_ref[slot]
        e = jnp.exp2(s - jnp.max(s, -1, keepdims=True))
        p_ref[slot] = (e * (1.0 / jnp.sum(e, -1, keepdims=True))).astype(BF16)

    def weighted_values(r, slot):
        q0, k0, _ = window(r)
        pv = jnp.dot(p_ref[slot], v_ref[pl.ds(k0, win), :], preferred_element_type=F32)
        acc = jnp.zeros((GRID_W, SLAB), F32)
        for h in range(4):
            acc = jnp.where(lane_head == h, pv[h * GRID_W:(h + 1) * GRID_W], acc)
        o_ref[pl.ds(q0, GRID_W), :] = acc.astype(BF16)

    def step(g, bank):
        for t in range(group):
            weighted_values((g - 1) * group + t, (1 - bank) * group + t)
        for t in range(group):
            scores((g + 1) * group + t, (1 - bank) * group + t)
        for t in range(group):
            softmax(bank * group + t)

    for t in range(group):
        scores(t, t)
        p_ref[group + t] = jnp.zeros(p_ref.shape[1:], BF16)

    def body(j, carry):
        step(2 * j, 0)
        step(2 * j + 1, 1)
        return carry

    n_step = rows // group
    lax.fori_loop(0, n_step // 2, body, 0)
    for t in range(group):
        weighted_values(rows - group + t, ((n_step - 1) % 2) * group + t)


def _na_bias_table(rpb):
    col = np.arange(GRID_W)
    c0 = np.clip(col - NA_KW // 2, 0, GRID_W - NA_KW)
    col_in = (col[None, :] >= c0[:, None]) & (col[None, :] < c0[:, None] + NA_KW)
    dc = np.clip(col[None, :] - col[:, None], 1 - NA_KW, NA_KW - 1) + NA_KW - 1
    rpb = rpb.astype(F32)
    by_col = jnp.zeros(rpb.shape[:2] + dc.shape, F32)
    for c in range(2 * NA_KW - 1):
        by_col = jnp.where(dc[None, None] == c, rpb[:, :, c][:, :, None, None], by_col)
    by_col = jnp.where(col_in[None, None], by_col * LOG2E, -jnp.inf)
    per_delta = [by_col[:, NA_KH - 1 - dl:2 * NA_KH - 1 - dl].transpose(0, 2, 1, 3)
                 .reshape(NA_HEADS, GRID_W, NA_KH * GRID_W) for dl in range(NA_KH)]
    b = jnp.stack(per_delta, axis=1).reshape(NA_HEADS // 4, 4, NA_KH, GRID_W, NA_KH * GRID_W)
    return b.transpose(0, 2, 1, 3, 4).reshape(NA_HEADS // 4, NA_KH, 4 * GRID_W, NA_KH * GRID_W)


def _na_attention(qkv, bias, bsz, seq, group=2):
    n_slab = D_MODEL // SLAB
    rows = seq // GRID_W
    assert rows % (2 * group) == 0
    blk = (seq, SLAB)
    tile = (4 * GRID_W, NA_KH * GRID_W)
    return pl.pallas_call(
        functools.partial(_na_kernel, rows=rows, group=group),
        out_shape=jax.ShapeDtypeStruct((bsz * seq, D_MODEL), BF16),
        grid=(n_slab, bsz),
        in_specs=[pl.BlockSpec(blk, lambda s, b: (b, s)),
                  pl.BlockSpec(blk, lambda s, b: (b, n_slab + s)),
                  pl.BlockSpec(blk, lambda s, b: (b, 2 * n_slab + s)),
                  pl.BlockSpec((1,) + bias.shape[1:], lambda s, b: (s, 0, 0, 0))],
        out_specs=pl.BlockSpec(blk, lambda s, b: (b, s)),
        scratch_shapes=[pltpu.VMEM((2 * group,) + tile, F32), pltpu.VMEM((2 * group,) + tile, BF16)],
        compiler_params=pltpu.CompilerParams(
            dimension_semantics=("parallel", "parallel"),
            vmem_limit_bytes=_vmem_limit(8 * _nbytes(blk, BF16), 2 * _nbytes(bias.shape[1:], F32),
                                         (3 * group + 4) * _nbytes(tile, F32))),
        name="na_attention",
    )(qkv, qkv, qkv, bias)


def _gqa_proj_kernel(x_ref, w_ref, cos_ref, sin_ref, gq_ref, gk_ref, ones_ref, o_ref):
    xb = x_ref[...]
    cos = cos_ref[...]
    sin = sin_ref[...]
    ones = ones_ref[...]
    n_norm = 2 * D_MODEL // (2 * SLAB)

    def project(pair):
        return jnp.dot(xb, w_ref[:, pair * 2 * SLAB:(pair + 1) * 2 * SLAB], preferred_element_type=F32)

    z_next = project(0)
    for pair in range(n_norm):
        z = z_next
        if pair + 1 < n_norm:
            z_next = project(pair + 1)
        else:
            c0 = 2 * D_MODEL
            o_ref[:, c0:] = jnp.dot(xb, w_ref[:, c0:], preferred_element_type=F32).astype(BF16)
        halves = [(z[:, j * SLAB:j * SLAB + V7X_LANES], z[:, j * SLAB + V7X_LANES:(j + 1) * SLAB]) for j in range(2)]
        ss = jnp.concatenate([a * a + b * b for a, b in halves], axis=1)
        hi = ss.astype(BF16)
        lo = (ss - hi.astype(F32)).astype(BF16)
        ms = (jnp.dot(hi, ones, preferred_element_type=F32)
              + jnp.dot(lo, ones, preferred_element_type=F32)) * (1.0 / HEAD_DIM)
        rs = lax.rsqrt(ms + EPS)
        is_q = pair < n_norm // 2
        g_ref = gq_ref if is_q else gk_ref
        for j, (a, b) in enumerate(halves):
            r = rs[:, j * V7X_LANES:(j + 1) * V7X_LANES]
            an = a * r * g_ref[:, :V7X_LANES]
            bn = b * r * g_ref[:, V7X_LANES:]
            oa = an * cos - bn * sin
            ob = an * sin + bn * cos
            if is_q:
                oa = oa * (HEAD_DIM ** -0.5 * LOG2E)
                ob = ob * (HEAD_DIM ** -0.5 * LOG2E)
            c0 = (pair * 2 + j) * SLAB
            o_ref[:, c0:c0 + V7X_LANES] = oa.astype(BF16)
            o_ref[:, c0 + V7X_LANES:c0 + SLAB] = ob.astype(BF16)


def _gqa_weight_layout(w):
    d = w.shape[0]
    kvd = GQA_KV_HEADS * HEAD_DIM
    half = HEAD_DIM // 2
    wq = w[:, :D_MODEL].reshape(d, GQA_KV_HEADS, GQA_GROUP, half, 2).transpose(0, 1, 4, 2, 3)
    wk = w[:, D_MODEL:D_MODEL + kvd].reshape(d, GQA_KV_HEADS, half, 2).transpose(0, 1, 3, 2)
    wk = jnp.broadcast_to(wk[:, :, :, None, :], (d, GQA_KV_HEADS, 2, GQA_GROUP, half))
    wv = w[:, D_MODEL + kvd:].reshape(d, GQA_KV_HEADS, 1, HEAD_DIM)
    wv = jnp.broadcast_to(wv, (d, GQA_KV_HEADS, GQA_GROUP, HEAD_DIM))
    return jnp.concatenate([wq.reshape(d, D_MODEL), wk.reshape(d, D_MODEL), wv.reshape(d, D_MODEL)], axis=1)


def _gqa_gain_layout(g):
    half = HEAD_DIM // 2
    return jnp.broadcast_to(g.reshape(half, 2).T[:, None, :], (2, GQA_GROUP, half)).reshape(1, SLAB)


def _group_sum_matrix():
    blk = np.arange(SLAB) // (HEAD_DIM // 2)
    return jnp.asarray(blk[:, None] == blk[None, :], BF16)


def _rope_tables(seq):
    t = jnp.arange(seq)
    row = (t // GRID_W).astype(F32)
    col = (t % GRID_W).astype(F32)
    n_pairs = HEAD_DIM // 4
    inv = ROPE_THETA ** (-jnp.arange(n_pairs, dtype=F32) / n_pairs)
    ang = jnp.concatenate([row[:, None] * inv, col[:, None] * inv], -1)
    return jnp.tile(jnp.cos(ang), (1, GQA_GROUP)), jnp.tile(jnp.sin(ang), (1, GQA_GROUP))


def _gqa_proj(x16, w, cos4, sin4, gq, gk, ones, seq, tm=512):
    t, d = x16.shape
    n = w.shape[1]
    per_seq = seq // tm
    return pl.pallas_call(
        _gqa_proj_kernel,
        out_shape=jax.ShapeDtypeStruct((t, n), BF16),
        grid=(t // tm,),
        in_specs=[pl.BlockSpec((tm, d), lambda i: (i, 0)), _const_spec((d, n)),
                  pl.BlockSpec((tm, V7X_LANES), lambda i: (i % per_seq, 0)),
                  pl.BlockSpec((tm, V7X_LANES), lambda i: (i % per_seq, 0)),
                  _const_spec((1, SLAB)), _const_spec((1, SLAB)), _const_spec((SLAB, SLAB))],
        out_specs=pl.BlockSpec((tm, n), lambda i: (i, 0)),
        compiler_params=pltpu.CompilerParams(
            dimension_semantics=("parallel",),
            vmem_limit_bytes=_vmem_limit(2 * _nbytes((tm, d), BF16), _nbytes((d, n), BF16),
                                         2 * _nbytes((tm, n), BF16), 6 * _nbytes((tm, 2 * SLAB), F32))),
        name="gqa_proj",
    )(x16, w, cos4, sin4, gq, gk, ones)


def _gqa_attn_kernel(q_ref, k_ref, v_ref, o_ref, *, row_block):
    lane = lax.broadcasted_iota(jnp.int32, (1, SLAB), 1)
    q = q_ref[...]
    q_head = (lane % V7X_LANES) // (HEAD_DIM // 2)

    def scores(h):
        qm = jnp.where(q_head == h, q, jnp.zeros_like(q))
        return lax.dot_general(qm, k_ref[...], NT_DIMS, preferred_element_type=F32)

    def softmax(s):
        e = jnp.exp2(s - jnp.max(s, -1, keepdims=True))
        return e.astype(BF16), 1.0 / jnp.sum(e, -1, keepdims=True)

    def keep_head(out, h, e, inv_l):
        pv = jnp.dot(e, v_ref[...], preferred_element_type=F32) * inv_l
        return jnp.where(lane // HEAD_DIM == h, pv, out)

    n_rb = q_ref.shape[0] // row_block
    outs = [jnp.zeros((row_block, SLAB), F32) for _ in range(n_rb)]
    s_next = scores(0)
    pending = []
    for h in range(GQA_GROUP):
        s = s_next
        if h + 1 < GQA_GROUP:
            s_next = scores(h + 1)
        current = []
        for rb in range(n_rb):
            current.append(softmax(s[rb * row_block:(rb + 1) * row_block]))
            if pending:
                outs[rb] = keep_head(outs[rb], h - 1, *pending[rb])
        pending = current
    for rb in range(n_rb):
        outs[rb] = keep_head(outs[rb], GQA_GROUP - 1, *pending[rb])
        o_ref[rb * row_block:(rb + 1) * row_block, :] = outs[rb].astype(BF16)


def _gqa_attention(qkv, bsz, seq, tq=1024, row_block=256):
    n_slab = D_MODEL // SLAB
    nq = seq // tq
    return pl.pallas_call(
        functools.partial(_gqa_attn_kernel, row_block=row_block),
        out_shape=jax.ShapeDtypeStruct((bsz * seq, D_MODEL), BF16),
        grid=(bsz, n_slab, nq),
        in_specs=[pl.BlockSpec((tq, SLAB), lambda b, g, i: (b * nq + i, g)),
                  pl.BlockSpec((seq, SLAB), lambda b, g, i: (b, n_slab + g)),
                  pl.BlockSpec((seq, SLAB), lambda b, g, i: (b, 2 * n_slab + g))],
        out_specs=pl.BlockSpec((tq, SLAB), lambda b, g, i: (b * nq + i, g)),
        compiler_params=pltpu.CompilerParams(
            dimension_semantics=("parallel", "parallel", "parallel"),
            vmem_limit_bytes=_vmem_limit(4 * _nbytes((tq, SLAB), BF16), 8 * _nbytes((seq, SLAB), BF16),
                                         3 * _nbytes((tq, seq), F32))),
        name="gqa_attention",
    )(qkv, qkv, qkv)


def _ml_proj_kernel(x_ref, w_ref, wkt_ref, wgt_ref, bg_ref, q_ref, k_ref, v_ref, o_ref, kt_ref, gt_ref):
    xb = x_ref[...]
    qk_w = ML_HEADS * ML_DQK
    q_ref[...] = (jnp.dot(xb, w_ref[:, :qk_w], preferred_element_type=F32) * (ML_DQK ** -0.5)).astype(BF16)
    k_ref[...] = jnp.dot(xb, w_ref[:, qk_w:2 * qk_w], preferred_element_type=F32).astype(BF16)
    v_ref[...] = jnp.dot(xb, w_ref[:, 2 * qk_w:2 * qk_w + D_MODEL], preferred_element_type=F32).astype(BF16)
    o_ref[...] = jax.nn.sigmoid(jnp.dot(xb, w_ref[:, 2 * qk_w + D_MODEL:], preferred_element_type=F32))
    kt = lax.dot_general(wkt_ref[...], xb, NT_DIMS, preferred_element_type=F32)
    for j in range(kt_ref.shape[1]):
        kt_ref[0, j] = kt[:, j * ML_BLOCK:(j + 1) * ML_BLOCK]
    gt_ref[0] = lax.dot_general(wgt_ref[...], xb, NT_DIMS, preferred_element_type=F32) + bg_ref[...]


def _ml_proj(x16, w, wkt, wgt, bg, bsz, seq, tm=512):
    t, d = x16.shape
    qk_w = ML_HEADS * ML_DQK
    per_seq = seq // tm
    n_gate = wgt.shape[0]
    row = lambda i: (i, 0)
    return pl.pallas_call(
        _ml_proj_kernel,
        out_shape=(jax.ShapeDtypeStruct((t, qk_w), BF16), jax.ShapeDtypeStruct((t, qk_w), BF16),
                   jax.ShapeDtypeStruct((t, D_MODEL), BF16), jax.ShapeDtypeStruct((t, D_MODEL), F32),
                   jax.ShapeDtypeStruct((bsz, seq // ML_BLOCK, qk_w, ML_BLOCK), F32),
                   jax.ShapeDtypeStruct((bsz, n_gate, seq), F32)),
        grid=(t // tm,),
        in_specs=[pl.BlockSpec((tm, d), row), _const_spec(w.shape), _const_spec(wkt.shape),
                  _const_spec(wgt.shape), _const_spec(bg.shape)],
        out_specs=(pl.BlockSpec((tm, qk_w), row), pl.BlockSpec((tm, qk_w), row),
                   pl.BlockSpec((tm, D_MODEL), row), pl.BlockSpec((tm, D_MODEL), row),
                   pl.BlockSpec((1, tm // ML_BLOCK, qk_w, ML_BLOCK), lambda i: (i // per_seq, i % per_seq, 0, 0)),
                   pl.BlockSpec((1, n_gate, tm), lambda i: (i // per_seq, 0, i % per_seq))),
        compiler_params=pltpu.CompilerParams(
            dimension_semantics=("parallel",),
            vmem_limit_bytes=_vmem_limit(2 * _nbytes((tm, d), BF16), _nbytes(w.shape, BF16), _nbytes(wkt.shape, BF16),
                                         2 * _nbytes((tm, 2 * qk_w + D_MODEL), BF16), 2 * _nbytes((tm, D_MODEL), F32),
                                         4 * _nbytes((qk_w, tm), F32), 2 * _nbytes((tm, D_MODEL), F32))),
        name="mlstm_proj",
    )(x16, w, wkt, wgt, bg)


def _log_sigmoid(x):
    return jnp.minimum(x, 0.0) - jnp.log1p(jnp.exp(-jnp.abs(x)))


def _exact_dot(x, m):
    hi = x.astype(BF16)
    r1 = x - hi.astype(F32)
    mid = r1.astype(BF16)
    lo = (r1 - mid.astype(F32)).astype(BF16)
    return (jnp.dot(hi, m, preferred_element_type=F32) + jnp.dot(mid, m, preferred_element_type=F32)
            + jnp.dot(lo, m, preferred_element_type=F32))


def _lane_cummax(x, lane_pos, reverse):
    shift = 1
    while shift < ML_CHUNK:
        if reverse:
            moved, ok = pltpu.roll(x, x.shape[1] - shift, axis=1), lane_pos < ML_CHUNK - shift
        else:
            moved, ok = pltpu.roll(x, shift, axis=1), lane_pos >= shift
        x = jnp.maximum(x, jnp.where(ok, moved, -jnp.inf))
        shift *= 2
    return x


def _mlstm_block(q, k, kt, v, a_row, ge_row, decay_row, m_row, g_col, b_col, c_state, n_state, vis, same, row_chunk,
                 reverse):
    per = ML_BLOCK // ML_CHUNK
    v_ext = jnp.concatenate([v, jnp.ones_like(v)], axis=1)
    wkt = jnp.exp(a_row - ge_row) * kt
    wkt4 = jnp.where(same, jnp.concatenate([wkt] * per, axis=0), 0.0)
    delta = jnp.dot(wkt4.astype(BF16), v_ext, preferred_element_type=F32)
    yield
    g_rep = jnp.broadcast_to(g_col, (ML_BLOCK, ML_DV))
    b_rep = jnp.broadcast_to(b_col, (ML_BLOCK, ML_DV))
    m_rep = jnp.broadcast_to(m_row[:, (per - 1) * ML_CHUNK:(per - 1) * ML_CHUNK + 1], g_rep.shape)
    for i in range(per - 2, -1, -1):
        m_rep = jnp.where(row_chunk == i, m_row[:, i * ML_CHUNK:i * ML_CHUNK + 1], m_rep)
    s_inter = jnp.exp(m_rep - g_rep)
    floor_rep = jnp.exp(-(b_rep + g_rep))
    w = jnp.where(vis, jnp.exp(a_row - jnp.concatenate([g_rep, g_rep], axis=1)), 0.0)
    qk = lax.dot_general(q, k, NT_DIMS, preferred_element_type=F32) * w
    yield
    intra = jnp.dot(qk.astype(BF16), v_ext, preferred_element_type=F32)
    starts = [None] * per
    state = jnp.concatenate([c_state, n_state], axis=1)
    for i in (range(per - 1, -1, -1) if reverse else range(per)):
        starts[i] = state
        state = decay_row[:, i * ML_CHUNK:i * ML_CHUNK + 1] * state + delta[i * ML_DQK:(i + 1) * ML_DQK]
    yield
    q4 = jnp.where(same, jnp.concatenate([q] * per, axis=1), jnp.zeros((), q.dtype))
    inter = jnp.dot(q4, jnp.concatenate(starts, axis=0).astype(BF16), preferred_element_type=F32)
    num = s_inter * inter[:, :ML_DV] + intra[:, :ML_DV]
    den = s_inter * inter[:, ML_DV:] + intra[:, ML_DV:]
    h = num / jnp.maximum(jnp.abs(den), floor_rep)
    yield h, state[:, :ML_DV], state[:, ML_DV:]


def _mlstm_kernel(q_ref, k_ref, kt_ref, v_ref, o_ref, g_ref, ng_ref, out_ref, hfw_ref, hbw_ref, row_ref, col_ref,
                  *, n_block):
    pair = pl.program_id(1)
    per = ML_BLOCK // ML_CHUNK
    t_idx = lax.broadcasted_iota(jnp.int32, (ML_BLOCK, ML_BLOCK), 0)
    s_idx = lax.broadcasted_iota(jnp.int32, (ML_BLOCK, ML_BLOCK), 1)
    same = (t_idx // ML_CHUNK) == (s_idx // ML_CHUNK)
    masks = (same & (s_idx <= t_idx), same & (s_idx >= t_idx))
    lane = lax.broadcasted_iota(jnp.int32, (1, ML_BLOCK), 1)
    lane_chunk, lane_pos = lane // ML_CHUNK, lane % ML_CHUNK
    blk_row = lax.broadcasted_iota(jnp.int32, (n_block, 1), 0)
    row_chunk = lax.broadcasted_iota(jnp.int32, (ML_BLOCK, ML_DV), 0) // ML_CHUNK

    per_query = []
    for direction in range(2):
        cum = jnp.where(masks[1 - direction], 1.0, 0.0).astype(BF16)
        for hh in range(2):
            chain = direction * 2 + hh
            head = pair * 2 + hh
            lf = _log_sigmoid(g_ref[0, (direction * 2 + 1) * ML_HEADS + head])
            b = _exact_dot(lf, cum)
            a = g_ref[0, direction * 2 * ML_HEADS + head] - b
            a_max = [jnp.max(jnp.where(lane_chunk == i, a, -jnp.inf), axis=1, keepdims=True) for i in range(per)]
            f_sum = [jnp.sum(jnp.where(lane_chunk == i, lf, 0.0), axis=1, keepdims=True) for i in range(per)]
            m = jnp.zeros((1, 1), F32)
            m_row = jnp.zeros(a.shape, F32)
            ge_row = jnp.zeros(a.shape, F32)
            n_chunk = n_block * per
            for c in (range(n_chunk - 1, -1, -1) if direction else range(n_chunk)):
                blk, i = divmod(c, per)
                g_end = jnp.maximum(m, a_max[i][blk:blk + 1])
                here = (blk_row == blk) & (lane_chunk == i)
                m_row = jnp.where(here, m, m_row)
                ge_row = jnp.where(here, g_end, ge_row)
                m = f_sum[i][blk:blk + 1] + g_end
            g_row = jnp.maximum(m_row, _lane_cummax(a, lane_pos, reverse=bool(direction)))
            row_ref[chain, 0] = a
            row_ref[chain, 1] = ge_row
            row_ref[chain, 2] = jnp.exp(m_row - ge_row)
            row_ref[chain, 3] = m_row
            per_query += [g_row, b]
    flat = [jnp.concatenate([x[blk:blk + 1] for blk in range(n_block)], axis=1) for x in per_query]
    col_ref[...] = jnp.concatenate(flat, axis=0).T

    def body(step, carry):
        chains = []
        for direction in range(2):
            blk = step if direction == 0 else n_block - 1 - step
            rows = pl.ds(pl.multiple_of(blk * ML_BLOCK, ML_BLOCK), ML_BLOCK)
            for hh in range(2):
                chain = direction * 2 + hh
                c_state, n_state = carry[chain]
                stages = _mlstm_block(
                    q_ref[rows, hh * ML_DQK:(hh + 1) * ML_DQK], k_ref[rows, hh * ML_DQK:(hh + 1) * ML_DQK],
                    kt_ref[0, blk, hh * ML_DQK:(hh + 1) * ML_DQK, :], v_ref[rows, hh * ML_DV:(hh + 1) * ML_DV],
                    *[row_ref[chain, j, pl.ds(blk, 1), :] for j in range(4)],
                    col_ref[rows, 2 * chain:2 * chain + 1], col_ref[rows, 2 * chain + 1:2 * chain + 2],
                    c_state, n_state, masks[direction], same, row_chunk, reverse=bool(direction))
                chains.append((stages, hfw_ref if direction == 0 else hbw_ref, rows, hh))
        for _ in range(3):
            for stages, _, _, _ in chains:
                next(stages)
        new_carry = []
        for stages, dst, rows, hh in chains:
            h, c_new, n_new = next(stages)
            dst[rows, hh * ML_DV:(hh + 1) * ML_DV] = h
            new_carry.append((c_new, n_new))
        return tuple(new_carry)

    init = tuple((jnp.zeros((ML_DQK, ML_DV), F32), jnp.zeros((ML_DQK, ML_DV), F32)) for _ in range(4))
    lax.fori_loop(0, n_block, body, init)

    for hh in range(2):
        cols = slice(hh * ML_DV, (hh + 1) * ML_DV)
        h = hfw_ref[:, cols] + hbw_ref[:, cols]
        ms = jnp.mean(h * h, -1, keepdims=True)
        hn = h * lax.rsqrt(ms + EPS) * ng_ref[:, cols]
        out_ref[:, cols] = (o_ref[:, cols] * hn).astype(BF16)


def _mlstm(q, k, kt, v, o, gates, norm_g, bsz, seq):
    n_pair = ML_HEADS // 2
    n_block = seq // ML_BLOCK
    pair_w = 2 * ML_DV
    return pl.pallas_call(
        functools.partial(_mlstm_kernel, n_block=n_block),
        out_shape=jax.ShapeDtypeStruct((bsz * seq, D_MODEL), BF16),
        grid=(bsz, n_pair),
        in_specs=[pl.BlockSpec((seq, 2 * ML_DQK), lambda b, p: (b, p)),
                  pl.BlockSpec((seq, 2 * ML_DQK), lambda b, p: (b, p)),
                  pl.BlockSpec((1, n_block, 2 * ML_DQK, ML_BLOCK), lambda b, p: (b, 0, p, 0)),
                  pl.BlockSpec((seq, pair_w), lambda b, p: (b, p)),
                  pl.BlockSpec((seq, pair_w), lambda b, p: (b, p)),
                  pl.BlockSpec((1,) + gates.shape[1:], lambda b, p: (b, 0, 0, 0)),
                  pl.BlockSpec((1, pair_w), lambda b, p: (0, p))],
        out_specs=pl.BlockSpec((seq, pair_w), lambda b, p: (b, p)),
        scratch_shapes=[pltpu.VMEM((seq, pair_w), F32), pltpu.VMEM((seq, pair_w), F32),
                        pltpu.VMEM((4, 4, n_block, ML_BLOCK), F32), pltpu.VMEM((seq, 8), F32)],
        compiler_params=pltpu.CompilerParams(
            dimension_semantics=("parallel", "parallel"),
            vmem_limit_bytes=_vmem_limit(4 * _nbytes((seq, 2 * ML_DQK), BF16), 2 * _nbytes((seq, 2 * ML_DQK), F32),
                                         4 * _nbytes((seq, pair_w), BF16), 2 * _nbytes((seq, pair_w), F32),
                                         2 * _nbytes(gates.shape[1:], F32), 2 * _nbytes((seq, pair_w), F32))),
        name="mlstm_scan",
    )(q, k, kt, v, o, gates, norm_g)


def _block_tail_kernel(mix_ref, x_ref, p_ref, wo_ref, g1_ref, b1_ref, w1_ref, w2_ref, wg_ref, wp_ref, g2_ref, b2_ref,
                       o32_ref, o16_ref, *, ff_chunk, parts):
    tr = x_ref.shape[0] // parts
    rows = [slice(i * tr, (i + 1) * tr) for i in range(parts)]

    def out_proj(r):
        return jnp.dot(mix_ref[r, :], wo_ref[...], preferred_element_type=F32) + DN_ALPHA * x_ref[r, :]

    def ffn_chunk(xb, c):
        sl = slice(c * ff_chunk, (c + 1) * ff_chunk)
        h = jnp.maximum(jnp.dot(xb, w1_ref[:, sl], preferred_element_type=F32), 0.0)
        return jnp.dot((h * h).astype(BF16), w2_ref[sl, :], preferred_element_type=F32)

    def finish(x1, acc, r):
        z = _layer_norm(DN_ALPHA * x1 + acc, g2_ref[...], b2_ref[...])
        o32_ref[r, :] = z
        o16_ref[r, :] = z.astype(BF16)

    y = out_proj(rows[0])
    done = None
    for i in range(parts):
        y_next = out_proj(rows[i + 1]) if i + 1 < parts else None
        x1 = _layer_norm(y, g1_ref[...], b1_ref[...])
        xb = x1.astype(BF16)
        gate = jax.nn.sigmoid(jnp.dot(xb, wg_ref[...], preferred_element_type=F32))
        acc = gate * jnp.dot(p_ref[rows[i], :].astype(BF16), wp_ref[...], preferred_element_type=F32)
        for c in range(w1_ref.shape[1] // ff_chunk):
            acc = acc + ffn_chunk(xb, c)
            if c == 0 and done is not None:
                finish(*done)
        done = (x1, acc, rows[i])
        y = y_next
    finish(*done)


def _block_tail(mix, x32, p, layer, wo, g1, b1, w1, w2, wg, wp, g2, b2, tm=512, ff_chunk=1024, parts=2):
    t, d = x32.shape
    row = lambda i: (i, 0)
    vec = _const_spec((1, d))
    return pl.pallas_call(
        functools.partial(_block_tail_kernel, ff_chunk=ff_chunk, parts=parts),
        out_shape=(jax.ShapeDtypeStruct((t, d), F32), jax.ShapeDtypeStruct((t, d), BF16)),
        grid=(t // tm,),
        in_specs=[pl.BlockSpec((tm, d), row), pl.BlockSpec((tm, d), row),
                  pl.BlockSpec((None, tm, p.shape[2]), lambda i: (layer, i, 0)),
                  _const_spec(wo.shape), vec, vec, _const_spec(w1.shape), _const_spec(w2.shape),
                  _const_spec(wg.shape), _const_spec(wp.shape), vec, vec],
        out_specs=(pl.BlockSpec((tm, d), row), pl.BlockSpec((tm, d), row)),
        compiler_params=pltpu.CompilerParams(
            dimension_semantics=("parallel",),
            vmem_limit_bytes=_vmem_limit(4 * _nbytes((tm, d), BF16), 4 * _nbytes((tm, d), F32),
                                         2 * _nbytes((tm, p.shape[2]), F32), _nbytes(wo.shape, BF16),
                                         _nbytes(w1.shape, BF16), _nbytes(w2.shape, BF16), _nbytes(wg.shape, BF16),
                                         _nbytes(wp.shape, BF16), 2 * _nbytes((tm, ff_chunk), F32),
                                         4 * _nbytes((tm, d), F32))),
        name="block_tail",
    )(mix, x32, p, wo, g1, b1, w1, w2, wg, wp, g2, b2)


def kernel(x, p, na_w_qkv, na_rpb, na_w_o, gq_w_qkv, gq_q_norm, gq_k_norm, gq_w_o, ml_w_in, ml_b_gates, ml_norm_g,
           ml_w_o, ln1_g, ln1_b, w_ff1, w_ff2, ln2_g, ln2_b, w_ple_gate, w_ple_proj):
    bsz, seq, d = x.shape
    assert d == D_MODEL and seq % (NA_KH * GRID_W) == 0 and p.shape == (DEPTH, bsz, seq, D_PLE)
    t = bsz * seq
    x32 = x.reshape(t, d)
    p_tok = p.reshape(DEPTH, t, D_PLE)
    x16 = None
    qk_w = ML_HEADS * ML_DQK
    for i in range(DEPTH):
        kind, j = i % 3, i // 3
        if kind == 0:
            qkv = _proj(x32 if x16 is None else x16, na_w_qkv[j].astype(BF16), n_chunk=D_MODEL,
                        first_scale=HEAD_DIM ** -0.5 * LOG2E)
            mix = _na_attention(qkv, _na_bias_table(na_rpb[j]), bsz, seq)
            w_o = na_w_o[j]
        elif kind == 1:
            cos4, sin4 = _rope_tables(seq)
            qkv = _gqa_proj(x16, _gqa_weight_layout(gq_w_qkv[j]).astype(BF16), cos4, sin4,
                            _gqa_gain_layout(gq_q_norm[j]), _gqa_gain_layout(gq_k_norm[j]), _group_sum_matrix(), seq)
            mix = _gqa_attention(qkv, bsz, seq)
            w_o = gq_w_o[j]
        else:
            w_in = ml_w_in[j]
            n_main = 2 * qk_w + 2 * D_MODEL
            wkt = w_in[:, qk_w:2 * qk_w].T.astype(BF16)
            wgt = w_in[:, n_main:].T.astype(BF16)
            q, k, v, o, kt, gt = _ml_proj(x16, w_in[:, :n_main].astype(BF16), wkt, wgt, ml_b_gates[j][:, None], bsz, seq)
            gates = gt.reshape(bsz, gt.shape[1], seq // ML_BLOCK, ML_BLOCK)
            mix = _mlstm(q, k, kt, v, o, gates, ml_norm_g[j][None, :], bsz, seq)
            w_o = ml_w_o[j]
        x32, x16 = _block_tail(mix, x32, p_tok, i, w_o.astype(BF16), ln1_g[i][None, :], ln1_b[i][None, :],
                               w_ff1[i].astype(BF16), w_ff2[i].astype(BF16), w_ple_gate[i].astype(BF16),
                               w_ple_proj[i].astype(BF16), ln2_g[i][None, :], ln2_b[i][None, :])
    return x32.reshape(bsz, seq, d)
```

```python
import functools

import jax
import jax.numpy as jnp
import numpy as np
from jax import lax
from jax.experimental import pallas as pl
from jax.experimental.pallas import tpu as pltpu

F32 = jnp.float32
BF16 = jnp.bfloat16

D_MODEL = 1024
DEPTH = 4
GRID_W = 64
HEAD_DIM = 64
D_FF = 4 * D_MODEL
D_PLE = 256
NA_HEADS = 16
NA_KH = 8
NA_KW = 16
GQA_KV_HEADS = 4
GQA_GROUP = 4
ROPE_THETA = 10000.0
ML_HEADS = 8
ML_DV = 128
ML_DQK = 64
ML_CHUNK = 64
ML_BLOCK = 256
DN_ALPHA = (2 * DEPTH) ** 0.25
EPS = 1e-6
LOG2E = 1.4426950408889634

V7X_VMEM_BYTES = 64 * 1024 * 1024
V7X_LANES = 128
SLAB = 2 * V7X_LANES

NT_DIMS = (((1,), (1,)), ((), ()))


def _vmem_limit(*byte_counts):
    est = int(sum(byte_counts) * 1.5) + (4 << 20)
    return min(est, V7X_VMEM_BYTES - (6 << 20))


def _nbytes(shape, dtype):
    return int(np.prod(shape)) * jnp.dtype(dtype).itemsize


def _const_spec(shape):
    nd = len(shape)
    return pl.BlockSpec(shape, lambda *_: (0,) * nd, pipeline_mode=pl.Buffered(1))


def _layer_norm(y, g, b):
    mu = jnp.mean(y, -1, keepdims=True)
    yc = y - mu
    var = jnp.mean(yc * yc, -1, keepdims=True)
    return yc * lax.rsqrt(var + EPS) * g + b


def _proj_kernel(x_ref, w_ref, o_ref, *, n_chunk, first_scale):
    xb = x_ref[...].astype(BF16)
    for j in range(o_ref.shape[1] // n_chunk):
        sl = slice(j * n_chunk, (j + 1) * n_chunk)
        y = jnp.dot(xb, w_ref[:, sl], preferred_element_type=F32)
        if j == 0 and first_scale != 1.0:
            y = y * first_scale
        o_ref[:, sl] = y.astype(BF16)


def _proj(x, w, tm=512, n_chunk=1024, first_scale=1.0):
    t, d = x.shape
    n = w.shape[1]
    return pl.pallas_call(
        functools.partial(_proj_kernel, n_chunk=n_chunk, first_scale=first_scale),
        out_shape=jax.ShapeDtypeStruct((t, n), BF16),
        grid=(t // tm,),
        in_specs=[pl.BlockSpec((tm, d), lambda i: (i, 0)), _const_spec((d, n))],
        out_specs=pl.BlockSpec((tm, n), lambda i: (i, 0)),
        compiler_params=pltpu.CompilerParams(
            dimension_semantics=("parallel",),
            vmem_limit_bytes=_vmem_limit(2 * _nbytes((tm, d), x.dtype), _nbytes((d, n), BF16),
                                         2 * _nbytes((tm, n), BF16), _nbytes((tm, n_chunk), F32))),
        name="proj_plain",
    )(x, w)


def _na_kernel(q_ref, k_ref, v_ref, bias_ref, o_ref, s_ref, p_ref, *, rows, group):
    lane_head = lax.broadcasted_iota(jnp.int32, (1, SLAB), 1) // HEAD_DIM
    win = NA_KH * GRID_W

    def window(r):
        r = jnp.clip(r, 0, rows - 1)
        r0 = jnp.clip(r - NA_KH // 2, 0, rows - NA_KH)
        return pl.multiple_of(r * GRID_W, GRID_W), pl.multiple_of(r0 * GRID_W, GRID_W), r - r0

    def scores(r, slot):
        q0, k0, delta = window(r)
        q = q_ref[pl.ds(q0, GRID_W), :]
        qs = jnp.concatenate([jnp.where(lane_head == h, q, jnp.zeros_like(q)) for h in range(4)], axis=0)
        s = lax.dot_general(qs, k_ref[pl.ds(k0, win), :], NT_DIMS, preferred_element_type=F32)
        s_ref[slot] = s + bias_ref[0, delta]

    def softmax(slot):
        s = s_ref[slot]
        e = jnp.exp2(s - jnp.max(s, -1, keepdims=True))
        p_ref[slot] = (e * (1.0 / jnp.sum(e, -1, keepdims=True))).astype(BF16)

    def weighted_values(r, slot):
        q0, k0, _ = window(r)
        pv = jnp.dot(p_ref[slot], v_ref[pl.ds(k0, win), :], preferred_element_type=F32)
        acc = jnp.zeros((GRID_W, SLAB), F32)
        for h in range(4):
            acc = jnp.where(lane_head == h, pv[h * GRID_W:(h + 1) * GRID_W], acc)
        o_ref[pl.ds(q0, GRID_W), :] = acc.astype(BF16)

    def step(g, bank):
        for t in range(group):
            weighted_values((g - 1) * group + t, (1 - bank) * group + t)
        for t in range(group):
            scores((g + 1) * group + t, (1 - bank) * group + t)
        for t in range(group):
            softmax(bank * group + t)

    for t in range(group):
        scores(t, t)
        p_ref[group + t] = jnp.zeros(p_ref.shape[1:], BF16)

    def body(j, carry):
        step(2 * j, 0)
        step(2 * j + 1, 1)
        return carry

    n_step = rows // group
    lax.fori_loop(0, n_step // 2, body, 0)
    for t in range(group):
        weighted_values(rows - group + t, ((n_step - 1) % 2) * group + t)


def _na_bias_table(rpb):
    col = np.arange(GRID_W)
    c0 = np.clip(col - NA_KW // 2, 0, GRID_W - NA_KW)
    col_in = (col[None, :] >= c0[:, None]) & (col[None, :] < c0[:, None] + NA_KW)
    dc = np.clip(col[None, :] - col[:, None], 1 - NA_KW, NA_KW - 1) + NA_KW - 1
    rpb = rpb.astype(F32)
    by_col = jnp.zeros(rpb.shape[:2] + dc.shape, F32)
    for c in range(2 * NA_KW - 1):
        by_col = jnp.where(dc[None, None] == c, rpb[:, :, c][:, :, None, None], by_col)
    by_col = jnp.where(col_in[None, None], by_col * LOG2E, -jnp.inf)
    per_delta = [by_col[:, NA_KH - 1 - dl:2 * NA_KH - 1 - dl].transpose(0, 2, 1, 3)
                 .reshape(NA_HEADS, GRID_W, NA_KH * GRID_W) for dl in range(NA_KH)]
    b = jnp.stack(per_delta, axis=1).reshape(NA_HEADS // 4, 4, NA_KH, GRID_W, NA_KH * GRID_W)
    return b.transpose(0, 2, 1, 3, 4).reshape(NA_HEADS // 4, NA_KH, 4 * GRID_W, NA_KH * GRID_W)


def _na_attention(qkv, bias, bsz, seq, group=2):
    n_slab = D_MODEL // SLAB
    rows = seq // GRID_W
    assert rows % (2 * group) == 0
    blk = (seq, SLAB)
    tile = (4 * GRID_W, NA_KH * GRID_W)
    return pl.pallas_call(
        functools.partial(_na_kernel, rows=rows, group=group),
        out_shape=jax.ShapeDtypeStruct((bsz * seq, D_MODEL), BF16),
        grid=(n_slab, bsz),
        in_specs=[pl.BlockSpec(blk, lambda s, b: (b, s)),
                  pl.BlockSpec(blk, lambda s, b: (b, n_slab + s)),
                  pl.BlockSpec(blk, lambda s, b: (b, 2 * n_slab + s)),
                  pl.BlockSpec((1,) + bias.shape[1:], lambda s, b: (s, 0, 0, 0))],
        out_specs=pl.BlockSpec(blk, lambda s, b: (b, s)),
        scratch_shapes=[pltpu.VMEM((2 * group,) + tile, F32), pltpu.VMEM((2 * group,) + tile, BF16)],
        compiler_params=pltpu.CompilerParams(
            dimension_semantics=("parallel", "parallel"),
            vmem_limit_bytes=_vmem_limit(8 * _nbytes(blk, BF16), 2 * _nbytes(bias.shape[1:], F32),
                                         (3 * group + 4) * _nbytes(tile, F32))),
        name="na_attention",
    )(qkv, qkv, qkv, bias)


def _gqa_proj_kernel(x_ref, w_ref, wvt_ref, cos_ref, sin_ref, gq_ref, gk_ref, ones_ref, o_ref, vt_ref):
    xb = x_ref[...]
    cos = cos_ref[...]
    sin = sin_ref[...]
    ones = ones_ref[...]
    n_norm = 2 * D_MODEL // (2 * SLAB)

    def project(pair):
        return jnp.dot(xb, w_ref[:, pair * 2 * SLAB:(pair + 1) * 2 * SLAB], preferred_element_type=F32)

    z_next = project(0)
    for pair in range(n_norm):
        z = z_next
        if pair + 1 < n_norm:
            z_next = project(pair + 1)
        else:
            vt_ref[...] = lax.dot_general(wvt_ref[...], xb, NT_DIMS, preferred_element_type=F32).astype(BF16)
        halves = [(z[:, j * SLAB:j * SLAB + V7X_LANES], z[:, j * SLAB + V7X_LANES:(j + 1) * SLAB]) for j in range(2)]
        ss = jnp.concatenate([a * a + b * b for a, b in halves], axis=1)
        hi = ss.astype(BF16)
        lo = (ss - hi.astype(F32)).astype(BF16)
        ms = (jnp.dot(hi, ones, preferred_element_type=F32)
              + jnp.dot(lo, ones, preferred_element_type=F32)) * (1.0 / HEAD_DIM)
        rs = lax.rsqrt(ms + EPS)
        is_q = pair < n_norm // 2
        g_ref = gq_ref if is_q else gk_ref
        for j, (a, b) in enumerate(halves):
            r = rs[:, j * V7X_LANES:(j + 1) * V7X_LANES]
            an = a * r * g_ref[:, :V7X_LANES]
            bn = b * r * g_ref[:, V7X_LANES:]
            oa = an * cos - bn * sin
            ob = an * sin + bn * cos
            if is_q:
                oa = oa * (HEAD_DIM ** -0.5 * LOG2E)
                ob = ob * (HEAD_DIM ** -0.5 * LOG2E)
            c0 = (pair * 2 + j) * SLAB
            o_ref[:, c0:c0 + V7X_LANES] = oa.astype(BF16)
            o_ref[:, c0 + V7X_LANES:c0 + SLAB] = ob.astype(BF16)


def _gqa_weight_layout(w):
    d = w.shape[0]
    kvd = GQA_KV_HEADS * HEAD_DIM
    half = HEAD_DIM // 2
    wq = w[:, :D_MODEL].reshape(d, GQA_KV_HEADS, GQA_GROUP, half, 2).transpose(0, 1, 4, 2, 3)
    wk = w[:, D_MODEL:D_MODEL + kvd].reshape(d, GQA_KV_HEADS, half, 2).transpose(0, 1, 3, 2)
    wk = jnp.broadcast_to(wk[:, :, :, None, :], (d, GQA_KV_HEADS, 2, GQA_GROUP, half))
    return jnp.concatenate([wq.reshape(d, D_MODEL), wk.reshape(d, D_MODEL)], axis=1), w[:, D_MODEL + kvd:].T


def _gqa_gain_layout(g):
    half = HEAD_DIM // 2
    return jnp.broadcast_to(g.reshape(half, 2).T[:, None, :], (2, GQA_GROUP, half)).reshape(1, SLAB)


def _group_sum_matrix():
    blk = np.arange(SLAB) // (HEAD_DIM // 2)
    return jnp.asarray(blk[:, None] == blk[None, :], BF16)


def _rope_tables(seq):
    t = jnp.arange(seq)
    row = (t // GRID_W).astype(F32)
    col = (t % GRID_W).astype(F32)
    n_pairs = HEAD_DIM // 4
    inv = ROPE_THETA ** (-jnp.arange(n_pairs, dtype=F32) / n_pairs)
    ang = jnp.concatenate([row[:, None] * inv, col[:, None] * inv], -1)
    return jnp.tile(jnp.cos(ang), (1, GQA_GROUP)), jnp.tile(jnp.sin(ang), (1, GQA_GROUP))


def _gqa_proj(x16, w, wvt, cos4, sin4, gq, gk, ones, bsz, seq, tm=512):
    t, d = x16.shape
    n = w.shape[1]
    per_seq = seq // tm
    return pl.pallas_call(
        _gqa_proj_kernel,
        out_shape=(jax.ShapeDtypeStruct((t, n), BF16), jax.ShapeDtypeStruct((bsz, wvt.shape[0], seq), BF16)),
        grid=(t // tm,),
        in_specs=[pl.BlockSpec((tm, d), lambda i: (i, 0)), _const_spec((d, n)), _const_spec(wvt.shape),
                  pl.BlockSpec((tm, V7X_LANES), lambda i: (i % per_seq, 0)),
                  pl.BlockSpec((tm, V7X_LANES), lambda i: (i % per_seq, 0)),
                  _const_spec((1, SLAB)), _const_spec((1, SLAB)), _const_spec((SLAB, SLAB))],
        out_specs=(pl.BlockSpec((tm, n), lambda i: (i, 0)),
                   pl.BlockSpec((None, wvt.shape[0], tm), lambda i: (i // per_seq, 0, i % per_seq))),
        compiler_params=pltpu.CompilerParams(
            dimension_semantics=("parallel",),
            vmem_limit_bytes=_vmem_limit(2 * _nbytes((tm, d), BF16), _nbytes((d, n), BF16), _nbytes(wvt.shape, BF16),
                                         2 * _nbytes((tm, n), BF16), 6 * _nbytes((tm, 2 * SLAB), F32))),
        name="gqa_proj",
    )(x16, w, wvt, cos4, sin4, gq, gk, ones)


def _gqa_attn_kernel(q_ref, k_ref, vt_ref, o_ref, *, q_block):
    lane = lax.broadcasted_iota(jnp.int32, (1, SLAB), 1)
    q = q_ref[...]
    q_head = (lane % V7X_LANES) // (HEAD_DIM // 2)

    def scores(h):
        qm = jnp.where(q_head == h, q, jnp.zeros_like(q))
        return lax.dot_general(k_ref[...], qm, NT_DIMS, preferred_element_type=F32)

    def softmax(s):
        e = jnp.exp2(s - jnp.max(s, 0, keepdims=True))
        return e.astype(BF16), 1.0 / jnp.sum(e, 0, keepdims=True)

    def weighted_values(e, inv_l):
        return jnp.dot(vt_ref[...], e, preferred_element_type=F32) * inv_l

    n_qb = q_ref.shape[0] // q_block
    outs = [[None] * GQA_GROUP for _ in range(n_qb)]
    s_next = scores(0)
    pending = []
    for h in range(GQA_GROUP):
        s = s_next
        if h + 1 < GQA_GROUP:
            s_next = scores(h + 1)
        current = []
        for qb in range(n_qb):
            current.append(softmax(s[:, qb * q_block:(qb + 1) * q_block]))
            if pending:
                outs[qb][h - 1] = weighted_values(*pending[qb])
        pending = current
    for qb in range(n_qb):
        outs[qb][GQA_GROUP - 1] = weighted_values(*pending[qb])
        o_ref[qb * q_block:(qb + 1) * q_block, :] = jnp.concatenate(outs[qb], axis=0).T.astype(BF16)


def _gqa_attention(qk, vt, bsz, seq, tq=1024, q_block=256):
    n_slab = D_MODEL // SLAB
    nq = seq // tq
    return pl.pallas_call(
        functools.partial(_gqa_attn_kernel, q_block=q_block),
        out_shape=jax.ShapeDtypeStruct((bsz * seq, D_MODEL), BF16),
        grid=(bsz, n_slab, nq),
        in_specs=[pl.BlockSpec((tq, SLAB), lambda b, g, i: (b * nq + i, g)),
                  pl.BlockSpec((seq, SLAB), lambda b, g, i: (b, n_slab + g)),
                  pl.BlockSpec((None, HEAD_DIM, seq), lambda b, g, i: (b, g, 0))],
        out_specs=pl.BlockSpec((tq, SLAB), lambda b, g, i: (b * nq + i, g)),
        compiler_params=pltpu.CompilerParams(
            dimension_semantics=("parallel", "parallel", "parallel"),
            vmem_limit_bytes=_vmem_limit(4 * _nbytes((tq, SLAB), BF16), 4 * _nbytes((seq, SLAB), BF16),
                                         2 * _nbytes((HEAD_DIM, seq), BF16), 3 * _nbytes((tq, seq), F32))),
        name="gqa_attention",
    )(qk, qk, vt)


def _ml_proj_kernel(x_ref, w_ref, wkt_ref, wgt_ref, bg_ref, q_ref, k_ref, v_ref, o_ref, kt_ref, gt_ref):
    xb = x_ref[...]
    qk_w = ML_HEADS * ML_DQK
    q_ref[...] = (jnp.dot(xb, w_ref[:, :qk_w], preferred_element_type=F32) * (ML_DQK ** -0.5)).astype(BF16)
    k_ref[...] = jnp.dot(xb, w_ref[:, qk_w:2 * qk_w], preferred_element_type=F32).astype(BF16)
    v_ref[...] = jnp.dot(xb, w_ref[:, 2 * qk_w:2 * qk_w + D_MODEL], preferred_element_type=F32).astype(BF16)
    o_ref[...] = jax.nn.sigmoid(jnp.dot(xb, w_ref[:, 2 * qk_w + D_MODEL:], preferred_element_type=F32))
    kt = lax.dot_general(wkt_ref[...], xb, NT_DIMS, preferred_element_type=F32)
    for j in range(kt_ref.shape[1]):
        kt_ref[0, j] = kt[:, j * ML_BLOCK:(j + 1) * ML_BLOCK]
    gt_ref[0] = lax.dot_general(wgt_ref[...], xb, NT_DIMS, preferred_element_type=F32) + bg_ref[...]


def _ml_proj(x16, w, wkt, wgt, bg, bsz, seq, tm=512):
    t, d = x16.shape
    qk_w = ML_HEADS * ML_DQK
    per_seq = seq // tm
    n_gate = wgt.shape[0]
    row = lambda i: (i, 0)
    return pl.pallas_call(
        _ml_proj_kernel,
        out_shape=(jax.ShapeDtypeStruct((t, qk_w), BF16), jax.ShapeDtypeStruct((t, qk_w), BF16),
                   jax.ShapeDtypeStruct((t, D_MODEL), BF16), jax.ShapeDtypeStruct((t, D_MODEL), F32),
                   jax.ShapeDtypeStruct((bsz, seq // ML_BLOCK, qk_w, ML_BLOCK), F32),
                   jax.ShapeDtypeStruct((bsz, n_gate, seq), F32)),
        grid=(t // tm,),
        in_specs=[pl.BlockSpec((tm, d), row), _const_spec(w.shape), _const_spec(wkt.shape),
                  _const_spec(wgt.shape), _const_spec(bg.shape)],
        out_specs=(pl.BlockSpec((tm, qk_w), row), pl.BlockSpec((tm, qk_w), row),
                   pl.BlockSpec((tm, D_MODEL), row), pl.BlockSpec((tm, D_MODEL), row),
                   pl.BlockSpec((1, tm // ML_BLOCK, qk_w, ML_BLOCK), lambda i: (i // per_seq, i % per_seq, 0, 0)),
                   pl.BlockSpec((1, n_gate, tm), lambda i: (i // per_seq, 0, i % per_seq))),
        compiler_params=pltpu.CompilerParams(
            dimension_semantics=("parallel",),
            vmem_limit_bytes=_vmem_limit(2 * _nbytes((tm, d), BF16), _nbytes(w.shape, BF16), _nbytes(wkt.shape, BF16),
                                         2 * _nbytes((tm, 2 * qk_w + D_MODEL), BF16), 2 * _nbytes((tm, D_MODEL), F32),
                                         4 * _nbytes((qk_w, tm), F32), 2 * _nbytes((tm, D_MODEL), F32))),
        name="mlstm_proj",
    )(x16, w, wkt, wgt, bg)


def _log_sigmoid(x):
    return jnp.minimum(x, 0.0) - jnp.log1p(jnp.exp(-jnp.abs(x)))


def _exact_dot(x, m):
    hi = x.astype(BF16)
    r1 = x - hi.astype(F32)
    mid = r1.astype(BF16)
    lo = (r1 - mid.astype(F32)).astype(BF16)
    return (jnp.dot(hi, m, preferred_element_type=F32) + jnp.dot(mid, m, preferred_element_type=F32)
            + jnp.dot(lo, m, preferred_element_type=F32))


def _lane_cummax(x, lane_pos, reverse):
    shift = 1
    while shift < ML_CHUNK:
        if reverse:
            moved, ok = pltpu.roll(x, x.shape[1] - shift, axis=1), lane_pos < ML_CHUNK - shift
        else:
            moved, ok = pltpu.roll(x, shift, axis=1), lane_pos >= shift
        x = jnp.maximum(x, jnp.where(ok, moved, -jnp.inf))
        shift *= 2
    return x


def _mlstm_block(q, k, kt, v, a_row, ge_row, decay_row, m_row, g_col, b_col, c_state, n_state, vis, same, row_chunk,
                 reverse):
    per = ML_BLOCK // ML_CHUNK
    v_ext = jnp.concatenate([v, jnp.ones_like(v)], axis=1)
    wkt = jnp.exp(a_row - ge_row) * kt
    wkt4 = jnp.where(same, jnp.concatenate([wkt] * per, axis=0), 0.0)
    delta = jnp.dot(wkt4.astype(BF16), v_ext, preferred_element_type=F32)
    yield
    g_rep = jnp.broadcast_to(g_col, (ML_BLOCK, ML_DV))
    b_rep = jnp.broadcast_to(b_col, (ML_BLOCK, ML_DV))
    m_rep = jnp.broadcast_to(m_row[:, (per - 1) * ML_CHUNK:(per - 1) * ML_CHUNK + 1], g_rep.shape)
    for i in range(per - 2, -1, -1):
        m_rep = jnp.where(row_chunk == i, m_row[:, i * ML_CHUNK:i * ML_CHUNK + 1], m_rep)
    s_inter = jnp.exp(m_rep - g_rep)
    floor_rep = jnp.exp(-(b_rep + g_rep))
    w = jnp.where(vis, jnp.exp(a_row - jnp.concatenate([g_rep, g_rep], axis=1)), 0.0)
    qk = lax.dot_general(q, k, NT_DIMS, preferred_element_type=F32) * w
    yield
    intra = jnp.dot(qk.astype(BF16), v_ext, preferred_element_type=F32)
    starts = [None] * per
    state = jnp.concatenate([c_state, n_state], axis=1)
    for i in (range(per - 1, -1, -1) if reverse else range(per)):
        starts[i] = state
        state = decay_row[:, i * ML_CHUNK:i * ML_CHUNK + 1] * state + delta[i * ML_DQK:(i + 1) * ML_DQK]
    yield
    q4 = jnp.where(same, jnp.concatenate([q] * per, axis=1), jnp.zeros((), q.dtype))
    inter = jnp.dot(q4, jnp.concatenate(starts, axis=0).astype(BF16), preferred_element_type=F32)
    num = s_inter * inter[:, :ML_DV] + intra[:, :ML_DV]
    den = s_inter * inter[:, ML_DV:] + intra[:, ML_DV:]
    h = num / jnp.maximum(jnp.abs(den), floor_rep)
    yield h, state[:, :ML_DV], state[:, ML_DV:]


def _mlstm_kernel(q_ref, k_ref, kt_ref, v_ref, o_ref, g_ref, ng_ref, out_ref, hfw_ref, hbw_ref, row_ref, col_ref,
                  *, n_block):
    pair = pl.program_id(1)
    per = ML_BLOCK // ML_CHUNK
    t_idx = lax.broadcasted_iota(jnp.int32, (ML_BLOCK, ML_BLOCK), 0)
    s_idx = lax.broadcasted_iota(jnp.int32, (ML_BLOCK, ML_BLOCK), 1)
    same = (t_idx // ML_CHUNK) == (s_idx // ML_CHUNK)
    masks = (same & (s_idx <= t_idx), same & (s_idx >= t_idx))
    lane = lax.broadcasted_iota(jnp.int32, (1, ML_BLOCK), 1)
    lane_chunk, lane_pos = lane // ML_CHUNK, lane % ML_CHUNK
    blk_row = lax.broadcasted_iota(jnp.int32, (n_block, 1), 0)
    row_chunk = lax.broadcasted_iota(jnp.int32, (ML_BLOCK, ML_DV), 0) // ML_CHUNK

    per_query = []
    for direction in range(2):
        cum = jnp.where(masks[1 - direction], 1.0, 0.0).astype(BF16)
        for hh in range(2):
            chain = direction * 2 + hh
            head = pair * 2 + hh
            lf = _log_sigmoid(g_ref[0, (direction * 2 + 1) * ML_HEADS + head])
            b = _exact_dot(lf, cum)
            a = g_ref[0, direction * 2 * ML_HEADS + head] - b
            a_max = [jnp.max(jnp.where(lane_chunk == i, a, -jnp.inf), axis=1, keepdims=True) for i in range(per)]
            f_sum = [jnp.sum(jnp.where(lane_chunk == i, lf, 0.0), axis=1, keepdims=True) for i in range(per)]
            m = jnp.zeros((1, 1), F32)
            m_row = jnp.zeros(a.shape, F32)
            ge_row = jnp.zeros(a.shape, F32)
            n_chunk = n_block * per
            for c in (range(n_chunk - 1, -1, -1) if direction else range(n_chunk)):
                blk, i = divmod(c, per)
                g_end = jnp.maximum(m, a_max[i][blk:blk + 1])
                here = (blk_row == blk) & (lane_chunk == i)
                m_row = jnp.where(here, m, m_row)
                ge_row = jnp.where(here, g_end, ge_row)
                m = f_sum[i][blk:blk + 1] + g_end
            g_row = jnp.maximum(m_row, _lane_cummax(a, lane_pos, reverse=bool(direction)))
            row_ref[chain, 0] = a
            row_ref[chain, 1] = ge_row
            row_ref[chain, 2] = jnp.exp(m_row - ge_row)
            row_ref[chain, 3] = m_row
            per_query += [g_row, b]
    flat = [jnp.concatenate([x[blk:blk + 1] for blk in range(n_block)], axis=1) for x in per_query]
    col_ref[...] = jnp.concatenate(flat, axis=0).T

    def body(step, carry):
        chains = []
        for direction in range(2):
            blk = step if direction == 0 else n_block - 1 - step
            rows = pl.ds(pl.multiple_of(blk * ML_BLOCK, ML_BLOCK), ML_BLOCK)
            for hh in range(2):
                chain = direction * 2 + hh
                c_state, n_state = carry[chain]
                stages = _mlstm_block(
                    q_ref[rows, hh * ML_DQK:(hh + 1) * ML_DQK], k_ref[rows, hh * ML_DQK:(hh + 1) * ML_DQK],
                    kt_ref[0, blk, hh * ML_DQK:(hh + 1) * ML_DQK, :], v_ref[rows, hh * ML_DV:(hh + 1) * ML_DV],
                    *[row_ref[chain, j, pl.ds(blk, 1), :] for j in range(4)],
                    col_ref[rows, 2 * chain:2 * chain + 1], col_ref[rows, 2 * chain + 1:2 * chain + 2],
                    c_state, n_state, masks[direction], same, row_chunk, reverse=bool(direction))
                chains.append((stages, hfw_ref if direction == 0 else hbw_ref, rows, hh))
        for _ in range(3):
            for stages, _, _, _ in chains:
                next(stages)
        new_carry = []
        for stages, dst, rows, hh in chains:
            h, c_new, n_new = next(stages)
            dst[rows, hh * ML_DV:(hh + 1) * ML_DV] = h
            new_carry.append((c_new, n_new))
        return tuple(new_carry)

    init = tuple((jnp.zeros((ML_DQK, ML_DV), F32), jnp.zeros((ML_DQK, ML_DV), F32)) for _ in range(4))
    lax.fori_loop(0, n_block, body, init)

    for hh in range(2):
        cols = slice(hh * ML_DV, (hh + 1) * ML_DV)
        h = hfw_ref[:, cols] + hbw_ref[:, cols]
        ms = jnp.mean(h * h, -1, keepdims=True)
        hn = h * lax.rsqrt(ms + EPS) * ng_ref[:, cols]
        out_ref[:, cols] = (o_ref[:, cols] * hn).astype(BF16)


def _mlstm(q, k, kt, v, o, gates, norm_g, bsz, seq):
    n_pair = ML_HEADS // 2
    n_block = seq // ML_BLOCK
    pair_w = 2 * ML_DV
    return pl.pallas_call(
        functools.partial(_mlstm_kernel, n_block=n_block),
        out_shape=jax.ShapeDtypeStruct((bsz * seq, D_MODEL), BF16),
        grid=(bsz, n_pair),
        in_specs=[pl.BlockSpec((seq, 2 * ML_DQK), lambda b, p: (b, p)),
                  pl.BlockSpec((seq, 2 * ML_DQK), lambda b, p: (b, p)),
                  pl.BlockSpec((1, n_block, 2 * ML_DQK, ML_BLOCK), lambda b, p: (b, 0, p, 0)),
                  pl.BlockSpec((seq, pair_w), lambda b, p: (b, p)),
                  pl.BlockSpec((seq, pair_w), lambda b, p: (b, p)),
                  pl.BlockSpec((1,) + gates.shape[1:], lambda b, p: (b, 0, 0, 0)),
                  pl.BlockSpec((1, pair_w), lambda b, p: (0, p))],
        out_specs=pl.BlockSpec((seq, pair_w), lambda b, p: (b, p)),
        scratch_shapes=[pltpu.VMEM((seq, pair_w), F32), pltpu.VMEM((seq, pair_w), F32),
                        pltpu.VMEM((4, 4, n_block, ML_BLOCK), F32), pltpu.VMEM((seq, 8), F32)],
        compiler_params=pltpu.CompilerParams(
            dimension_semantics=("parallel", "parallel"),
            vmem_limit_bytes=_vmem_limit(4 * _nbytes((seq, 2 * ML_DQK), BF16), 2 * _nbytes((seq, 2 * ML_DQK), F32),
                                         4 * _nbytes((seq, pair_w), BF16), 2 * _nbytes((seq, pair_w), F32),
                                         2 * _nbytes(gates.shape[1:], F32), 2 * _nbytes((seq, pair_w), F32))),
        name="mlstm_scan",
    )(q, k, kt, v, o, gates, norm_g)


def _block_tail_kernel(mix_ref, x_ref, p_ref, wo_ref, g1_ref, b1_ref, w1_ref, w2_ref, wg_ref, wp_ref, g2_ref, b2_ref,
                       o32_ref, o16_ref, *, ff_chunk, parts):
    tr = x_ref.shape[0] // parts
    rows = [slice(i * tr, (i + 1) * tr) for i in range(parts)]

    def out_proj(r):
        return jnp.dot(mix_ref[r, :], wo_ref[...], preferred_element_type=F32) + DN_ALPHA * x_ref[r, :]

    def ffn_chunk(xb, c):
        sl = slice(c * ff_chunk, (c + 1) * ff_chunk)
        h = jnp.maximum(jnp.dot(xb, w1_ref[:, sl], preferred_element_type=F32), 0.0)
        return jnp.dot((h * h).astype(BF16), w2_ref[sl, :], preferred_element_type=F32)

    def finish(x1, acc, r):
        z = _layer_norm(DN_ALPHA * x1 + acc, g2_ref[...], b2_ref[...])
        o32_ref[r, :] = z
        o16_ref[r, :] = z.astype(BF16)

    y = out_proj(rows[0])
    done = None
    for i in range(parts):
        y_next = out_proj(rows[i + 1]) if i + 1 < parts else None
        x1 = _layer_norm(y, g1_ref[...], b1_ref[...])
        xb = x1.astype(BF16)
        gate = jax.nn.sigmoid(jnp.dot(xb, wg_ref[...], preferred_element_type=F32))
        acc = gate * jnp.dot(p_ref[rows[i], :].astype(BF16), wp_ref[...], preferred_element_type=F32)
        for c in range(w1_ref.shape[1] // ff_chunk):
            acc = acc + ffn_chunk(xb, c)
            if c == 0 and done is not None:
                finish(*done)
        done = (x1, acc, rows[i])
        y = y_next
    finish(*done)


def _block_tail(mix, x32, p, layer, wo, g1, b1, w1, w2, wg, wp, g2, b2, tm=512, ff_chunk=1024, parts=2):
    t, d = x32.shape
    row = lambda i: (i, 0)
    vec = _const_spec((1, d))
    return pl.pallas_call(
        functools.partial(_block_tail_kernel, ff_chunk=ff_chunk, parts=parts),
        out_shape=(jax.ShapeDtypeStruct((t, d), F32), jax.ShapeDtypeStruct((t, d), BF16)),
        grid=(t // tm,),
        in_specs=[pl.BlockSpec((tm, d), row), pl.BlockSpec((tm, d), row),
                  pl.BlockSpec((None, tm, p.shape[2]), lambda i: (layer, i, 0)),
                  _const_spec(wo.shape), vec, vec, _const_spec(w1.shape), _const_spec(w2.shape),
                  _const_spec(wg.shape), _const_spec(wp.shape), vec, vec],
        out_specs=(pl.BlockSpec((tm, d), row), pl.BlockSpec((tm, d), row)),
        compiler_params=pltpu.CompilerParams(
            dimension_semantics=("parallel",),
            vmem_limit_bytes=_vmem_limit(4 * _nbytes((tm, d), BF16), 4 * _nbytes((tm, d), F32),
                                         2 * _nbytes((tm, p.shape[2]), F32), _nbytes(wo.shape, BF16),
                                         _nbytes(w1.shape, BF16), _nbytes(w2.shape, BF16), _nbytes(wg.shape, BF16),
                                         _nbytes(wp.shape, BF16), 2 * _nbytes((tm, ff_chunk), F32),
                                         4 * _nbytes((tm, d), F32))),
        name="block_tail",
    )(mix, x32, p, wo, g1, b1, w1, w2, wg, wp, g2, b2)


def kernel(x, p, na_w_qkv, na_rpb, na_w_o, gq_w_qkv, gq_q_norm, gq_k_norm, gq_w_o, ml_w_in, ml_b_gates, ml_norm_g,
           ml_w_o, ln1_g, ln1_b, w_ff1, w_ff2, ln2_g, ln2_b, w_ple_gate, w_ple_proj):
    bsz, seq, d = x.shape
    assert d == D_MODEL and seq % (NA_KH * GRID_W) == 0 and p.shape == (DEPTH, bsz, seq, D_PLE)
    t = bsz * seq
    x32 = x.reshape(t, d)
    p_tok = p.reshape(DEPTH, t, D_PLE)
    x16 = None
    qk_w = ML_HEADS * ML_DQK
    for i in range(DEPTH):
        kind, j = i % 3, i // 3
        if kind == 0:
            qkv = _proj(x32 if x16 is None else x16, na_w_qkv[j].astype(BF16), n_chunk=D_MODEL,
                        first_scale=HEAD_DIM ** -0.5 * LOG2E)
            mix = _na_attention(qkv, _na_bias_table(na_rpb[j]), bsz, seq)
            w_o = na_w_o[j]
        elif kind == 1:
            cos4, sin4 = _rope_tables(seq)
            w_qk, w_vt = _gqa_weight_layout(gq_w_qkv[j])
            qk, vt = _gqa_proj(x16, w_qk.astype(BF16), w_vt.astype(BF16), cos4, sin4, _gqa_gain_layout(gq_q_norm[j]),
                               _gqa_gain_layout(gq_k_norm[j]), _group_sum_matrix(), bsz, seq)
            mix = _gqa_attention(qk, vt, bsz, seq)
            w_o = gq_w_o[j]
        else:
            w_in = ml_w_in[j]
            n_main = 2 * qk_w + 2 * D_MODEL
            wkt = w_in[:, qk_w:2 * qk_w].T.astype(BF16)
            wgt = w_in[:, n_main:].T.astype(BF16)
            q, k, v, o, kt, gt = _ml_proj(x16, w_in[:, :n_main].astype(BF16), wkt, wgt, ml_b_gates[j][:, None], bsz, seq)
            gates = gt.reshape(bsz, gt.shape[1], seq // ML_BLOCK, ML_BLOCK)
            mix = _mlstm(q, k, kt, v, o, gates, ml_norm_g[j][None, :], bsz, seq)
            w_o = ml_w_o[j]
        x32, x16 = _block_tail(mix, x32, p_tok, i, w_o.astype(BF16), ln1_g[i][None, :], ln1_b[i][None, :],
                               w_ff1[i].astype(BF16), w_ff2[i].astype(BF16), w_ple_gate[i].astype(BF16),
                               w_ple_proj[i].astype(BF16), ln2_g[i][None, :], ln2_b[i][None, :])
    return x32.reshape(bsz, seq, d)
```

```python
import functools

import jax
import jax.numpy as jnp
import numpy as np
from jax import lax
from jax.experimental import pallas as pl
from jax.experimental.pallas import tpu as pltpu

F32 = jnp.float32
BF16 = jnp.bfloat16

D_MODEL = 1024
DEPTH = 4
GRID_W = 64
HEAD_DIM = 64
D_FF = 4 * D_MODEL
D_PLE = 256
NA_HEADS = 16
NA_KH = 8
NA_KW = 16
GQA_KV_HEADS = 4
GQA_GROUP = 4
ROPE_THETA = 10000.0
ML_HEADS = 8
ML_DV = 128
ML_DQK = 64
ML_CHUNK = 64
ML_BLOCK = 256
DN_ALPHA = (2 * DEPTH) ** 0.25
EPS = 1e-6
LOG2E = 1.4426950408889634

V7X_VMEM_BYTES = 64 * 1024 * 1024
V7X_LANES = 128
SLAB = 2 * V7X_LANES

NT_DIMS = (((1,), (1,)), ((), ()))


def _vmem_limit(*byte_counts):
    est = int(sum(byte_counts) * 1.5) + (4 << 20)
    return min(est, V7X_VMEM_BYTES - (6 << 20))


def _nbytes(shape, dtype):
    return int(np.prod(shape)) * jnp.dtype(dtype).itemsize


def _const_spec(shape):
    nd = len(shape)
    return pl.BlockSpec(shape, lambda *_: (0,) * nd, pipeline_mode=pl.Buffered(1))


def _layer_norm(y, g, b):
    mu = jnp.mean(y, -1, keepdims=True)
    yc = y - mu
    var = jnp.mean(yc * yc, -1, keepdims=True)
    return yc * lax.rsqrt(var + EPS) * g + b


def _proj_kernel(x_ref, w_ref, o_ref, *, n_chunk, first_scale):
    xb = x_ref[...].astype(BF16)
    for j in range(o_ref.shape[1] // n_chunk):
        sl = slice(j * n_chunk, (j + 1) * n_chunk)
        y = jnp.dot(xb, w_ref[:, sl], preferred_element_type=F32)
        if j == 0 and first_scale != 1.0:
            y = y * first_scale
        o_ref[:, sl] = y.astype(BF16)


def _proj(x, w, tm=512, n_chunk=1024, first_scale=1.0):
    t, d = x.shape
    n = w.shape[1]
    return pl.pallas_call(
        functools.partial(_proj_kernel, n_chunk=n_chunk, first_scale=first_scale),
        out_shape=jax.ShapeDtypeStruct((t, n), BF16),
        grid=(t // tm,),
        in_specs=[pl.BlockSpec((tm, d), lambda i: (i, 0)), _const_spec((d, n))],
        out_specs=pl.BlockSpec((tm, n), lambda i: (i, 0)),
        compiler_params=pltpu.CompilerParams(
            dimension_semantics=("parallel",),
            vmem_limit_bytes=_vmem_limit(2 * _nbytes((tm, d), x.dtype), _nbytes((d, n), BF16),
                                         2 * _nbytes((tm, n), BF16), _nbytes((tm, n_chunk), F32))),
        name="proj_plain",
    )(x, w)


def _na_kernel(q_ref, k_ref, v_ref, bias_ref, o_ref, s_ref, p_ref, *, rows, group):
    lane_head = lax.broadcasted_iota(jnp.int32, (1, SLAB), 1) // HEAD_DIM
    win = NA_KH * GRID_W

    def window(r):
        r = jnp.clip(r, 0, rows - 1)
        r0 = jnp.clip(r - NA_KH // 2, 0, rows - NA_KH)
        return pl.multiple_of(r * GRID_W, GRID_W), pl.multiple_of(r0 * GRID_W, GRID_W), r - r0

    def scores(r, slot):
        q0, k0, delta = window(r)
        q = q_ref[pl.ds(q0, GRID_W), :]
        qs = jnp.concatenate([jnp.where(lane_head == h, q, jnp.zeros_like(q)) for h in range(4)], axis=0)
        s = lax.dot_general(qs, k_ref[pl.ds(k0, win), :], NT_DIMS, preferred_element_type=F32)
        s_ref[slot] = s + bias_ref[0, delta]

    def softmax(slot):
        s = s_ref[slot]
        e = jnp.exp2(s - jnp.max(s, -1, keepdims=True))
        p_ref[slot] = (e * (1.0 / jnp.sum(e, -1, keepdims=True))).astype(BF16)

    def weighted_values(r, slot):
        q0, k0, _ = window(r)
        pv = jnp.dot(p_ref[slot], v_ref[pl.ds(k0, win), :], preferred_element_type=F32)
        acc = jnp.zeros((GRID_W, SLAB), F32)
        for h in range(4):
            acc = jnp.where(lane_head == h, pv[h * GRID_W:(h + 1) * GRID_W], acc)
        o_ref[pl.ds(q0, GRID_W), :] = acc.astype(BF16)

    def step(g, bank):
        for t in range(group):
            weighted_values((g - 1) * group + t, (1 - bank) * group + t)
        for t in range(group):
            scores((g + 1) * group + t, (1 - bank) * group + t)
        for t in range(group):
            softmax(bank * group + t)

    for t in range(group):
        scores(t, t)
        p_ref[group + t] = jnp.zeros(p_ref.shape[1:], BF16)

    def body(j, carry):
        step(2 * j, 0)
        step(2 * j + 1, 1)
        return carry

    n_step = rows // group
    lax.fori_loop(0, n_step // 2, body, 0)
    for t in range(group):
        weighted_values(rows - group + t, ((n_step - 1) % 2) * group + t)


def _na_bias_table(rpb):
    col = np.arange(GRID_W)
    c0 = np.clip(col - NA_KW // 2, 0, GRID_W - NA_KW)
    col_in = (col[None, :] >= c0[:, None]) & (col[None, :] < c0[:, None] + NA_KW)
    dc = np.clip(col[None, :] - col[:, None], 1 - NA_KW, NA_KW - 1) + NA_KW - 1
    rpb = rpb.astype(F32)
    by_col = jnp.zeros(rpb.shape[:2] + dc.shape, F32)
    for c in range(2 * NA_KW - 1):
        by_col = jnp.where(dc[None, None] == c, rpb[:, :, c][:, :, None, None], by_col)
    by_col = jnp.where(col_in[None, None], by_col * LOG2E, -jnp.inf)
    per_delta = [by_col[:, NA_KH - 1 - dl:2 * NA_KH - 1 - dl].transpose(0, 2, 1, 3)
                 .reshape(NA_HEADS, GRID_W, NA_KH * GRID_W) for dl in range(NA_KH)]
    b = jnp.stack(per_delta, axis=1).reshape(NA_HEADS // 4, 4, NA_KH, GRID_W, NA_KH * GRID_W)
    return b.transpose(0, 2, 1, 3, 4).reshape(NA_HEADS // 4, NA_KH, 4 * GRID_W, NA_KH * GRID_W)


def _na_attention(qkv, bias, bsz, seq, group=2):
    n_slab = D_MODEL // SLAB
    rows = seq // GRID_W
    assert rows % (2 * group) == 0
    blk = (seq, SLAB)
    tile = (4 * GRID_W, NA_KH * GRID_W)
    return pl.pallas_call(
        functools.partial(_na_kernel, rows=rows, group=group),
        out_shape=jax.ShapeDtypeStruct((bsz * seq, D_MODEL), BF16),
        grid=(n_slab, bsz),
        in_specs=[pl.BlockSpec(blk, lambda s, b: (b, s)),
                  pl.BlockSpec(blk, lambda s, b: (b, n_slab + s)),
                  pl.BlockSpec(blk, lambda s, b: (b, 2 * n_slab + s)),
                  pl.BlockSpec((1,) + bias.shape[1:], lambda s, b: (s, 0, 0, 0))],
        out_specs=pl.BlockSpec(blk, lambda s, b: (b, s)),
        scratch_shapes=[pltpu.VMEM((2 * group,) + tile, F32), pltpu.VMEM((2 * group,) + tile, BF16)],
        compiler_params=pltpu.CompilerParams(
            dimension_semantics=("parallel", "parallel"),
            vmem_limit_bytes=_vmem_limit(8 * _nbytes(blk, BF16), 2 * _nbytes(bias.shape[1:], F32),
                                         (3 * group + 4) * _nbytes(tile, F32))),
        name="na_attention",
    )(qkv, qkv, qkv, bias)


def _gqa_proj_kernel(x_ref, w_ref, wvt_ref, cos_ref, sin_ref, gq_ref, gk_ref, ones_ref, o_ref, vt_ref):
    xb = x_ref[...]
    cos = cos_ref[...]
    sin = sin_ref[...]
    ones = ones_ref[...]
    n_norm = 2 * D_MODEL // (2 * SLAB)

    def project(pair):
        return jnp.dot(xb, w_ref[:, pair * 2 * SLAB:(pair + 1) * 2 * SLAB], preferred_element_type=F32)

    z_next = project(0)
    for pair in range(n_norm):
        z = z_next
        if pair + 1 < n_norm:
            z_next = project(pair + 1)
        else:
            vt_ref[...] = lax.dot_general(wvt_ref[...], xb, NT_DIMS, preferred_element_type=F32).astype(BF16)
        halves = [(z[:, j * SLAB:j * SLAB + V7X_LANES], z[:, j * SLAB + V7X_LANES:(j + 1) * SLAB]) for j in range(2)]
        ss = jnp.concatenate([a * a + b * b for a, b in halves], axis=1)
        hi = ss.astype(BF16)
        lo = (ss - hi.astype(F32)).astype(BF16)
        ms = (jnp.dot(hi, ones, preferred_element_type=F32)
              + jnp.dot(lo, ones, preferred_element_type=F32)) * (1.0 / HEAD_DIM)
        rs = lax.rsqrt(ms + EPS)
        is_q = pair < n_norm // 2
        g_ref = gq_ref if is_q else gk_ref
        for j, (a, b) in enumerate(halves):
            r = rs[:, j * V7X_LANES:(j + 1) * V7X_LANES]
            an = a * r * g_ref[:, :V7X_LANES]
            bn = b * r * g_ref[:, V7X_LANES:]
            oa = an * cos - bn * sin
            ob = an * sin + bn * cos
            if is_q:
                oa = oa * (HEAD_DIM ** -0.5 * LOG2E)
                ob = ob * (HEAD_DIM ** -0.5 * LOG2E)
            c0 = (pair * 2 + j) * SLAB
            o_ref[:, c0:c0 + V7X_LANES] = oa.astype(BF16)
            o_ref[:, c0 + V7X_LANES:c0 + SLAB] = ob.astype(BF16)


def _gqa_weight_layout(w):
    d = w.shape[0]
    kvd = GQA_KV_HEADS * HEAD_DIM
    half = HEAD_DIM // 2
    wq = w[:, :D_MODEL].reshape(d, GQA_KV_HEADS, GQA_GROUP, half, 2).transpose(0, 1, 4, 2, 3)
    wk = w[:, D_MODEL:D_MODEL + kvd].reshape(d, GQA_KV_HEADS, half, 2).transpose(0, 1, 3, 2)
    wk = jnp.broadcast_to(wk[:, :, :, None, :], (d, GQA_KV_HEADS, 2, GQA_GROUP, half))
    return jnp.concatenate([wq.reshape(d, D_MODEL), wk.reshape(d, D_MODEL)], axis=1), w[:, D_MODEL + kvd:].T


def _gqa_gain_layout(g):
    half = HEAD_DIM // 2
    return jnp.broadcast_to(g.reshape(half, 2).T[:, None, :], (2, GQA_GROUP, half)).reshape(1, SLAB)


def _group_sum_matrix():
    blk = np.arange(SLAB) // (HEAD_DIM // 2)
    return jnp.asarray(blk[:, None] == blk[None, :], BF16)


def _rope_tables(seq):
    t = jnp.arange(seq)
    row = (t // GRID_W).astype(F32)
    col = (t % GRID_W).astype(F32)
    n_pairs = HEAD_DIM // 4
    inv = ROPE_THETA ** (-jnp.arange(n_pairs, dtype=F32) / n_pairs)
    ang = jnp.concatenate([row[:, None] * inv, col[:, None] * inv], -1)
    return jnp.tile(jnp.cos(ang), (1, GQA_GROUP)), jnp.tile(jnp.sin(ang), (1, GQA_GROUP))


def _gqa_proj(x16, w, wvt, cos4, sin4, gq, gk, ones, bsz, seq, tm=512):
    t, d = x16.shape
    n = w.shape[1]
    per_seq = seq // tm
    return pl.pallas_call(
        _gqa_proj_kernel,
        out_shape=(jax.ShapeDtypeStruct((t, n), BF16), jax.ShapeDtypeStruct((bsz, wvt.shape[0], seq), BF16)),
        grid=(t // tm,),
        in_specs=[pl.BlockSpec((tm, d), lambda i: (i, 0)), _const_spec((d, n)), _const_spec(wvt.shape),
                  pl.BlockSpec((tm, V7X_LANES), lambda i: (i % per_seq, 0)),
                  pl.BlockSpec((tm, V7X_LANES), lambda i: (i % per_seq, 0)),
                  _const_spec((1, SLAB)), _const_spec((1, SLAB)), _const_spec((SLAB, SLAB))],
        out_specs=(pl.BlockSpec((tm, n), lambda i: (i, 0)),
                   pl.BlockSpec((None, wvt.shape[0], tm), lambda i: (i // per_seq, 0, i % per_seq))),
        compiler_params=pltpu.CompilerParams(
            dimension_semantics=("parallel",),
            vmem_limit_bytes=_vmem_limit(2 * _nbytes((tm, d), BF16), _nbytes((d, n), BF16), _nbytes(wvt.shape, BF16),
                                         2 * _nbytes((tm, n), BF16), 6 * _nbytes((tm, 2 * SLAB), F32))),
        name="gqa_proj",
    )(x16, w, wvt, cos4, sin4, gq, gk, ones)


def _gqa_attn_kernel(q_ref, k_ref, vt_ref, o_ref, *, q_block):
    lane = lax.broadcasted_iota(jnp.int32, (1, SLAB), 1)
    q = q_ref[...]
    q_head = (lane % V7X_LANES) // (HEAD_DIM // 2)

    def scores(h):
        qm = jnp.where(q_head == h, q, jnp.zeros_like(q))
        return lax.dot_general(k_ref[...], qm, NT_DIMS, preferred_element_type=F32)

    def softmax(s):
        e = jnp.exp2(s - jnp.max(s, 0, keepdims=True))
        return e.astype(BF16), 1.0 / jnp.sum(e, 0, keepdims=True)

    def weighted_values(e, inv_l):
        return jnp.dot(vt_ref[...], e, preferred_element_type=F32) * inv_l

    n_qb = q_ref.shape[0] // q_block
    outs = [[None] * GQA_GROUP for _ in range(n_qb)]
    s_next = scores(0)
    pending = []
    for h in range(GQA_GROUP):
        s = s_next
        if h + 1 < GQA_GROUP:
            s_next = scores(h + 1)
        current = []
        for qb in range(n_qb):
            current.append(softmax(s[:, qb * q_block:(qb + 1) * q_block]))
            if pending:
                outs[qb][h - 1] = weighted_values(*pending[qb])
        pending = current
    for qb in range(n_qb):
        outs[qb][GQA_GROUP - 1] = weighted_values(*pending[qb])
        o_ref[qb * q_block:(qb + 1) * q_block, :] = jnp.concatenate(outs[qb], axis=0).T.astype(BF16)


def _gqa_attention(qk, vt, bsz, seq, tq=1024, q_block=256):
    n_slab = D_MODEL // SLAB
    nq = seq // tq
    return pl.pallas_call(
        functools.partial(_gqa_attn_kernel, q_block=q_block),
        out_shape=jax.ShapeDtypeStruct((bsz * seq, D_MODEL), BF16),
        grid=(bsz, n_slab, nq),
        in_specs=[pl.BlockSpec((tq, SLAB), lambda b, g, i: (b * nq + i, g)),
                  pl.BlockSpec((seq, SLAB), lambda b, g, i: (b, n_slab + g)),
                  pl.BlockSpec((None, HEAD_DIM, seq), lambda b, g, i: (b, g, 0))],
        out_specs=pl.BlockSpec((tq, SLAB), lambda b, g, i: (b * nq + i, g)),
        compiler_params=pltpu.CompilerParams(
            dimension_semantics=("parallel", "parallel", "parallel"),
            vmem_limit_bytes=_vmem_limit(4 * _nbytes((tq, SLAB), BF16), 4 * _nbytes((seq, SLAB), BF16),
                                         2 * _nbytes((HEAD_DIM, seq), BF16), 3 * _nbytes((tq, seq), F32))),
        name="gqa_attention",
    )(qk, qk, vt)


def _ml_proj_kernel(x_ref, w_ref, wkt_ref, wgt_ref, bg_ref, q_ref, k_ref, v_ref, o_ref, kt_ref, gt_ref):
    xb = x_ref[...]
    qk_w = ML_HEADS * ML_DQK
    q_ref[...] = (jnp.dot(xb, w_ref[:, :qk_w], preferred_element_type=F32) * (ML_DQK ** -0.5)).astype(BF16)
    k_ref[...] = jnp.dot(xb, w_ref[:, qk_w:2 * qk_w], preferred_element_type=F32).astype(BF16)
    v_ref[...] = jnp.dot(xb, w_ref[:, 2 * qk_w:2 * qk_w + D_MODEL], preferred_element_type=F32).astype(BF16)
    o_ref[...] = jax.nn.sigmoid(jnp.dot(xb, w_ref[:, 2 * qk_w + D_MODEL:], preferred_element_type=F32))
    kt = lax.dot_general(wkt_ref[...], xb, NT_DIMS, preferred_element_type=F32)
    for j in range(kt_ref.shape[1]):
        kt_ref[0, j] = kt[:, j * ML_BLOCK:(j + 1) * ML_BLOCK]
    gt_ref[0] = lax.dot_general(wgt_ref[...], xb, NT_DIMS, preferred_element_type=F32) + bg_ref[...]


def _ml_proj(x16, w, wkt, wgt, bg, bsz, seq, tm=512):
    t, d = x16.shape
    qk_w = ML_HEADS * ML_DQK
    per_seq = seq // tm
    n_gate = wgt.shape[0]
    row = lambda i: (i, 0)
    return pl.pallas_call(
        _ml_proj_kernel,
        out_shape=(jax.ShapeDtypeStruct((t, qk_w), BF16), jax.ShapeDtypeStruct((t, qk_w), BF16),
                   jax.ShapeDtypeStruct((t, D_MODEL), BF16), jax.ShapeDtypeStruct((t, D_MODEL), F32),
                   jax.ShapeDtypeStruct((bsz, seq // ML_BLOCK, qk_w, ML_BLOCK), F32),
                   jax.ShapeDtypeStruct((bsz, n_gate, seq), F32)),
        grid=(t // tm,),
        in_specs=[pl.BlockSpec((tm, d), row), _const_spec(w.shape), _const_spec(wkt.shape),
                  _const_spec(wgt.shape), _const_spec(bg.shape)],
        out_specs=(pl.BlockSpec((tm, qk_w), row), pl.BlockSpec((tm, qk_w), row),
                   pl.BlockSpec((tm, D_MODEL), row), pl.BlockSpec((tm, D_MODEL), row),
                   pl.BlockSpec((1, tm // ML_BLOCK, qk_w, ML_BLOCK), lambda i: (i // per_seq, i % per_seq, 0, 0)),
                   pl.BlockSpec((1, n_gate, tm), lambda i: (i // per_seq, 0, i % per_seq))),
        compiler_params=pltpu.CompilerParams(
            dimension_semantics=("parallel",),
            vmem_limit_bytes=_vmem_limit(2 * _nbytes((tm, d), BF16), _nbytes(w.shape, BF16), _nbytes(wkt.shape, BF16),
                                         2 * _nbytes((tm, 2 * qk_w + D_MODEL), BF16), 2 * _nbytes((tm, D_MODEL), F32),
                                         4 * _nbytes((qk_w, tm), F32), 2 * _nbytes((tm, D_MODEL), F32))),
        name="mlstm_proj",
    )(x16, w, wkt, wgt, bg)


def _log_sigmoid(x):
    return jnp.minimum(x, 0.0) - jnp.log1p(jnp.exp(-jnp.abs(x)))


def _exact_dot(x, m):
    hi = x.astype(BF16)
    r1 = x - hi.astype(F32)
    mid = r1.astype(BF16)
    lo = (r1 - mid.astype(F32)).astype(BF16)
    return (jnp.dot(hi, m, preferred_element_type=F32) + jnp.dot(mid, m, preferred_element_type=F32)
            + jnp.dot(lo, m, preferred_element_type=F32))


def _lane_cummax(x, lane_pos, reverse):
    shift = 1
    while shift < ML_CHUNK:
        if reverse:
            moved, ok = pltpu.roll(x, x.shape[1] - shift, axis=1), lane_pos < ML_CHUNK - shift
        else:
            moved, ok = pltpu.roll(x, shift, axis=1), lane_pos >= shift
        x = jnp.maximum(x, jnp.where(ok, moved, -jnp.inf))
        shift *= 2
    return x


def _mlstm_block(q, k, kt, v, a_row, ge_row, decay_row, m_row, g_col, b_col, c_state, n_state, vis, same, row_chunk,
                 reverse):
    per = ML_BLOCK // ML_CHUNK
    v_ext = jnp.concatenate([v, jnp.ones_like(v)], axis=1)
    wkt = jnp.exp(a_row - ge_row) * kt
    wkt4 = jnp.where(same, jnp.concatenate([wkt] * per, axis=0), 0.0)
    delta = jnp.dot(wkt4.astype(BF16), v_ext, preferred_element_type=F32)
    yield
    g_rep = jnp.broadcast_to(g_col, (ML_BLOCK, ML_DV))
    b_rep = jnp.broadcast_to(b_col, (ML_BLOCK, ML_DV))
    m_rep = jnp.broadcast_to(m_row[:, (per - 1) * ML_CHUNK:(per - 1) * ML_CHUNK + 1], g_rep.shape)
    for i in range(per - 2, -1, -1):
        m_rep = jnp.where(row_chunk == i, m_row[:, i * ML_CHUNK:i * ML_CHUNK + 1], m_rep)
    s_inter = jnp.exp(m_rep - g_rep)
    floor_rep = jnp.exp(-(b_rep + g_rep))
    w = jnp.where(vis, jnp.exp(a_row - jnp.concatenate([g_rep, g_rep], axis=1)), 0.0)
    qk = lax.dot_general(q, k, NT_DIMS, preferred_element_type=F32) * w
    yield
    intra = jnp.dot(qk.astype(BF16), v_ext, preferred_element_type=F32)
    starts = [None] * per
    state = jnp.concatenate([c_state, n_state], axis=1)
    for i in (range(per - 1, -1, -1) if reverse else range(per)):
        starts[i] = state
        state = decay_row[:, i * ML_CHUNK:i * ML_CHUNK + 1] * state + delta[i * ML_DQK:(i + 1) * ML_DQK]
    yield
    q4 = jnp.where(same, jnp.concatenate([q] * per, axis=1), jnp.zeros((), q.dtype))
    inter = jnp.dot(q4, jnp.concatenate(starts, axis=0).astype(BF16), preferred_element_type=F32)
    num = s_inter * inter[:, :ML_DV] + intra[:, :ML_DV]
    den = s_inter * inter[:, ML_DV:] + intra[:, ML_DV:]
    h = num / jnp.maximum(jnp.abs(den), floor_rep)
    yield h, state[:, :ML_DV], state[:, ML_DV:]


def _mlstm_kernel(q_ref, k_ref, kt_ref, v_ref, o_ref, g_ref, ng_ref, out_ref, hfw_ref, hbw_ref, row_ref, col_ref,
                  *, n_block):
    pair = pl.program_id(1)
    per = ML_BLOCK // ML_CHUNK
    t_idx = lax.broadcasted_iota(jnp.int32, (ML_BLOCK, ML_BLOCK), 0)
    s_idx = lax.broadcasted_iota(jnp.int32, (ML_BLOCK, ML_BLOCK), 1)
    same = (t_idx // ML_CHUNK) == (s_idx // ML_CHUNK)
    masks = (same & (s_idx <= t_idx), same & (s_idx >= t_idx))
    lane = lax.broadcasted_iota(jnp.int32, (1, ML_BLOCK), 1)
    lane_chunk, lane_pos = lane // ML_CHUNK, lane % ML_CHUNK
    blk_row = lax.broadcasted_iota(jnp.int32, (n_block, 1), 0)
    row_chunk = lax.broadcasted_iota(jnp.int32, (ML_BLOCK, ML_DV), 0) // ML_CHUNK

    per_query = []
    for direction in range(2):
        cum = jnp.where(masks[1 - direction], 1.0, 0.0).astype(BF16)
        for hh in range(2):
            chain = direction * 2 + hh
            head = pair * 2 + hh
            lf = _log_sigmoid(g_ref[0, (direction * 2 + 1) * ML_HEADS + head])
            b = _exact_dot(lf, cum)
            a = g_ref[0, direction * 2 * ML_HEADS + head] - b
            a_max = [jnp.max(jnp.where(lane_chunk == i, a, -jnp.inf), axis=1, keepdims=True) for i in range(per)]
            f_sum = [jnp.sum(jnp.where(lane_chunk == i, lf, 0.0), axis=1, keepdims=True) for i in range(per)]
            m = jnp.zeros((1, 1), F32)
            m_row = jnp.zeros(a.shape, F32)
            ge_row = jnp.zeros(a.shape, F32)
            n_chunk = n_block * per
            for c in (range(n_chunk - 1, -1, -1) if direction else range(n_chunk)):
                blk, i = divmod(c, per)
                g_end = jnp.maximum(m, a_max[i][blk:blk + 1])
                here = (blk_row == blk) & (lane_chunk == i)
                m_row = jnp.where(here, m, m_row)
                ge_row = jnp.where(here, g_end, ge_row)
                m = f_sum[i][blk:blk + 1] + g_end
            g_row = jnp.maximum(m_row, _lane_cummax(a, lane_pos, reverse=bool(direction)))
            row_ref[chain, 0] = a
            row_ref[chain, 1] = ge_row
            row_ref[chain, 2] = jnp.exp(m_row - ge_row)
            row_ref[chain, 3] = m_row
            per_query += [g_row, b]
    flat = [jnp.concatenate([x[blk:blk + 1] for blk in range(n_block)], axis=1) for x in per_query]
    col_ref[...] = jnp.concatenate(flat, axis=0).T

    def body(step, carry):
        chains = []
        for direction in range(2):
            blk = step if direction == 0 else n_block - 1 - step
            rows = pl.ds(pl.multiple_of(blk * ML_BLOCK, ML_BLOCK), ML_BLOCK)
            for hh in range(2):
                chain = direction * 2 + hh
                c_state, n_state = carry[chain]
                stages = _mlstm_block(
                    q_ref[rows, hh * ML_DQK:(hh + 1) * ML_DQK], k_ref[rows, hh * ML_DQK:(hh + 1) * ML_DQK],
                    kt_ref[0, blk, hh * ML_DQK:(hh + 1) * ML_DQK, :], v_ref[rows, hh * ML_DV:(hh + 1) * ML_DV],
                    *[row_ref[chain, j, pl.ds(blk, 1), :] for j in range(4)],
                    col_ref[rows, 2 * chain:2 * chain + 1], col_ref[rows, 2 * chain + 1:2 * chain + 2],
                    c_state, n_state, masks[direction], same, row_chunk, reverse=bool(direction))
                chains.append((stages, hfw_ref if direction == 0 else hbw_ref, rows, hh))
        for _ in range(3):
            for stages, _, _, _ in chains:
                next(stages)
        new_carry = []
        for stages, dst, rows, hh in chains:
            h, c_new, n_new = next(stages)
            dst[rows, hh * ML_DV:(hh + 1) * ML_DV] = h
            new_carry.append((c_new, n_new))
        return tuple(new_carry)

    init = tuple((jnp.zeros((ML_DQK, ML_DV), F32), jnp.zeros((ML_DQK, ML_DV), F32)) for _ in range(4))
    lax.fori_loop(0, n_block, body, init)

    for hh in range(2):
        cols = slice(hh * ML_DV, (hh + 1) * ML_DV)
        h = hfw_ref[:, cols] + hbw_ref[:, cols]
        ms = jnp.mean(h * h, -1, keepdims=True)
        hn = h * lax.rsqrt(ms + EPS) * ng_ref[:, cols]
        out_ref[:, cols] = (o_ref[:, cols] * hn).astype(BF16)


def _mlstm(q, k, kt, v, o, gates, norm_g, bsz, seq):
    n_pair = ML_HEADS // 2
    n_block = seq // ML_BLOCK
    pair_w = 2 * ML_DV
    return pl.pallas_call(
        functools.partial(_mlstm_kernel, n_block=n_block),
        out_shape=jax.ShapeDtypeStruct((bsz * seq, D_MODEL), BF16),
        grid=(bsz, n_pair),
        in_specs=[pl.BlockSpec((seq, 2 * ML_DQK), lambda b, p: (b, p)),
                  pl.BlockSpec((seq, 2 * ML_DQK), lambda b, p: (b, p)),
                  pl.BlockSpec((1, n_block, 2 * ML_DQK, ML_BLOCK), lambda b, p: (b, 0, p, 0)),
                  pl.BlockSpec((seq, pair_w), lambda b, p: (b, p)),
                  pl.BlockSpec((seq, pair_w), lambda b, p: (b, p)),
                  pl.BlockSpec((1,) + gates.shape[1:], lambda b, p: (b, 0, 0, 0)),
                  pl.BlockSpec((1, pair_w), lambda b, p: (0, p))],
        out_specs=pl.BlockSpec((seq, pair_w), lambda b, p: (b, p)),
        scratch_shapes=[pltpu.VMEM((seq, pair_w), F32), pltpu.VMEM((seq, pair_w), F32),
                        pltpu.VMEM((4, 4, n_block, ML_BLOCK), F32), pltpu.VMEM((seq, 8), F32)],
        compiler_params=pltpu.CompilerParams(
            dimension_semantics=("parallel", "parallel"),
            vmem_limit_bytes=_vmem_limit(4 * _nbytes((seq, 2 * ML_DQK), BF16), 2 * _nbytes((seq, 2 * ML_DQK), F32),
                                         4 * _nbytes((seq, pair_w), BF16), 2 * _nbytes((seq, pair_w), F32),
                                         2 * _nbytes(gates.shape[1:], F32), 2 * _nbytes((seq, pair_w), F32))),
        name="mlstm_scan",
    )(q, k, kt, v, o, gates, norm_g)


def _block_tail_kernel(mix_ref, x_ref, p_ref, wo_ref, g1_ref, b1_ref, w1_ref, w2_ref, wg_ref, wp_ref, g2_ref, b2_ref,
                       o32_ref, o16_ref, *, ff_chunk, parts):
    tr = x_ref.shape[0] // parts
    rows = [slice(i * tr, (i + 1) * tr) for i in range(parts)]

    def out_proj(r):
        return jnp.dot(mix_ref[r, :], wo_ref[...], preferred_element_type=F32) + DN_ALPHA * x_ref[r, :]

    def ffn_chunk(xb, c):
        sl = slice(c * ff_chunk, (c + 1) * ff_chunk)
        h = jnp.maximum(jnp.dot(xb, w1_ref[:, sl], preferred_element_type=F32), 0.0)
        return jnp.dot((h * h).astype(BF16), w2_ref[sl, :], preferred_element_type=F32)

    def finish(x1, acc, r):
        z = _layer_norm(DN_ALPHA * x1 + acc, g2_ref[...], b2_ref[...])
        o32_ref[r, :] = z
        o16_ref[r, :] = z.astype(BF16)

    y = out_proj(rows[0])
    done = None
    for i in range(parts):
        y_next = out_proj(rows[i + 1]) if i + 1 < parts else None
        x1 = _layer_norm(y, g1_ref[...], b1_ref[...])
        xb = x1.astype(BF16)
        gate = jax.nn.sigmoid(jnp.dot(xb, wg_ref[...], preferred_element_type=F32))
        acc = gate * jnp.dot(p_ref[rows[i], :].astype(BF16), wp_ref[...], preferred_element_type=F32)
        for c in range(w1_ref.shape[1] // ff_chunk):
            acc = acc + ffn_chunk(xb, c)
            if c == 0 and done is not None:
                finish(*done)
        done = (x1, acc, rows[i])
        y = y_next
    finish(*done)


def _block_tail(mix, x32, wo, layer, p, g1, b1, w1, w2, wg, wp, g2, b2, tm=512, ff_chunk=1024, parts=2):
    t, d = x32.shape
    row = lambda i: (i, 0)

    def of_layer(a):
        nd = a.ndim - 1
        return pl.BlockSpec((None,) + a.shape[1:], lambda i: (layer,) + (0,) * nd, pipeline_mode=pl.Buffered(1))

    return pl.pallas_call(
        functools.partial(_block_tail_kernel, ff_chunk=ff_chunk, parts=parts),
        out_shape=(jax.ShapeDtypeStruct((t, d), F32), jax.ShapeDtypeStruct((t, d), BF16)),
        grid=(t // tm,),
        in_specs=[pl.BlockSpec((tm, d), row), pl.BlockSpec((tm, d), row),
                  pl.BlockSpec((None, tm, p.shape[2]), lambda i: (layer, i, 0)),
                  _const_spec(wo.shape), of_layer(g1), of_layer(b1), of_layer(w1), of_layer(w2),
                  of_layer(wg), of_layer(wp), of_layer(g2), of_layer(b2)],
        out_specs=(pl.BlockSpec((tm, d), row), pl.BlockSpec((tm, d), row)),
        compiler_params=pltpu.CompilerParams(
            dimension_semantics=("parallel",),
            vmem_limit_bytes=_vmem_limit(4 * _nbytes((tm, d), BF16), 4 * _nbytes((tm, d), F32),
                                         2 * _nbytes((tm, p.shape[2]), F32), _nbytes(wo.shape, BF16),
                                         _nbytes(w1.shape[1:], BF16), _nbytes(w2.shape[1:], BF16),
                                         _nbytes(wg.shape[1:], BF16), _nbytes(wp.shape[1:], BF16),
                                         2 * _nbytes((tm, ff_chunk), F32), 4 * _nbytes((tm, d), F32))),
        name="block_tail",
    )(mix, x32, p, wo, g1, b1, w1, w2, wg, wp, g2, b2)


def kernel(x, p, na_w_qkv, na_rpb, na_w_o, gq_w_qkv, gq_q_norm, gq_k_norm, gq_w_o, ml_w_in, ml_b_gates, ml_norm_g,
           ml_w_o, ln1_g, ln1_b, w_ff1, w_ff2, ln2_g, ln2_b, w_ple_gate, w_ple_proj):
    bsz, seq, d = x.shape
    assert d == D_MODEL and seq % (NA_KH * GRID_W) == 0 and p.shape == (DEPTH, bsz, seq, D_PLE)
    t = bsz * seq
    x32 = x.reshape(t, d)
    per_layer = (p.reshape(DEPTH, t, D_PLE), ln1_g[:, None, :], ln1_b[:, None, :], w_ff1.astype(BF16),
                 w_ff2.astype(BF16), w_ple_gate.astype(BF16), w_ple_proj.astype(BF16), ln2_g[:, None, :],
                 ln2_b[:, None, :])
    x16 = None
    qk_w = ML_HEADS * ML_DQK
    for i in range(DEPTH):
        kind, j = i % 3, i // 3
        if kind == 0:
            qkv = _proj(x32 if x16 is None else x16, na_w_qkv[j].astype(BF16), n_chunk=D_MODEL,
                        first_scale=HEAD_DIM ** -0.5 * LOG2E)
            mix = _na_attention(qkv, _na_bias_table(na_rpb[j]), bsz, seq)
            w_o = na_w_o[j]
        elif kind == 1:
            cos4, sin4 = _rope_tables(seq)
            w_qk, w_vt = _gqa_weight_layout(gq_w_qkv[j])
            qk, vt = _gqa_proj(x16, w_qk.astype(BF16), w_vt.astype(BF16), cos4, sin4, _gqa_gain_layout(gq_q_norm[j]),
                               _gqa_gain_layout(gq_k_norm[j]), _group_sum_matrix(), bsz, seq)
            mix = _gqa_attention(qk, vt, bsz, seq)
            w_o = gq_w_o[j]
        else:
            w_in = ml_w_in[j]
            n_main = 2 * qk_w + 2 * D_MODEL
            wkt = w_in[:, qk_w:2 * qk_w].T.astype(BF16)
            wgt = w_in[:, n_main:].T.astype(BF16)
            q, k, v, o, kt, gt = _ml_proj(x16, w_in[:, :n_main].astype(BF16), wkt, wgt, ml_b_gates[j][:, None], bsz, seq)
            gates = gt.reshape(bsz, gt.shape[1], seq // ML_BLOCK, ML_BLOCK)
            mix = _mlstm(q, k, kt, v, o, gates, ml_norm_g[j][None, :], bsz, seq)
            w_o = ml_w_o[j]
        x32, x16 = _block_tail(mix, x32, w_o.astype(BF16), i, *per_layer)
    return x32.reshape(bsz, seq, d)
```

```python
import functools

import jax
import jax.numpy as jnp
import numpy as np
from jax import lax
from jax.experimental import pallas as pl
from jax.experimental.pallas import tpu as pltpu

F32 = jnp.float32
BF16 = jnp.bfloat16

D_MODEL = 1024
DEPTH = 4
GRID_W = 64
HEAD_DIM = 64
D_PLE = 256
NA_HEADS = 16
NA_KH = 8
NA_KW = 16
GQA_KV_HEADS = 4
GQA_GROUP = 4
ROPE_THETA = 10000.0
ML_HEADS = 8
ML_DV = 128
ML_DQK = 64
ML_CHUNK = 64
ML_BLOCK = 256
DN_ALPHA = (2 * DEPTH) ** 0.25
EPS = 1e-6
LOG2E = 1.4426950408889634

V7X_VMEM_BYTES = 64 * 1024 * 1024
V7X_LANES = 128
SLAB = 2 * V7X_LANES

NT_DIMS = (((1,), (1,)), ((), ()))


def _vmem_limit(*byte_counts):
    est = int(sum(byte_counts) * 1.5) + (4 << 20)
    return min(est, V7X_VMEM_BYTES - (6 << 20))


def _nbytes(shape, dtype):
    return int(np.prod(shape)) * jnp.dtype(dtype).itemsize


def _const_spec(shape):
    nd = len(shape)
    return pl.BlockSpec(shape, lambda *_: (0,) * nd, pipeline_mode=pl.Buffered(1))


def _layer_norm(y, g, b):
    mu = jnp.mean(y, -1, keepdims=True)
    yc = y - mu
    var = jnp.mean(yc * yc, -1, keepdims=True)
    return yc * lax.rsqrt(var + EPS) * g + b


def _proj_kernel(x_ref, w_ref, o_ref, *, n_chunk, first_scale):
    xb = x_ref[...].astype(BF16)
    for j in range(o_ref.shape[1] // n_chunk):
        sl = slice(j * n_chunk, (j + 1) * n_chunk)
        y = jnp.dot(xb, w_ref[:, sl], preferred_element_type=F32)
        if j == 0 and first_scale != 1.0:
            y = y * first_scale
        o_ref[:, sl] = y.astype(BF16)


def _proj(x, w, tm=512, n_chunk=1024, first_scale=1.0):
    t, d = x.shape
    n = w.shape[1]
    return pl.pallas_call(
        functools.partial(_proj_kernel, n_chunk=n_chunk, first_scale=first_scale),
        out_shape=jax.ShapeDtypeStruct((t, n), BF16),
        grid=(t // tm,),
        in_specs=[pl.BlockSpec((tm, d), lambda i: (i, 0)), _const_spec((d, n))],
        out_specs=pl.BlockSpec((tm, n), lambda i: (i, 0)),
        compiler_params=pltpu.CompilerParams(
            dimension_semantics=("parallel",),
            vmem_limit_bytes=_vmem_limit(2 * _nbytes((tm, d), x.dtype), _nbytes((d, n), BF16),
                                         2 * _nbytes((tm, n), BF16), _nbytes((tm, n_chunk), F32))),
        name="proj_plain",
    )(x, w)


def _na_kernel(q_ref, k_ref, v_ref, bias_ref, o_ref, s_ref, p_ref, *, rows, group):
    lane_head = lax.broadcasted_iota(jnp.int32, (1, SLAB), 1) // HEAD_DIM
    win = NA_KH * GRID_W

    def window(r):
        r = jnp.clip(r, 0, rows - 1)
        r0 = jnp.clip(r - NA_KH // 2, 0, rows - NA_KH)
        return pl.multiple_of(r * GRID_W, GRID_W), pl.multiple_of(r0 * GRID_W, GRID_W), r - r0

    def scores(r, slot):
        q0, k0, delta = window(r)
        q = q_ref[pl.ds(q0, GRID_W), :]
        qs = jnp.concatenate([jnp.where(lane_head == h, q, jnp.zeros_like(q)) for h in range(4)], axis=0)
        s = lax.dot_general(qs, k_ref[pl.ds(k0, win), :], NT_DIMS, preferred_element_type=F32)
        s_ref[slot] = s + bias_ref[0, delta]

    def softmax(slot):
        s = s_ref[slot]
        e = jnp.exp2(s - jnp.max(s, -1, keepdims=True))
        p_ref[slot] = (e * (1.0 / jnp.sum(e, -1, keepdims=True))).astype(BF16)

    def weighted_values(r, slot):
        q0, k0, _ = window(r)
        pv = jnp.dot(p_ref[slot], v_ref[pl.ds(k0, win), :], preferred_element_type=F32)
        acc = jnp.zeros((GRID_W, SLAB), F32)
        for h in range(4):
            acc = jnp.where(lane_head == h, pv[h * GRID_W:(h + 1) * GRID_W], acc)
        o_ref[pl.ds(q0, GRID_W), :] = acc.astype(BF16)

    def step(g, bank):
        for t in range(group):
            weighted_values((g - 1) * group + t, (1 - bank) * group + t)
        for t in range(group):
            scores((g + 1) * group + t, (1 - bank) * group + t)
        for t in range(group):
            softmax(bank * group + t)

    for t in range(group):
        scores(t, t)
        p_ref[group + t] = jnp.zeros(p_ref.shape[1:], BF16)

    def body(j, carry):
        step(2 * j, 0)
        step(2 * j + 1, 1)
        return carry

    n_step = rows // group
    lax.fori_loop(0, n_step // 2, body, 0)
    for t in range(group):
        weighted_values(rows - group + t, ((n_step - 1) % 2) * group + t)


def _na_bias_table(rpb):
    col = np.arange(GRID_W)
    c0 = np.clip(col - NA_KW // 2, 0, GRID_W - NA_KW)
    col_in = (col[None, :] >= c0[:, None]) & (col[None, :] < c0[:, None] + NA_KW)
    dc = np.clip(col[None, :] - col[:, None], 1 - NA_KW, NA_KW - 1) + NA_KW - 1
    rpb = rpb.astype(F32)
    by_col = jnp.zeros(rpb.shape[:2] + dc.shape, F32)
    for c in range(2 * NA_KW - 1):
        by_col = jnp.where(dc[None, None] == c, rpb[:, :, c][:, :, None, None], by_col)
    by_col = jnp.where(col_in[None, None], by_col * LOG2E, -jnp.inf)
    per_delta = [by_col[:, NA_KH - 1 - dl:2 * NA_KH - 1 - dl].transpose(0, 2, 1, 3)
                 .reshape(NA_HEADS, GRID_W, NA_KH * GRID_W) for dl in range(NA_KH)]
    b = jnp.stack(per_delta, axis=1).reshape(NA_HEADS // 4, 4, NA_KH, GRID_W, NA_KH * GRID_W)
    return b.transpose(0, 2, 1, 3, 4).reshape(NA_HEADS // 4, NA_KH, 4 * GRID_W, NA_KH * GRID_W)


def _na_attention(qkv, bias, bsz, seq, group=2):
    n_slab = D_MODEL // SLAB
    rows = seq // GRID_W
    assert rows % (2 * group) == 0
    blk = (seq, SLAB)
    tile = (4 * GRID_W, NA_KH * GRID_W)
    return pl.pallas_call(
        functools.partial(_na_kernel, rows=rows, group=group),
        out_shape=jax.ShapeDtypeStruct((bsz * seq, D_MODEL), BF16),
        grid=(n_slab, bsz),
        in_specs=[pl.BlockSpec(blk, lambda s, b: (b, s)),
                  pl.BlockSpec(blk, lambda s, b: (b, n_slab + s)),
                  pl.BlockSpec(blk, lambda s, b: (b, 2 * n_slab + s)),
                  pl.BlockSpec((1,) + bias.shape[1:], lambda s, b: (s, 0, 0, 0))],
        out_specs=pl.BlockSpec(blk, lambda s, b: (b, s)),
        scratch_shapes=[pltpu.VMEM((2 * group,) + tile, F32), pltpu.VMEM((2 * group,) + tile, BF16)],
        compiler_params=pltpu.CompilerParams(
            dimension_semantics=("parallel", "parallel"),
            vmem_limit_bytes=_vmem_limit(8 * _nbytes(blk, BF16), 2 * _nbytes(bias.shape[1:], F32),
                                         (3 * group + 4) * _nbytes(tile, F32))),
        name="na_attention",
    )(qkv, qkv, qkv, bias)


def _gqa_proj_kernel(x_ref, w_ref, wvt_ref, cos_ref, sin_ref, gq_ref, gk_ref, ones_ref, o_ref, vt_ref):
    xb = x_ref[...]
    cos = cos_ref[...]
    sin = sin_ref[...]
    ones = ones_ref[...]
    n_norm = 2 * D_MODEL // (2 * SLAB)

    def project(pair):
        return jnp.dot(xb, w_ref[:, pair * 2 * SLAB:(pair + 1) * 2 * SLAB], preferred_element_type=F32)

    z_next = project(0)
    for pair in range(n_norm):
        z = z_next
        if pair + 1 < n_norm:
            z_next = project(pair + 1)
        else:
            vt_ref[...] = lax.dot_general(wvt_ref[...], xb, NT_DIMS, preferred_element_type=F32).astype(BF16)
        halves = [(z[:, j * SLAB:j * SLAB + V7X_LANES], z[:, j * SLAB + V7X_LANES:(j + 1) * SLAB]) for j in range(2)]
        ss = jnp.concatenate([a * a + b * b for a, b in halves], axis=1)
        hi = ss.astype(BF16)
        lo = (ss - hi.astype(F32)).astype(BF16)
        ms = (jnp.dot(hi, ones, preferred_element_type=F32)
              + jnp.dot(lo, ones, preferred_element_type=F32)) * (1.0 / HEAD_DIM)
        rs = lax.rsqrt(ms + EPS)
        is_q = pair < n_norm // 2
        g_ref = gq_ref if is_q else gk_ref
        for j, (a, b) in enumerate(halves):
            r = rs[:, j * V7X_LANES:(j + 1) * V7X_LANES]
            an = a * r * g_ref[:, :V7X_LANES]
            bn = b * r * g_ref[:, V7X_LANES:]
            oa = an * cos - bn * sin
            ob = an * sin + bn * cos
            if is_q:
                oa = oa * (HEAD_DIM ** -0.5 * LOG2E)
                ob = ob * (HEAD_DIM ** -0.5 * LOG2E)
            c0 = (pair * 2 + j) * SLAB
            o_ref[:, c0:c0 + V7X_LANES] = oa.astype(BF16)
            o_ref[:, c0 + V7X_LANES:c0 + SLAB] = ob.astype(BF16)


def _gqa_weight_layout(w):
    d = w.shape[0]
    kvd = GQA_KV_HEADS * HEAD_DIM
    half = HEAD_DIM // 2
    wq = w[:, :D_MODEL].reshape(d, GQA_KV_HEADS, GQA_GROUP, half, 2).transpose(0, 1, 4, 2, 3)
    wk = w[:, D_MODEL:D_MODEL + kvd].reshape(d, GQA_KV_HEADS, half, 2).transpose(0, 1, 3, 2)
    wk = jnp.broadcast_to(wk[:, :, :, None, :], (d, GQA_KV_HEADS, 2, GQA_GROUP, half))
    return jnp.concatenate([wq.reshape(d, D_MODEL), wk.reshape(d, D_MODEL)], axis=1), w[:, D_MODEL + kvd:].T


def _gqa_gain_layout(g):
    half = HEAD_DIM // 2
    return jnp.broadcast_to(g.reshape(half, 2).T[:, None, :], (2, GQA_GROUP, half)).reshape(1, SLAB)


def _group_sum_matrix():
    blk = np.arange(SLAB) // (HEAD_DIM // 2)
    return jnp.asarray(blk[:, None] == blk[None, :], BF16)


def _rope_tables(seq):
    t = jnp.arange(seq)
    row = (t // GRID_W).astype(F32)
    col = (t % GRID_W).astype(F32)
    n_pairs = HEAD_DIM // 4
    inv = ROPE_THETA ** (-jnp.arange(n_pairs, dtype=F32) / n_pairs)
    ang = jnp.concatenate([row[:, None] * inv, col[:, None] * inv], -1)
    return jnp.tile(jnp.cos(ang), (1, GQA_GROUP)), jnp.tile(jnp.sin(ang), (1, GQA_GROUP))


def _gqa_proj(x16, w, wvt, cos4, sin4, gq, gk, ones, bsz, seq, tm=512):
    t, d = x16.shape
    n = w.shape[1]
    per_seq = seq // tm
    return pl.pallas_call(
        _gqa_proj_kernel,
        out_shape=(jax.ShapeDtypeStruct((t, n), BF16), jax.ShapeDtypeStruct((bsz, wvt.shape[0], seq), BF16)),
        grid=(t // tm,),
        in_specs=[pl.BlockSpec((tm, d), lambda i: (i, 0)), _const_spec((d, n)), _const_spec(wvt.shape),
                  pl.BlockSpec((tm, V7X_LANES), lambda i: (i % per_seq, 0)),
                  pl.BlockSpec((tm, V7X_LANES), lambda i: (i % per_seq, 0)),
                  _const_spec((1, SLAB)), _const_spec((1, SLAB)), _const_spec((SLAB, SLAB))],
        out_specs=(pl.BlockSpec((tm, n), lambda i: (i, 0)),
                   pl.BlockSpec((None, wvt.shape[0], tm), lambda i: (i // per_seq, 0, i % per_seq))),
        compiler_params=pltpu.CompilerParams(
            dimension_semantics=("parallel",),
            vmem_limit_bytes=_vmem_limit(2 * _nbytes((tm, d), BF16), _nbytes((d, n), BF16), _nbytes(wvt.shape, BF16),
                                         2 * _nbytes((tm, n), BF16), 6 * _nbytes((tm, 2 * SLAB), F32))),
        name="gqa_proj",
    )(x16, w, wvt, cos4, sin4, gq, gk, ones)


def _gqa_attn_kernel(q_ref, k_ref, vt_ref, o_ref, *, q_block):
    lane = lax.broadcasted_iota(jnp.int32, (1, SLAB), 1)
    q = q_ref[...]
    q_head = (lane % V7X_LANES) // (HEAD_DIM // 2)

    def scores(h):
        qm = jnp.where(q_head == h, q, jnp.zeros_like(q))
        return lax.dot_general(k_ref[...], qm, NT_DIMS, preferred_element_type=F32)

    def softmax(s):
        e = jnp.exp2(s - jnp.max(s, 0, keepdims=True))
        return e.astype(BF16), 1.0 / jnp.sum(e, 0, keepdims=True)

    def weighted_values(e, inv_l):
        return jnp.dot(vt_ref[...], e, preferred_element_type=F32) * inv_l

    n_qb = q_ref.shape[0] // q_block
    outs = [[None] * GQA_GROUP for _ in range(n_qb)]
    s_next = scores(0)
    pending = []
    for h in range(GQA_GROUP):
        s = s_next
        if h + 1 < GQA_GROUP:
            s_next = scores(h + 1)
        current = []
        for qb in range(n_qb):
            current.append(softmax(s[:, qb * q_block:(qb + 1) * q_block]))
            if pending:
                outs[qb][h - 1] = weighted_values(*pending[qb])
        pending = current
    for qb in range(n_qb):
        outs[qb][GQA_GROUP - 1] = weighted_values(*pending[qb])
        o_ref[qb * q_block:(qb + 1) * q_block, :] = jnp.concatenate(outs[qb], axis=0).T.astype(BF16)


def _gqa_attention(qk, vt, bsz, seq, tq=1024, q_block=256):
    n_slab = D_MODEL // SLAB
    nq = seq // tq
    return pl.pallas_call(
        functools.partial(_gqa_attn_kernel, q_block=q_block),
        out_shape=jax.ShapeDtypeStruct((bsz * seq, D_MODEL), BF16),
        grid=(bsz, n_slab, nq),
        in_specs=[pl.BlockSpec((tq, SLAB), lambda b, g, i: (b * nq + i, g)),
                  pl.BlockSpec((seq, SLAB), lambda b, g, i: (b, n_slab + g)),
                  pl.BlockSpec((None, HEAD_DIM, seq), lambda b, g, i: (b, g, 0))],
        out_specs=pl.BlockSpec((tq, SLAB), lambda b, g, i: (b * nq + i, g)),
        compiler_params=pltpu.CompilerParams(
            dimension_semantics=("parallel", "parallel", "parallel"),
            vmem_limit_bytes=_vmem_limit(4 * _nbytes((tq, SLAB), BF16), 4 * _nbytes((seq, SLAB), BF16),
                                         2 * _nbytes((HEAD_DIM, seq), BF16), 3 * _nbytes((tq, seq), F32))),
        name="gqa_attention",
    )(qk, qk, vt)


def _ml_proj_kernel(x_ref, w_ref, wkt_ref, wgt_ref, bg_ref, q_ref, k_ref, v_ref, o_ref, kt_ref, gt_ref):
    xb = x_ref[...]
    qk_w = ML_HEADS * ML_DQK
    q_ref[...] = (jnp.dot(xb, w_ref[:, :qk_w], preferred_element_type=F32) * (ML_DQK ** -0.5)).astype(BF16)
    k_ref[...] = jnp.dot(xb, w_ref[:, qk_w:2 * qk_w], preferred_element_type=F32).astype(BF16)
    v_ref[...] = jnp.dot(xb, w_ref[:, 2 * qk_w:2 * qk_w + D_MODEL], preferred_element_type=F32).astype(BF16)
    o_ref[...] = jax.nn.sigmoid(jnp.dot(xb, w_ref[:, 2 * qk_w + D_MODEL:], preferred_element_type=F32))
    kt = lax.dot_general(wkt_ref[...], xb, NT_DIMS, preferred_element_type=F32)
    for j in range(kt_ref.shape[1]):
        kt_ref[0, j] = kt[:, j * ML_BLOCK:(j + 1) * ML_BLOCK]
    gt_ref[0] = lax.dot_general(wgt_ref[...], xb, NT_DIMS, preferred_element_type=F32) + bg_ref[...]


def _ml_proj(x16, w, wkt, wgt, bg, bsz, seq, tm=512):
    t, d = x16.shape
    qk_w = ML_HEADS * ML_DQK
    per_seq = seq // tm
    n_gate = wgt.shape[0]
    row = lambda i: (i, 0)
    return pl.pallas_call(
        _ml_proj_kernel,
        out_shape=(jax.ShapeDtypeStruct((t, qk_w), BF16), jax.ShapeDtypeStruct((t, qk_w), BF16),
                   jax.ShapeDtypeStruct((t, D_MODEL), BF16), jax.ShapeDtypeStruct((t, D_MODEL), F32),
                   jax.ShapeDtypeStruct((bsz, seq // ML_BLOCK, qk_w, ML_BLOCK), F32),
                   jax.ShapeDtypeStruct((bsz, n_gate, seq), F32)),
        grid=(t // tm,),
        in_specs=[pl.BlockSpec((tm, d), row), _const_spec(w.shape), _const_spec(wkt.shape),
                  _const_spec(wgt.shape), _const_spec(bg.shape)],
        out_specs=(pl.BlockSpec((tm, qk_w), row), pl.BlockSpec((tm, qk_w), row),
                   pl.BlockSpec((tm, D_MODEL), row), pl.BlockSpec((tm, D_MODEL), row),
                   pl.BlockSpec((1, tm // ML_BLOCK, qk_w, ML_BLOCK), lambda i: (i // per_seq, i % per_seq, 0, 0)),
                   pl.BlockSpec((1, n_gate, tm), lambda i: (i // per_seq, 0, i % per_seq))),
        compiler_params=pltpu.CompilerParams(
            dimension_semantics=("parallel",),
            vmem_limit_bytes=_vmem_limit(2 * _nbytes((tm, d), BF16), _nbytes(w.shape, BF16), _nbytes(wkt.shape, BF16),
                                         2 * _nbytes((tm, 2 * qk_w + D_MODEL), BF16), 2 * _nbytes((tm, D_MODEL), F32),
                                         4 * _nbytes((qk_w, tm), F32), 2 * _nbytes((tm, D_MODEL), F32))),
        name="mlstm_proj",
    )(x16, w, wkt, wgt, bg)


def _log_sigmoid(x):
    return jnp.minimum(x, 0.0) - jnp.log1p(jnp.exp(-jnp.abs(x)))


def _exact_dot(x, m):
    hi = x.astype(BF16)
    r1 = x - hi.astype(F32)
    mid = r1.astype(BF16)
    lo = (r1 - mid.astype(F32)).astype(BF16)
    return (jnp.dot(hi, m, preferred_element_type=F32) + jnp.dot(mid, m, preferred_element_type=F32)
            + jnp.dot(lo, m, preferred_element_type=F32))


def _lane_cummax(x, lane_pos, reverse):
    shift = 1
    while shift < ML_CHUNK:
        if reverse:
            moved, ok = pltpu.roll(x, x.shape[1] - shift, axis=1), lane_pos < ML_CHUNK - shift
        else:
            moved, ok = pltpu.roll(x, shift, axis=1), lane_pos >= shift
        x = jnp.maximum(x, jnp.where(ok, moved, -jnp.inf))
        shift *= 2
    return x


def _mlstm_block(q, k, kt, v, a_row, ge_row, decay_row, m_row, g_col, b_col, c_state, n_state, vis, same, row_chunk,
                 reverse):
    per = ML_BLOCK // ML_CHUNK
    v_ext = jnp.concatenate([v, jnp.ones_like(v)], axis=1)
    wkt = jnp.exp(a_row - ge_row) * kt
    wkt4 = jnp.where(same, jnp.concatenate([wkt] * per, axis=0), 0.0)
    delta = jnp.dot(wkt4.astype(BF16), v_ext, preferred_element_type=F32)
    yield
    g_rep = jnp.broadcast_to(g_col, (ML_BLOCK, ML_DV))
    b_rep = jnp.broadcast_to(b_col, (ML_BLOCK, ML_DV))
    m_rep = jnp.broadcast_to(m_row[:, (per - 1) * ML_CHUNK:(per - 1) * ML_CHUNK + 1], g_rep.shape)
    for i in range(per - 2, -1, -1):
        m_rep = jnp.where(row_chunk == i, m_row[:, i * ML_CHUNK:i * ML_CHUNK + 1], m_rep)
    s_inter = jnp.exp(m_rep - g_rep)
    floor_rep = jnp.exp(-(b_rep + g_rep))
    w = jnp.where(vis, jnp.exp(a_row - jnp.concatenate([g_rep, g_rep], axis=1)), 0.0)
    qk = lax.dot_general(q, k, NT_DIMS, preferred_element_type=F32) * w
    yield
    intra = jnp.dot(qk.astype(BF16), v_ext, preferred_element_type=F32)
    starts = [None] * per
    state = jnp.concatenate([c_state, n_state], axis=1)
    for i in (range(per - 1, -1, -1) if reverse else range(per)):
        starts[i] = state
        state = decay_row[:, i * ML_CHUNK:i * ML_CHUNK + 1] * state + delta[i * ML_DQK:(i + 1) * ML_DQK]
    yield
    q4 = jnp.where(same, jnp.concatenate([q] * per, axis=1), jnp.zeros((), q.dtype))
    inter = jnp.dot(q4, jnp.concatenate(starts, axis=0).astype(BF16), preferred_element_type=F32)
    num = s_inter * inter[:, :ML_DV] + intra[:, :ML_DV]
    den = s_inter * inter[:, ML_DV:] + intra[:, ML_DV:]
    h = num / jnp.maximum(jnp.abs(den), floor_rep)
    yield h, state[:, :ML_DV], state[:, ML_DV:]


def _mlstm_kernel(q_ref, k_ref, kt_ref, v_ref, o_ref, g_ref, ng_ref, out_ref, hfw_ref, hbw_ref, row_ref, col_ref,
                  *, n_block):
    pair = pl.program_id(1)
    per = ML_BLOCK // ML_CHUNK
    t_idx = lax.broadcasted_iota(jnp.int32, (ML_BLOCK, ML_BLOCK), 0)
    s_idx = lax.broadcasted_iota(jnp.int32, (ML_BLOCK, ML_BLOCK), 1)
    same = (t_idx // ML_CHUNK) == (s_idx // ML_CHUNK)
    masks = (same & (s_idx <= t_idx), same & (s_idx >= t_idx))
    lane = lax.broadcasted_iota(jnp.int32, (1, ML_BLOCK), 1)
    lane_chunk, lane_pos = lane // ML_CHUNK, lane % ML_CHUNK
    blk_row = lax.broadcasted_iota(jnp.int32, (n_block, 1), 0)
    row_chunk = lax.broadcasted_iota(jnp.int32, (ML_BLOCK, ML_DV), 0) // ML_CHUNK

    per_query = []
    for direction in range(2):
        cum = jnp.where(masks[1 - direction], 1.0, 0.0).astype(BF16)
        for hh in range(2):
            chain = direction * 2 + hh
            head = pair * 2 + hh
            lf = _log_sigmoid(g_ref[0, (direction * 2 + 1) * ML_HEADS + head])
            b = _exact_dot(lf, cum)
            a = g_ref[0, direction * 2 * ML_HEADS + head] - b
            a_max = [jnp.max(jnp.where(lane_chunk == i, a, -jnp.inf), axis=1, keepdims=True) for i in range(per)]
            f_sum = [jnp.sum(jnp.where(lane_chunk == i, lf, 0.0), axis=1, keepdims=True) for i in range(per)]
            m = jnp.zeros((1, 1), F32)
            m_row = jnp.zeros(a.shape, F32)
            ge_row = jnp.zeros(a.shape, F32)
            n_chunk = n_block * per
            for c in (range(n_chunk - 1, -1, -1) if direction else range(n_chunk)):
                blk, i = divmod(c, per)
                g_end = jnp.maximum(m, a_max[i][blk:blk + 1])
                here = (blk_row == blk) & (lane_chunk == i)
                m_row = jnp.where(here, m, m_row)
                ge_row = jnp.where(here, g_end, ge_row)
                m = f_sum[i][blk:blk + 1] + g_end
            g_row = jnp.maximum(m_row, _lane_cummax(a, lane_pos, reverse=bool(direction)))
            row_ref[chain, 0] = a
            row_ref[chain, 1] = ge_row
            row_ref[chain, 2] = jnp.exp(m_row - ge_row)
            row_ref[chain, 3] = m_row
            per_query += [g_row, b]
    flat = [jnp.concatenate([x[blk:blk + 1] for blk in range(n_block)], axis=1) for x in per_query]
    col_ref[...] = jnp.concatenate(flat, axis=0).T

    def body(step, carry):
        chains = []
        for direction in range(2):
            blk = step if direction == 0 else n_block - 1 - step
            rows = pl.ds(pl.multiple_of(blk * ML_BLOCK, ML_BLOCK), ML_BLOCK)
            for hh in range(2):
                chain = direction * 2 + hh
                c_state, n_state = carry[chain]
                stages = _mlstm_block(
                    q_ref[rows, hh * ML_DQK:(hh + 1) * ML_DQK], k_ref[rows, hh * ML_DQK:(hh + 1) * ML_DQK],
                    kt_ref[0, blk, hh * ML_DQK:(hh + 1) * ML_DQK, :], v_ref[rows, hh * ML_DV:(hh + 1) * ML_DV],
                    *[row_ref[chain, j, pl.ds(blk, 1), :] for j in range(4)],
                    col_ref[rows, 2 * chain:2 * chain + 1], col_ref[rows, 2 * chain + 1:2 * chain + 2],
                    c_state, n_state, masks[direction], same, row_chunk, reverse=bool(direction))
                chains.append((stages, hfw_ref if direction == 0 else hbw_ref, rows, hh))
        for _ in range(3):
            for stages, _, _, _ in chains:
                next(stages)
        new_carry = []
        for stages, dst, rows, hh in chains:
            h, c_new, n_new = next(stages)
            dst[rows, hh * ML_DV:(hh + 1) * ML_DV] = h
            new_carry.append((c_new, n_new))
        return tuple(new_carry)

    init = tuple((jnp.zeros((ML_DQK, ML_DV), F32), jnp.zeros((ML_DQK, ML_DV), F32)) for _ in range(4))
    lax.fori_loop(0, n_block, body, init)

    for hh in range(2):
        cols = slice(hh * ML_DV, (hh + 1) * ML_DV)
        h = hfw_ref[:, cols] + hbw_ref[:, cols]
        ms = jnp.mean(h * h, -1, keepdims=True)
        hn = h * lax.rsqrt(ms + EPS) * ng_ref[:, cols]
        out_ref[:, cols] = (o_ref[:, cols] * hn).astype(BF16)


def _mlstm(q, k, kt, v, o, gates, norm_g, bsz, seq):
    n_pair = ML_HEADS // 2
    n_block = seq // ML_BLOCK
    pair_w = 2 * ML_DV
    return pl.pallas_call(
        functools.partial(_mlstm_kernel, n_block=n_block),
        out_shape=jax.ShapeDtypeStruct((bsz * seq, D_MODEL), BF16),
        grid=(bsz, n_pair),
        in_specs=[pl.BlockSpec((seq, 2 * ML_DQK), lambda b, p: (b, p)),
                  pl.BlockSpec((seq, 2 * ML_DQK), lambda b, p: (b, p)),
                  pl.BlockSpec((1, n_block, 2 * ML_DQK, ML_BLOCK), lambda b, p: (b, 0, p, 0)),
                  pl.BlockSpec((seq, pair_w), lambda b, p: (b, p)),
                  pl.BlockSpec((seq, pair_w), lambda b, p: (b, p)),
                  pl.BlockSpec((1,) + gates.shape[1:], lambda b, p: (b, 0, 0, 0)),
                  pl.BlockSpec((1, pair_w), lambda b, p: (0, p))],
        out_specs=pl.BlockSpec((seq, pair_w), lambda b, p: (b, p)),
        scratch_shapes=[pltpu.VMEM((seq, pair_w), F32), pltpu.VMEM((seq, pair_w), F32),
                        pltpu.VMEM((4, 4, n_block, ML_BLOCK), F32), pltpu.VMEM((seq, 8), F32)],
        compiler_params=pltpu.CompilerParams(
            dimension_semantics=("parallel", "parallel"),
            vmem_limit_bytes=_vmem_limit(4 * _nbytes((seq, 2 * ML_DQK), BF16), 2 * _nbytes((seq, 2 * ML_DQK), F32),
                                         4 * _nbytes((seq, pair_w), BF16), 2 * _nbytes((seq, pair_w), F32),
                                         2 * _nbytes(gates.shape[1:], F32), 2 * _nbytes((seq, pair_w), F32))),
        name="mlstm_scan",
    )(q, k, kt, v, o, gates, norm_g)


def _block_tail_kernel(mix_ref, x_ref, p_ref, wo_ref, g1_ref, b1_ref, w1_ref, w2_ref, wg_ref, wp_ref, g2_ref, b2_ref,
                       o32_ref, o16_ref, *, ff_chunk, parts):
    tr = x_ref.shape[0] // parts
    rows = [slice(i * tr, (i + 1) * tr) for i in range(parts)]

    def out_proj(r):
        return jnp.dot(mix_ref[r, :], wo_ref[...], preferred_element_type=F32) + DN_ALPHA * x_ref[r, :]

    def ffn_chunk(xb, c):
        sl = slice(c * ff_chunk, (c + 1) * ff_chunk)
        h = jnp.maximum(jnp.dot(xb, w1_ref[:, sl], preferred_element_type=F32), 0.0)
        return jnp.dot((h * h).astype(BF16), w2_ref[sl, :], preferred_element_type=F32)

    def finish(x1, acc, r):
        z = _layer_norm(DN_ALPHA * x1 + acc, g2_ref[...], b2_ref[...])
        o32_ref[r, :] = z
        o16_ref[r, :] = z.astype(BF16)

    y = out_proj(rows[0])
    done = None
    for i in range(parts):
        y_next = out_proj(rows[i + 1]) if i + 1 < parts else None
        x1 = _layer_norm(y, g1_ref[...], b1_ref[...])
        xb = x1.astype(BF16)
        gate = jax.nn.sigmoid(jnp.dot(xb, wg_ref[...], preferred_element_type=F32))
        acc = gate * jnp.dot(p_ref[rows[i], :].astype(BF16), wp_ref[...], preferred_element_type=F32)
        for c in range(w1_ref.shape[1] // ff_chunk):
            acc = acc + ffn_chunk(xb, c)
            if c == 0 and done is not None:
                finish(*done)
        done = (x1, acc, rows[i])
        y = y_next
    finish(*done)


def _block_tail(mix, x32, wo, layer, p, g1, b1, w1, w2, wg, wp, g2, b2, tm=512, ff_chunk=1024, parts=2):
    t, d = x32.shape
    row = lambda i: (i, 0)

    def of_layer(a):
        nd = a.ndim - 1
        return pl.BlockSpec((None,) + a.shape[1:], lambda i: (layer,) + (0,) * nd, pipeline_mode=pl.Buffered(1))

    return pl.pallas_call(
        functools.partial(_block_tail_kernel, ff_chunk=ff_chunk, parts=parts),
        out_shape=(jax.ShapeDtypeStruct((t, d), F32), jax.ShapeDtypeStruct((t, d), BF16)),
        grid=(t // tm,),
        in_specs=[pl.BlockSpec((tm, d), row), pl.BlockSpec((tm, d), row),
                  pl.BlockSpec((None, tm, p.shape[2]), lambda i: (layer, i, 0)),
                  _const_spec(wo.shape), of_layer(g1), of_layer(b1), of_layer(w1), of_layer(w2),
                  of_layer(wg), of_layer(wp), of_layer(g2), of_layer(b2)],
        out_specs=(pl.BlockSpec((tm, d), row), pl.BlockSpec((tm, d), row)),
        compiler_params=pltpu.CompilerParams(
            dimension_semantics=("parallel",),
            vmem_limit_bytes=_vmem_limit(4 * _nbytes((tm, d), BF16), 4 * _nbytes((tm, d), F32),
                                         2 * _nbytes((tm, p.shape[2]), F32), _nbytes(wo.shape, BF16),
                                         _nbytes(w1.shape[1:], BF16), _nbytes(w2.shape[1:], BF16),
                                         _nbytes(wg.shape[1:], BF16), _nbytes(wp.shape[1:], BF16),
                                         2 * _nbytes((tm, ff_chunk), F32), 4 * _nbytes((tm, d), F32))),
        name="block_tail",
    )(mix, x32, p, wo, g1, b1, w1, w2, wg, wp, g2, b2)


def kernel(x, p, na_w_qkv, na_rpb, na_w_o, gq_w_qkv, gq_q_norm, gq_k_norm, gq_w_o, ml_w_in, ml_b_gates, ml_norm_g,
           ml_w_o, ln1_g, ln1_b, w_ff1, w_ff2, ln2_g, ln2_b, w_ple_gate, w_ple_proj):
    bsz, seq, d = x.shape
    assert d == D_MODEL and seq % (NA_KH * GRID_W) == 0 and p.shape == (DEPTH, bsz, seq, D_PLE)
    t = bsz * seq
    x32 = x.reshape(t, d)
    per_layer = (p.reshape(DEPTH, t, D_PLE), ln1_g[:, None, :], ln1_b[:, None, :], w_ff1.astype(BF16),
                 w_ff2.astype(BF16), w_ple_gate.astype(BF16), w_ple_proj.astype(BF16), ln2_g[:, None, :],
                 ln2_b[:, None, :])
    x16 = None
    qk_w = ML_HEADS * ML_DQK
    for i in range(DEPTH):
        kind, j = i % 3, i // 3
        if kind == 0:
            qkv = _proj(x32 if x16 is None else x16, na_w_qkv[j].astype(BF16), n_chunk=D_MODEL,
                        first_scale=HEAD_DIM ** -0.5 * LOG2E)
            mix = _na_attention(qkv, _na_bias_table(na_rpb[j]), bsz, seq)
            w_o = na_w_o[j]
        elif kind == 1:
            cos4, sin4 = _rope_tables(seq)
            w_qk, w_vt = _gqa_weight_layout(gq_w_qkv[j])
            qk, vt = _gqa_proj(x16, w_qk.astype(BF16), w_vt.astype(BF16), cos4, sin4, _gqa_gain_layout(gq_q_norm[j]),
                               _gqa_gain_layout(gq_k_norm[j]), _group_sum_matrix(), bsz, seq)
            mix = _gqa_attention(qk, vt, bsz, seq)
            w_o = gq_w_o[j]
        else:
            w_in = ml_w_in[j]
            n_main = 2 * qk_w + 2 * D_MODEL
            wkt = w_in[:, qk_w:2 * qk_w].T.astype(BF16)
            wgt = w_in[:, n_main:].T.astype(BF16)
            q, k, v, o, kt, gt = _ml_proj(x16, w_in[:, :n_main].astype(BF16), wkt, wgt, ml_b_gates[j][:, None], bsz, seq)
            gates = gt.reshape(bsz, gt.shape[1], seq // ML_BLOCK, ML_BLOCK)
            mix = _mlstm(q, k, kt, v, o, gates, ml_norm_g[j][None, :], bsz, seq)
            w_o = ml_w_o[j]
        x32, x16 = _block_tail(mix, x32, w_o.astype(BF16), i, *per_layer)
    return x32.reshape(bsz, seq, d)
```

```python
import functools

import jax
import jax.numpy as jnp
import numpy as np
from jax import lax
from jax.experimental import pallas as pl
from jax.experimental.pallas import tpu as pltpu

F32 = jnp.float32
BF16 = jnp.bfloat16

D_MODEL = 1024
DEPTH = 4
GRID_W = 64
HEAD_DIM = 64
D_PLE = 256
NA_HEADS = 16
NA_KH = 8
NA_KW = 16
GQA_KV_HEADS = 4
GQA_GROUP = 4
ROPE_THETA = 10000.0
ML_HEADS = 8
ML_DV = 128
ML_DQK = 64
ML_CHUNK = 64
ML_BLOCK = 256
DN_ALPHA = (2 * DEPTH) ** 0.25
EPS = 1e-6
LOG2E = 1.4426950408889634

V7X_VMEM_BYTES = 64 * 1024 * 1024
V7X_LANES = 128
SLAB = 2 * V7X_LANES

NT_DIMS = (((1,), (1,)), ((), ()))


def _vmem_limit(*byte_counts):
    est = int(sum(byte_counts) * 1.5) + (4 << 20)
    return min(est, V7X_VMEM_BYTES - (6 << 20))


def _nbytes(shape, dtype):
    return int(np.prod(shape)) * jnp.dtype(dtype).itemsize


def _const_spec(shape):
    nd = len(shape)
    return pl.BlockSpec(shape, lambda *_: (0,) * nd, pipeline_mode=pl.Buffered(1))


def _layer_norm(y, g, b):
    mu = jnp.mean(y, -1, keepdims=True)
    yc = y - mu
    var = jnp.mean(yc * yc, -1, keepdims=True)
    return yc * lax.rsqrt(var + EPS) * g + b


def _proj_kernel(x_ref, w_ref, o_ref, *, n_chunk, first_scale):
    xb = x_ref[...].astype(BF16)
    for j in range(o_ref.shape[1] // n_chunk):
        sl = slice(j * n_chunk, (j + 1) * n_chunk)
        y = jnp.dot(xb, w_ref[:, sl], preferred_element_type=F32)
        if j == 0 and first_scale != 1.0:
            y = y * first_scale
        o_ref[:, sl] = y.astype(BF16)


def _proj(x, w, tm=1024, n_chunk=1024, first_scale=1.0):
    t, d = x.shape
    n = w.shape[1]
    return pl.pallas_call(
        functools.partial(_proj_kernel, n_chunk=n_chunk, first_scale=first_scale),
        out_shape=jax.ShapeDtypeStruct((t, n), BF16),
        grid=(t // tm,),
        in_specs=[pl.BlockSpec((tm, d), lambda i: (i, 0)), _const_spec((d, n))],
        out_specs=pl.BlockSpec((tm, n), lambda i: (i, 0)),
        compiler_params=pltpu.CompilerParams(
            dimension_semantics=("parallel",),
            vmem_limit_bytes=_vmem_limit(2 * _nbytes((tm, d), x.dtype), _nbytes((d, n), BF16),
                                         2 * _nbytes((tm, n), BF16), _nbytes((tm, n_chunk), F32))),
        name="proj_plain",
    )(x, w)


def _na_kernel(q_ref, k_ref, v_ref, bias_ref, o_ref, s_ref, p_ref, *, rows, group):
    lane_head = lax.broadcasted_iota(jnp.int32, (1, SLAB), 1) // HEAD_DIM
    win = NA_KH * GRID_W

    def window(r):
        r = jnp.clip(r, 0, rows - 1)
        r0 = jnp.clip(r - NA_KH // 2, 0, rows - NA_KH)
        return pl.multiple_of(r * GRID_W, GRID_W), pl.multiple_of(r0 * GRID_W, GRID_W), r - r0

    def scores(r, slot):
        q0, k0, delta = window(r)
        q = q_ref[pl.ds(q0, GRID_W), :]
        qs = jnp.concatenate([jnp.where(lane_head == h, q, jnp.zeros_like(q)) for h in range(4)], axis=0)
        s = lax.dot_general(qs, k_ref[pl.ds(k0, win), :], NT_DIMS, preferred_element_type=F32)
        s_ref[slot] = s + bias_ref[0, delta]

    def softmax(slot):
        s = s_ref[slot]
        e = jnp.exp2(s - jnp.max(s, -1, keepdims=True))
        p_ref[slot] = (e * (1.0 / jnp.sum(e, -1, keepdims=True))).astype(BF16)

    def weighted_values(r, slot):
        q0, k0, _ = window(r)
        pv = jnp.dot(p_ref[slot], v_ref[pl.ds(k0, win), :], preferred_element_type=F32)
        acc = jnp.zeros((GRID_W, SLAB), F32)
        for h in range(4):
            acc = jnp.where(lane_head == h, pv[h * GRID_W:(h + 1) * GRID_W], acc)
        o_ref[pl.ds(q0, GRID_W), :] = acc.astype(BF16)

    def step(g, bank):
        for t in range(group):
            weighted_values((g - 1) * group + t, (1 - bank) * group + t)
        for t in range(group):
            scores((g + 1) * group + t, (1 - bank) * group + t)
        for t in range(group):
            softmax(bank * group + t)

    for t in range(group):
        scores(t, t)
        p_ref[group + t] = jnp.zeros(p_ref.shape[1:], BF16)

    def body(j, carry):
        step(2 * j, 0)
        step(2 * j + 1, 1)
        return carry

    n_step = rows // group
    lax.fori_loop(0, n_step // 2, body, 0)
    for t in range(group):
        weighted_values(rows - group + t, ((n_step - 1) % 2) * group + t)


def _na_bias_table(rpb):
    col = np.arange(GRID_W)
    c0 = np.clip(col - NA_KW // 2, 0, GRID_W - NA_KW)
    col_in = (col[None, :] >= c0[:, None]) & (col[None, :] < c0[:, None] + NA_KW)
    dc = np.clip(col[None, :] - col[:, None], 1 - NA_KW, NA_KW - 1) + NA_KW - 1
    rpb = rpb.astype(F32)
    by_col = jnp.zeros(rpb.shape[:2] + dc.shape, F32)
    for c in range(2 * NA_KW - 1):
        by_col = jnp.where(dc[None, None] == c, rpb[:, :, c][:, :, None, None], by_col)
    by_col = jnp.where(col_in[None, None], by_col * LOG2E, -jnp.inf)
    per_delta = [by_col[:, NA_KH - 1 - dl:2 * NA_KH - 1 - dl].transpose(0, 2, 1, 3)
                 .reshape(NA_HEADS, GRID_W, NA_KH * GRID_W) for dl in range(NA_KH)]
    b = jnp.stack(per_delta, axis=1).reshape(NA_HEADS // 4, 4, NA_KH, GRID_W, NA_KH * GRID_W)
    return b.transpose(0, 2, 1, 3, 4).reshape(NA_HEADS // 4, NA_KH, 4 * GRID_W, NA_KH * GRID_W)


def _na_attention(qkv, bias, bsz, seq, group=2):
    n_slab = D_MODEL // SLAB
    rows = seq // GRID_W
    assert rows % (2 * group) == 0
    blk = (seq, SLAB)
    tile = (4 * GRID_W, NA_KH * GRID_W)
    return pl.pallas_call(
        functools.partial(_na_kernel, rows=rows, group=group),
        out_shape=jax.ShapeDtypeStruct((bsz * seq, D_MODEL), BF16),
        grid=(n_slab, bsz),
        in_specs=[pl.BlockSpec(blk, lambda s, b: (b, s)),
                  pl.BlockSpec(blk, lambda s, b: (b, n_slab + s)),
                  pl.BlockSpec(blk, lambda s, b: (b, 2 * n_slab + s)),
                  pl.BlockSpec((1,) + bias.shape[1:], lambda s, b: (s, 0, 0, 0))],
        out_specs=pl.BlockSpec(blk, lambda s, b: (b, s)),
        scratch_shapes=[pltpu.VMEM((2 * group,) + tile, F32), pltpu.VMEM((2 * group,) + tile, BF16)],
        compiler_params=pltpu.CompilerParams(
            dimension_semantics=("parallel", "parallel"),
            vmem_limit_bytes=_vmem_limit(8 * _nbytes(blk, BF16), 2 * _nbytes(bias.shape[1:], F32),
                                         (3 * group + 4) * _nbytes(tile, F32))),
        name="na_attention",
    )(qkv, qkv, qkv, bias)


def _gqa_proj_kernel(x_ref, w_ref, wvt_ref, cos_ref, sin_ref, gq_ref, gk_ref, ones_ref, o_ref, vt_ref):
    xb = x_ref[...]
    cos = cos_ref[...]
    sin = sin_ref[...]
    ones = ones_ref[...]
    n_norm = 2 * D_MODEL // (2 * SLAB)

    def project(pair):
        return jnp.dot(xb, w_ref[:, pair * 2 * SLAB:(pair + 1) * 2 * SLAB], preferred_element_type=F32)

    z_next = project(0)
    for pair in range(n_norm):
        z = z_next
        if pair + 1 < n_norm:
            z_next = project(pair + 1)
        else:
            vt_ref[...] = lax.dot_general(wvt_ref[...], xb, NT_DIMS, preferred_element_type=F32).astype(BF16)
        halves = [(z[:, j * SLAB:j * SLAB + V7X_LANES], z[:, j * SLAB + V7X_LANES:(j + 1) * SLAB]) for j in range(2)]
        ss = jnp.concatenate([a * a + b * b for a, b in halves], axis=1)
        hi = ss.astype(BF16)
        lo = (ss - hi.astype(F32)).astype(BF16)
        ms = (jnp.dot(hi, ones, preferred_element_type=F32)
              + jnp.dot(lo, ones, preferred_element_type=F32)) * (1.0 / HEAD_DIM)
        rs = lax.rsqrt(ms + EPS)
        is_q = pair < n_norm // 2
        g_ref = gq_ref if is_q else gk_ref
        for j, (a, b) in enumerate(halves):
            r = rs[:, j * V7X_LANES:(j + 1) * V7X_LANES]
            an = a * r * g_ref[:, :V7X_LANES]
            bn = b * r * g_ref[:, V7X_LANES:]
            oa = an * cos - bn * sin
            ob = an * sin + bn * cos
            if is_q:
                oa = oa * (HEAD_DIM ** -0.5 * LOG2E)
                ob = ob * (HEAD_DIM ** -0.5 * LOG2E)
            c0 = (pair * 2 + j) * SLAB
            o_ref[:, c0:c0 + V7X_LANES] = oa.astype(BF16)
            o_ref[:, c0 + V7X_LANES:c0 + SLAB] = ob.astype(BF16)


def _gqa_weight_layout(w):
    d = w.shape[0]
    kvd = GQA_KV_HEADS * HEAD_DIM
    half = HEAD_DIM // 2
    wq = w[:, :D_MODEL].reshape(d, GQA_KV_HEADS, GQA_GROUP, half, 2).transpose(0, 1, 4, 2, 3)
    wk = w[:, D_MODEL:D_MODEL + kvd].reshape(d, GQA_KV_HEADS, half, 2).transpose(0, 1, 3, 2)
    wk = jnp.broadcast_to(wk[:, :, :, None, :], (d, GQA_KV_HEADS, 2, GQA_GROUP, half))
    return jnp.concatenate([wq.reshape(d, D_MODEL), wk.reshape(d, D_MODEL)], axis=1), w[:, D_MODEL + kvd:].T


def _gqa_gain_layout(g):
    half = HEAD_DIM // 2
    return jnp.broadcast_to(g.reshape(half, 2).T[:, None, :], (2, GQA_GROUP, half)).reshape(1, SLAB)


def _group_sum_matrix():
    blk = np.arange(SLAB) // (HEAD_DIM // 2)
    return jnp.asarray(blk[:, None] == blk[None, :], BF16)


def _rope_tables(seq):
    t = jnp.arange(seq)
    row = (t // GRID_W).astype(F32)
    col = (t % GRID_W).astype(F32)
    n_pairs = HEAD_DIM // 4
    inv = ROPE_THETA ** (-jnp.arange(n_pairs, dtype=F32) / n_pairs)
    ang = jnp.concatenate([row[:, None] * inv, col[:, None] * inv], -1)
    return jnp.tile(jnp.cos(ang), (1, GQA_GROUP)), jnp.tile(jnp.sin(ang), (1, GQA_GROUP))


def _gqa_proj(x16, w, wvt, cos4, sin4, gq, gk, ones, bsz, seq, tm=512):
    t, d = x16.shape
    n = w.shape[1]
    per_seq = seq // tm
    return pl.pallas_call(
        _gqa_proj_kernel,
        out_shape=(jax.ShapeDtypeStruct((t, n), BF16), jax.ShapeDtypeStruct((bsz, wvt.shape[0], seq), BF16)),
        grid=(t // tm,),
        in_specs=[pl.BlockSpec((tm, d), lambda i: (i, 0)), _const_spec((d, n)), _const_spec(wvt.shape),
                  pl.BlockSpec((tm, V7X_LANES), lambda i: (i % per_seq, 0)),
                  pl.BlockSpec((tm, V7X_LANES), lambda i: (i % per_seq, 0)),
                  _const_spec((1, SLAB)), _const_spec((1, SLAB)), _const_spec((SLAB, SLAB))],
        out_specs=(pl.BlockSpec((tm, n), lambda i: (i, 0)),
                   pl.BlockSpec((None, wvt.shape[0], tm), lambda i: (i // per_seq, 0, i % per_seq))),
        compiler_params=pltpu.CompilerParams(
            dimension_semantics=("parallel",),
            vmem_limit_bytes=_vmem_limit(2 * _nbytes((tm, d), BF16), _nbytes((d, n), BF16), _nbytes(wvt.shape, BF16),
                                         2 * _nbytes((tm, n), BF16), 6 * _nbytes((tm, 2 * SLAB), F32))),
        name="gqa_proj",
    )(x16, w, wvt, cos4, sin4, gq, gk, ones)


def _gqa_attn_kernel(q_ref, k_ref, vt_ref, o_ref, *, q_block):
    lane = lax.broadcasted_iota(jnp.int32, (1, SLAB), 1)
    q = q_ref[...]
    q_head = (lane % V7X_LANES) // (HEAD_DIM // 2)

    def scores(h):
        qm = jnp.where(q_head == h, q, jnp.zeros_like(q))
        return lax.dot_general(k_ref[...], qm, NT_DIMS, preferred_element_type=F32)

    def softmax(s):
        e = jnp.exp2(s - jnp.max(s, 0, keepdims=True))
        return e.astype(BF16), 1.0 / jnp.sum(e, 0, keepdims=True)

    def weighted_values(e, inv_l):
        return jnp.dot(vt_ref[...], e, preferred_element_type=F32) * inv_l

    n_qb = q_ref.shape[0] // q_block
    outs = [[None] * GQA_GROUP for _ in range(n_qb)]
    s_next = scores(0)
    pending = []
    for h in range(GQA_GROUP):
        s = s_next
        if h + 1 < GQA_GROUP:
            s_next = scores(h + 1)
        current = []
        for qb in range(n_qb):
            current.append(softmax(s[:, qb * q_block:(qb + 1) * q_block]))
            if pending:
                outs[qb][h - 1] = weighted_values(*pending[qb])
        pending = current
    for qb in range(n_qb):
        outs[qb][GQA_GROUP - 1] = weighted_values(*pending[qb])
        o_ref[qb * q_block:(qb + 1) * q_block, :] = jnp.concatenate(outs[qb], axis=0).T.astype(BF16)


def _gqa_attention(qk, vt, bsz, seq, tq=1024, q_block=256):
    n_slab = D_MODEL // SLAB
    nq = seq // tq
    return pl.pallas_call(
        functools.partial(_gqa_attn_kernel, q_block=q_block),
        out_shape=jax.ShapeDtypeStruct((bsz * seq, D_MODEL), BF16),
        grid=(bsz, n_slab, nq),
        in_specs=[pl.BlockSpec((tq, SLAB), lambda b, g, i: (b * nq + i, g)),
                  pl.BlockSpec((seq, SLAB), lambda b, g, i: (b, n_slab + g)),
                  pl.BlockSpec((None, HEAD_DIM, seq), lambda b, g, i: (b, g, 0))],
        out_specs=pl.BlockSpec((tq, SLAB), lambda b, g, i: (b * nq + i, g)),
        compiler_params=pltpu.CompilerParams(
            dimension_semantics=("parallel", "parallel", "parallel"),
            vmem_limit_bytes=_vmem_limit(4 * _nbytes((tq, SLAB), BF16), 4 * _nbytes((seq, SLAB), BF16),
                                         2 * _nbytes((HEAD_DIM, seq), BF16), 3 * _nbytes((tq, seq), F32))),
        name="gqa_attention",
    )(qk, qk, vt)


def _ml_proj_kernel(x_ref, w_ref, wkt_ref, wgt_ref, bg_ref, q_ref, k_ref, v_ref, o_ref, kt_ref, gt_ref):
    xb = x_ref[...]
    qk_w = ML_HEADS * ML_DQK
    q_ref[...] = (jnp.dot(xb, w_ref[:, :qk_w], preferred_element_type=F32) * (ML_DQK ** -0.5)).astype(BF16)
    k_ref[...] = jnp.dot(xb, w_ref[:, qk_w:2 * qk_w], preferred_element_type=F32).astype(BF16)
    v_ref[...] = jnp.dot(xb, w_ref[:, 2 * qk_w:2 * qk_w + D_MODEL], preferred_element_type=F32).astype(BF16)
    o_ref[...] = jax.nn.sigmoid(jnp.dot(xb, w_ref[:, 2 * qk_w + D_MODEL:], preferred_element_type=F32))
    kt = lax.dot_general(wkt_ref[...], xb, NT_DIMS, preferred_element_type=F32)
    for j in range(kt_ref.shape[1]):
        kt_ref[0, j] = kt[:, j * ML_BLOCK:(j + 1) * ML_BLOCK]
    gt_ref[0] = lax.dot_general(wgt_ref[...], xb, NT_DIMS, preferred_element_type=F32) + bg_ref[...]


def _ml_proj(x16, w, wkt, wgt, bg, bsz, seq, tm=1024):
    t, d = x16.shape
    qk_w = ML_HEADS * ML_DQK
    per_seq = seq // tm
    n_gate = wgt.shape[0]
    row = lambda i: (i, 0)
    return pl.pallas_call(
        _ml_proj_kernel,
        out_shape=(jax.ShapeDtypeStruct((t, qk_w), BF16), jax.ShapeDtypeStruct((t, qk_w), BF16),
                   jax.ShapeDtypeStruct((t, D_MODEL), BF16), jax.ShapeDtypeStruct((t, D_MODEL), F32),
                   jax.ShapeDtypeStruct((bsz, seq // ML_BLOCK, qk_w, ML_BLOCK), F32),
                   jax.ShapeDtypeStruct((bsz, n_gate, seq), F32)),
        grid=(t // tm,),
        in_specs=[pl.BlockSpec((tm, d), row), _const_spec(w.shape), _const_spec(wkt.shape),
                  _const_spec(wgt.shape), _const_spec(bg.shape)],
        out_specs=(pl.BlockSpec((tm, qk_w), row), pl.BlockSpec((tm, qk_w), row),
                   pl.BlockSpec((tm, D_MODEL), row), pl.BlockSpec((tm, D_MODEL), row),
                   pl.BlockSpec((1, tm // ML_BLOCK, qk_w, ML_BLOCK), lambda i: (i // per_seq, i % per_seq, 0, 0)),
                   pl.BlockSpec((1, n_gate, tm), lambda i: (i // per_seq, 0, i % per_seq))),
        compiler_params=pltpu.CompilerParams(
            dimension_semantics=("parallel",),
            vmem_limit_bytes=_vmem_limit(2 * _nbytes((tm, d), BF16), _nbytes(w.shape, BF16), _nbytes(wkt.shape, BF16),
                                         2 * _nbytes((tm, 2 * qk_w + D_MODEL), BF16), 2 * _nbytes((tm, D_MODEL), F32),
                                         4 * _nbytes((qk_w, tm), F32), 2 * _nbytes((tm, D_MODEL), F32))),
        name="mlstm_proj",
    )(x16, w, wkt, wgt, bg)


def _log_sigmoid(x):
    return jnp.minimum(x, 0.0) - jnp.log1p(jnp.exp(-jnp.abs(x)))


def _exact_dot(x, m):
    hi = x.astype(BF16)
    r1 = x - hi.astype(F32)
    mid = r1.astype(BF16)
    lo = (r1 - mid.astype(F32)).astype(BF16)
    return (jnp.dot(hi, m, preferred_element_type=F32) + jnp.dot(mid, m, preferred_element_type=F32)
            + jnp.dot(lo, m, preferred_element_type=F32))


def _lane_cummax(x, lane_pos, reverse):
    shift = 1
    while shift < ML_CHUNK:
        if reverse:
            moved, ok = pltpu.roll(x, x.shape[1] - shift, axis=1), lane_pos < ML_CHUNK - shift
        else:
            moved, ok = pltpu.roll(x, shift, axis=1), lane_pos >= shift
        x = jnp.maximum(x, jnp.where(ok, moved, -jnp.inf))
        shift *= 2
    return x


def _mlstm_block(q, k, kt, v, a_row, ge_row, decay_row, m_row, g_col, b_col, c_state, n_state, vis, same, row_chunk,
                 reverse):
    per = ML_BLOCK // ML_CHUNK
    v_ext = jnp.concatenate([v, jnp.ones_like(v)], axis=1)
    wkt = jnp.exp(a_row - ge_row) * kt
    wkt4 = jnp.where(same, jnp.concatenate([wkt] * per, axis=0), 0.0)
    delta = jnp.dot(wkt4.astype(BF16), v_ext, preferred_element_type=F32)
    yield
    g_rep = jnp.broadcast_to(g_col, (ML_BLOCK, ML_DV))
    b_rep = jnp.broadcast_to(b_col, (ML_BLOCK, ML_DV))
    m_rep = jnp.broadcast_to(m_row[:, (per - 1) * ML_CHUNK:(per - 1) * ML_CHUNK + 1], g_rep.shape)
    for i in range(per - 2, -1, -1):
        m_rep = jnp.where(row_chunk == i, m_row[:, i * ML_CHUNK:i * ML_CHUNK + 1], m_rep)
    s_inter = jnp.exp(m_rep - g_rep)
    floor_rep = jnp.exp(-(b_rep + g_rep))
    w = jnp.where(vis, jnp.exp(a_row - jnp.concatenate([g_rep, g_rep], axis=1)), 0.0)
    qk = lax.dot_general(q, k, NT_DIMS, preferred_element_type=F32) * w
    yield
    intra = jnp.dot(qk.astype(BF16), v_ext, preferred_element_type=F32)
    starts = [None] * per
    state = jnp.concatenate([c_state, n_state], axis=1)
    for i in (range(per - 1, -1, -1) if reverse else range(per)):
        starts[i] = state
        state = decay_row[:, i * ML_CHUNK:i * ML_CHUNK + 1] * state + delta[i * ML_DQK:(i + 1) * ML_DQK]
    yield
    q4 = jnp.where(same, jnp.concatenate([q] * per, axis=1), jnp.zeros((), q.dtype))
    inter = jnp.dot(q4, jnp.concatenate(starts, axis=0).astype(BF16), preferred_element_type=F32)
    num = s_inter * inter[:, :ML_DV] + intra[:, :ML_DV]
    den = s_inter * inter[:, ML_DV:] + intra[:, ML_DV:]
    h = num / jnp.maximum(jnp.abs(den), floor_rep)
    yield h, state[:, :ML_DV], state[:, ML_DV:]


def _mlstm_kernel(q_ref, k_ref, kt_ref, v_ref, o_ref, g_ref, ng_ref, out_ref, hfw_ref, hbw_ref, row_ref, col_ref,
                  *, n_block):
    pair = pl.program_id(1)
    per = ML_BLOCK // ML_CHUNK
    t_idx = lax.broadcasted_iota(jnp.int32, (ML_BLOCK, ML_BLOCK), 0)
    s_idx = lax.broadcasted_iota(jnp.int32, (ML_BLOCK, ML_BLOCK), 1)
    same = (t_idx // ML_CHUNK) == (s_idx // ML_CHUNK)
    masks = (same & (s_idx <= t_idx), same & (s_idx >= t_idx))
    lane = lax.broadcasted_iota(jnp.int32, (1, ML_BLOCK), 1)
    lane_chunk, lane_pos = lane // ML_CHUNK, lane % ML_CHUNK
    blk_row = lax.broadcasted_iota(jnp.int32, (n_block, 1), 0)
    row_chunk = lax.broadcasted_iota(jnp.int32, (ML_BLOCK, ML_DV), 0) // ML_CHUNK

    per_query = []
    for direction in range(2):
        cum = jnp.where(masks[1 - direction], 1.0, 0.0).astype(BF16)
        for hh in range(2):
            chain = direction * 2 + hh
            head = pair * 2 + hh
            lf = _log_sigmoid(g_ref[0, (direction * 2 + 1) * ML_HEADS + head])
            b = _exact_dot(lf, cum)
            a = g_ref[0, direction * 2 * ML_HEADS + head] - b
            a_max = [jnp.max(jnp.where(lane_chunk == i, a, -jnp.inf), axis=1, keepdims=True) for i in range(per)]
            f_sum = [jnp.sum(jnp.where(lane_chunk == i, lf, 0.0), axis=1, keepdims=True) for i in range(per)]
            m = jnp.zeros((1, 1), F32)
            m_row = jnp.zeros(a.shape, F32)
            ge_row = jnp.zeros(a.shape, F32)
            n_chunk = n_block * per
            for c in (range(n_chunk - 1, -1, -1) if direction else range(n_chunk)):
                blk, i = divmod(c, per)
                g_end = jnp.maximum(m, a_max[i][blk:blk + 1])
                here = (blk_row == blk) & (lane_chunk == i)
                m_row = jnp.where(here, m, m_row)
                ge_row = jnp.where(here, g_end, ge_row)
                m = f_sum[i][blk:blk + 1] + g_end
            g_row = jnp.maximum(m_row, _lane_cummax(a, lane_pos, reverse=bool(direction)))
            row_ref[chain, 0] = a
            row_ref[chain, 1] = ge_row
            row_ref[chain, 2] = jnp.exp(m_row - ge_row)
            row_ref[chain, 3] = m_row
            per_query += [g_row, b]
    flat = [jnp.concatenate([x[blk:blk + 1] for blk in range(n_block)], axis=1) for x in per_query]
    col_ref[...] = jnp.concatenate(flat, axis=0).T

    def body(step, carry):
        chains = []
        for direction in range(2):
            blk = step if direction == 0 else n_block - 1 - step
            rows = pl.ds(pl.multiple_of(blk * ML_BLOCK, ML_BLOCK), ML_BLOCK)
            for hh in range(2):
                chain = direction * 2 + hh
                c_state, n_state = carry[chain]
                stages = _mlstm_block(
                    q_ref[rows, hh * ML_DQK:(hh + 1) * ML_DQK], k_ref[rows, hh * ML_DQK:(hh + 1) * ML_DQK],
                    kt_ref[0, blk, hh * ML_DQK:(hh + 1) * ML_DQK, :], v_ref[rows, hh * ML_DV:(hh + 1) * ML_DV],
                    *[row_ref[chain, j, pl.ds(blk, 1), :] for j in range(4)],
                    col_ref[rows, 2 * chain:2 * chain + 1], col_ref[rows, 2 * chain + 1:2 * chain + 2],
                    c_state, n_state, masks[direction], same, row_chunk, reverse=bool(direction))
                chains.append((stages, hfw_ref if direction == 0 else hbw_ref, rows, hh))
        for _ in range(3):
            for stages, _, _, _ in chains:
                next(stages)
        new_carry = []
        for stages, dst, rows, hh in chains:
            h, c_new, n_new = next(stages)
            dst[rows, hh * ML_DV:(hh + 1) * ML_DV] = h
            new_carry.append((c_new, n_new))
        return tuple(new_carry)

    init = tuple((jnp.zeros((ML_DQK, ML_DV), F32), jnp.zeros((ML_DQK, ML_DV), F32)) for _ in range(4))
    lax.fori_loop(0, n_block, body, init)

    for hh in range(2):
        cols = slice(hh * ML_DV, (hh + 1) * ML_DV)
        h = hfw_ref[:, cols] + hbw_ref[:, cols]
        ms = jnp.mean(h * h, -1, keepdims=True)
        hn = h * lax.rsqrt(ms + EPS) * ng_ref[:, cols]
        out_ref[:, cols] = (o_ref[:, cols] * hn).astype(BF16)


def _mlstm(q, k, kt, v, o, gates, norm_g, bsz, seq):
    n_pair = ML_HEADS // 2
    n_block = seq // ML_BLOCK
    pair_w = 2 * ML_DV
    return pl.pallas_call(
        functools.partial(_mlstm_kernel, n_block=n_block),
        out_shape=jax.ShapeDtypeStruct((bsz * seq, D_MODEL), BF16),
        grid=(bsz, n_pair),
        in_specs=[pl.BlockSpec((seq, 2 * ML_DQK), lambda b, p: (b, p)),
                  pl.BlockSpec((seq, 2 * ML_DQK), lambda b, p: (b, p)),
                  pl.BlockSpec((1, n_block, 2 * ML_DQK, ML_BLOCK), lambda b, p: (b, 0, p, 0)),
                  pl.BlockSpec((seq, pair_w), lambda b, p: (b, p)),
                  pl.BlockSpec((seq, pair_w), lambda b, p: (b, p)),
                  pl.BlockSpec((1,) + gates.shape[1:], lambda b, p: (b, 0, 0, 0)),
                  pl.BlockSpec((1, pair_w), lambda b, p: (0, p))],
        out_specs=pl.BlockSpec((seq, pair_w), lambda b, p: (b, p)),
        scratch_shapes=[pltpu.VMEM((seq, pair_w), F32), pltpu.VMEM((seq, pair_w), F32),
                        pltpu.VMEM((4, 4, n_block, ML_BLOCK), F32), pltpu.VMEM((seq, 8), F32)],
        compiler_params=pltpu.CompilerParams(
            dimension_semantics=("parallel", "parallel"),
            vmem_limit_bytes=_vmem_limit(4 * _nbytes((seq, 2 * ML_DQK), BF16), 2 * _nbytes((seq, 2 * ML_DQK), F32),
                                         4 * _nbytes((seq, pair_w), BF16), 2 * _nbytes((seq, pair_w), F32),
                                         2 * _nbytes(gates.shape[1:], F32), 2 * _nbytes((seq, pair_w), F32))),
        name="mlstm_scan",
    )(q, k, kt, v, o, gates, norm_g)


def _block_tail_kernel(mix_ref, x_ref, p_ref, wo_ref, g1_ref, b1_ref, w1_ref, w2_ref, wg_ref, wp_ref, g2_ref, b2_ref,
                       o32_ref, o16_ref, *, ff_chunk, parts):
    tr = x_ref.shape[0] // parts
    rows = [slice(i * tr, (i + 1) * tr) for i in range(parts)]

    def out_proj(r):
        return jnp.dot(mix_ref[r, :], wo_ref[...], preferred_element_type=F32) + DN_ALPHA * x_ref[r, :]

    def ffn_chunk(xb, c):
        sl = slice(c * ff_chunk, (c + 1) * ff_chunk)
        h = jnp.maximum(jnp.dot(xb, w1_ref[:, sl], preferred_element_type=F32), 0.0)
        return jnp.dot((h * h).astype(BF16), w2_ref[sl, :], preferred_element_type=F32)

    def finish(x1, acc, r):
        z = _layer_norm(DN_ALPHA * x1 + acc, g2_ref[...], b2_ref[...])
        o32_ref[r, :] = z
        o16_ref[r, :] = z.astype(BF16)

    y = out_proj(rows[0])
    done = None
    for i in range(parts):
        y_next = out_proj(rows[i + 1]) if i + 1 < parts else None
        x1 = _layer_norm(y, g1_ref[...], b1_ref[...])
        xb = x1.astype(BF16)
        gate = jax.nn.sigmoid(jnp.dot(xb, wg_ref[...], preferred_element_type=F32))
        acc = gate * jnp.dot(p_ref[rows[i], :].astype(BF16), wp_ref[...], preferred_element_type=F32)
        for c in range(w1_ref.shape[1] // ff_chunk):
            acc = acc + ffn_chunk(xb, c)
            if c == 0 and done is not None:
                finish(*done)
        done = (x1, acc, rows[i])
        y = y_next
    finish(*done)


def _block_tail(mix, x32, wo, layer, p, g1, b1, w1, w2, wg, wp, g2, b2, tm=512, ff_chunk=1024, parts=2):
    t, d = x32.shape
    row = lambda i: (i, 0)

    def of_layer(a):
        nd = a.ndim - 1
        return pl.BlockSpec((None,) + a.shape[1:], lambda i: (layer,) + (0,) * nd, pipeline_mode=pl.Buffered(1))

    return pl.pallas_call(
        functools.partial(_block_tail_kernel, ff_chunk=ff_chunk, parts=parts),
        out_shape=(jax.ShapeDtypeStruct((t, d), F32), jax.ShapeDtypeStruct((t, d), BF16)),
        grid=(t // tm,),
        in_specs=[pl.BlockSpec((tm, d), row), pl.BlockSpec((tm, d), row),
                  pl.BlockSpec((None, tm, p.shape[2]), lambda i: (layer, i, 0)),
                  _const_spec(wo.shape), of_layer(g1), of_layer(b1), of_layer(w1), of_layer(w2),
                  of_layer(wg), of_layer(wp), of_layer(g2), of_layer(b2)],
        out_specs=(pl.BlockSpec((tm, d), row), pl.BlockSpec((tm, d), row)),
        compiler_params=pltpu.CompilerParams(
            dimension_semantics=("parallel",),
            vmem_limit_bytes=_vmem_limit(4 * _nbytes((tm, d), BF16), 4 * _nbytes((tm, d), F32),
                                         2 * _nbytes((tm, p.shape[2]), F32), _nbytes(wo.shape, BF16),
                                         _nbytes(w1.shape[1:], BF16), _nbytes(w2.shape[1:], BF16),
                                         _nbytes(wg.shape[1:], BF16), _nbytes(wp.shape[1:], BF16),
                                         2 * _nbytes((tm, ff_chunk), F32), 4 * _nbytes((tm, d), F32))),
        name="block_tail",
    )(mix, x32, p, wo, g1, b1, w1, w2, wg, wp, g2, b2)


def kernel(x, p, na_w_qkv, na_rpb, na_w_o, gq_w_qkv, gq_q_norm, gq_k_norm, gq_w_o, ml_w_in, ml_b_gates, ml_norm_g,
           ml_w_o, ln1_g, ln1_b, w_ff1, w_ff2, ln2_g, ln2_b, w_ple_gate, w_ple_proj):
    bsz, seq, d = x.shape
    assert d == D_MODEL and seq % (NA_KH * GRID_W) == 0 and p.shape == (DEPTH, bsz, seq, D_PLE)
    t = bsz * seq
    x32 = x.reshape(t, d)
    per_layer = (p.reshape(DEPTH, t, D_PLE), ln1_g[:, None, :], ln1_b[:, None, :], w_ff1.astype(BF16),
                 w_ff2.astype(BF16), w_ple_gate.astype(BF16), w_ple_proj.astype(BF16), ln2_g[:, None, :],
                 ln2_b[:, None, :])
    x16 = None
    qk_w = ML_HEADS * ML_DQK
    for i in range(DEPTH):
        kind, j = i % 3, i // 3
        if kind == 0:
            qkv = _proj(x32 if x16 is None else x16, na_w_qkv[j].astype(BF16), n_chunk=D_MODEL,
                        first_scale=HEAD_DIM ** -0.5 * LOG2E)
            mix = _na_attention(qkv, _na_bias_table(na_rpb[j]), bsz, seq)
            w_o = na_w_o[j]
        elif kind == 1:
            cos4, sin4 = _rope_tables(seq)
            w_qk, w_vt = _gqa_weight_layout(gq_w_qkv[j])
            qk, vt = _gqa_proj(x16, w_qk.astype(BF16), w_vt.astype(BF16), cos4, sin4, _gqa_gain_layout(gq_q_norm[j]),
                               _gqa_gain_layout(gq_k_norm[j]), _group_sum_matrix(), bsz, seq)
            mix = _gqa_attention(qk, vt, bsz, seq)
            w_o = gq_w_o[j]
        else:
            w_in = ml_w_in[j]
            n_main = 2 * qk_w + 2 * D_MODEL
            wkt = w_in[:, qk_w:2 * qk_w].T.astype(BF16)
            wgt = w_in[:, n_main:].T.astype(BF16)
            q, k, v, o, kt, gt = _ml_proj(x16, w_in[:, :n_main].astype(BF16), wkt, wgt, ml_b_gates[j][:, None], bsz, seq)
            gates = gt.reshape(bsz, gt.shape[1], seq // ML_BLOCK, ML_BLOCK)
            mix = _mlstm(q, k, kt, v, o, gates, ml_norm_g[j][None, :], bsz, seq)
            w_o = ml_w_o[j]
        x32, x16 = _block_tail(mix, x32, w_o.astype(BF16), i, *per_layer)
    return x32.reshape(bsz, seq, d)
```

```python
import functools

import jax
import jax.numpy as jnp
import numpy as np
from jax import lax
from jax.experimental import pallas as pl
from jax.experimental.pallas import tpu as pltpu

F32 = jnp.float32
BF16 = jnp.bfloat16

D_MODEL = 1024
DEPTH = 4
GRID_W = 64
HEAD_DIM = 64
D_PLE = 256
NA_HEADS = 16
NA_KH = 8
NA_KW = 16
GQA_KV_HEADS = 4
GQA_GROUP = 4
ROPE_THETA = 10000.0
ML_HEADS = 8
ML_DV = 128
ML_DQK = 64
ML_CHUNK = 64
ML_BLOCK = 256
DN_ALPHA = (2 * DEPTH) ** 0.25
EPS = 1e-6
LOG2E = 1.4426950408889634

V7X_VMEM_BYTES = 64 * 1024 * 1024
V7X_LANES = 128
BF16_SUBLANES = 16
SLAB = 2 * V7X_LANES

NT_DIMS = (((1,), (1,)), ((), ()))


def _vmem_limit(*byte_counts):
    est = int(sum(byte_counts) * 1.5) + (4 << 20)
    return min(est, V7X_VMEM_BYTES - (6 << 20))


def _nbytes(shape, dtype):
    return int(np.prod(shape)) * jnp.dtype(dtype).itemsize


def _const_spec(shape):
    nd = len(shape)
    return pl.BlockSpec(shape, lambda *_: (0,) * nd, pipeline_mode=pl.Buffered(1))


def _layer_norm(y, g, b):
    mu = jnp.mean(y, -1, keepdims=True)
    yc = y - mu
    var = jnp.mean(yc * yc, -1, keepdims=True)
    return yc * lax.rsqrt(var + EPS) * g + b


def _proj_kernel(x_ref, w_ref, o_ref, *, n_chunk, first_scale):
    xb = x_ref[...].astype(BF16)
    for j in range(o_ref.shape[1] // n_chunk):
        sl = slice(j * n_chunk, (j + 1) * n_chunk)
        y = jnp.dot(xb, w_ref[:, sl], preferred_element_type=F32)
        if j == 0 and first_scale != 1.0:
            y = y * first_scale
        o_ref[:, sl] = y.astype(BF16)


def _proj(x, w, tm=1024, n_chunk=1024, first_scale=1.0):
    t, d = x.shape
    n = w.shape[1]
    return pl.pallas_call(
        functools.partial(_proj_kernel, n_chunk=n_chunk, first_scale=first_scale),
        out_shape=jax.ShapeDtypeStruct((t, n), BF16),
        grid=(t // tm,),
        in_specs=[pl.BlockSpec((tm, d), lambda i: (i, 0)), _const_spec((d, n))],
        out_specs=pl.BlockSpec((tm, n), lambda i: (i, 0)),
        compiler_params=pltpu.CompilerParams(
            dimension_semantics=("parallel",),
            vmem_limit_bytes=_vmem_limit(2 * _nbytes((tm, d), x.dtype), _nbytes((d, n), BF16),
                                         2 * _nbytes((tm, n), BF16), _nbytes((tm, n_chunk), F32))),
        name="proj_plain",
    )(x, w)


def _na_kernel(q_ref, k_ref, v_ref, bias_ref, o_ref, s_ref, p_ref, *, rows, group):
    lane_head = lax.broadcasted_iota(jnp.int32, (1, SLAB), 1) // HEAD_DIM
    win = NA_KH * GRID_W

    def window(r):
        r = jnp.clip(r, 0, rows - 1)
        r0 = jnp.clip(r - NA_KH // 2, 0, rows - NA_KH)
        return pl.multiple_of(r * GRID_W, GRID_W), pl.multiple_of(r0 * GRID_W, GRID_W), r - r0

    def scores(r, slot):
        q0, k0, delta = window(r)
        q = q_ref[pl.ds(q0, GRID_W), :]
        qs = jnp.concatenate([jnp.where(lane_head == h, q, jnp.zeros_like(q)) for h in range(4)], axis=0)
        s = lax.dot_general(qs, k_ref[pl.ds(k0, win), :], NT_DIMS, preferred_element_type=F32)
        s_ref[slot] = s + bias_ref[0, delta]

    def softmax(slot):
        s = s_ref[slot]
        e = jnp.exp2(s - jnp.max(s, -1, keepdims=True))
        p_ref[slot] = (e * (1.0 / jnp.sum(e, -1, keepdims=True))).astype(BF16)

    def weighted_values(r, slot):
        q0, k0, _ = window(r)
        pv = jnp.dot(p_ref[slot], v_ref[pl.ds(k0, win), :], preferred_element_type=F32)
        acc = jnp.zeros((GRID_W, SLAB), F32)
        for h in range(4):
            acc = jnp.where(lane_head == h, pv[h * GRID_W:(h + 1) * GRID_W], acc)
        o_ref[pl.ds(q0, GRID_W), :] = acc.astype(BF16)

    def step(g, bank):
        for t in range(group):
            weighted_values((g - 1) * group + t, (1 - bank) * group + t)
        for t in range(group):
            scores((g + 1) * group + t, (1 - bank) * group + t)
        for t in range(group):
            softmax(bank * group + t)

    for t in range(group):
        scores(t, t)
        p_ref[group + t] = jnp.zeros(p_ref.shape[1:], BF16)

    def body(j, carry):
        step(2 * j, 0)
        step(2 * j + 1, 1)
        return carry

    n_step = rows // group
    lax.fori_loop(0, n_step // 2, body, 0)
    for t in range(group):
        weighted_values(rows - group + t, ((n_step - 1) % 2) * group + t)


def _na_bias_table(rpb):
    col = np.arange(GRID_W)
    c0 = np.clip(col - NA_KW // 2, 0, GRID_W - NA_KW)
    col_in = (col[None, :] >= c0[:, None]) & (col[None, :] < c0[:, None] + NA_KW)
    dc = np.clip(col[None, :] - col[:, None], 1 - NA_KW, NA_KW - 1) + NA_KW - 1
    rpb = rpb.astype(F32)
    by_col = jnp.zeros(rpb.shape[:2] + dc.shape, F32)
    for c in range(2 * NA_KW - 1):
        by_col = jnp.where(dc[None, None] == c, rpb[:, :, c][:, :, None, None], by_col)
    by_col = jnp.where(col_in[None, None], by_col * LOG2E, -jnp.inf)
    per_delta = [by_col[:, NA_KH - 1 - dl:2 * NA_KH - 1 - dl].transpose(0, 2, 1, 3)
                 .reshape(NA_HEADS, GRID_W, NA_KH * GRID_W) for dl in range(NA_KH)]
    b = jnp.stack(per_delta, axis=1).reshape(NA_HEADS // 4, 4, NA_KH, GRID_W, NA_KH * GRID_W)
    return b.transpose(0, 2, 1, 3, 4).reshape(NA_HEADS // 4, NA_KH, 4 * GRID_W, NA_KH * GRID_W)


def _na_attention(qkv, bias, bsz, seq, group=2):
    n_slab = D_MODEL // SLAB
    rows = seq // GRID_W
    assert rows % (2 * group) == 0
    blk = (seq, SLAB)
    tile = (4 * GRID_W, NA_KH * GRID_W)
    return pl.pallas_call(
        functools.partial(_na_kernel, rows=rows, group=group),
        out_shape=jax.ShapeDtypeStruct((bsz * seq, D_MODEL), BF16),
        grid=(n_slab, bsz),
        in_specs=[pl.BlockSpec(blk, lambda s, b: (b, s)),
                  pl.BlockSpec(blk, lambda s, b: (b, n_slab + s)),
                  pl.BlockSpec(blk, lambda s, b: (b, 2 * n_slab + s)),
                  pl.BlockSpec((1,) + bias.shape[1:], lambda s, b: (s, 0, 0, 0))],
        out_specs=pl.BlockSpec(blk, lambda s, b: (b, s)),
        scratch_shapes=[pltpu.VMEM((2 * group,) + tile, F32), pltpu.VMEM((2 * group,) + tile, BF16)],
        compiler_params=pltpu.CompilerParams(
            dimension_semantics=("parallel", "parallel"),
            vmem_limit_bytes=_vmem_limit(8 * _nbytes(blk, BF16), 2 * _nbytes(bias.shape[1:], F32),
                                         (3 * group + 4) * _nbytes(tile, F32))),
        name="na_attention",
    )(qkv, qkv, qkv, bias)


def _gqa_proj_kernel(x_ref, w_ref, wvt_ref, cos_ref, sin_ref, gq_ref, gk_ref, ones_ref, o_ref, vt_ref):
    xb = x_ref[...]
    cos = cos_ref[...]
    sin = sin_ref[...]
    ones = ones_ref[...]
    n_norm = 2 * D_MODEL // (2 * SLAB)

    def project(pair):
        return jnp.dot(xb, w_ref[:, pair * 2 * SLAB:(pair + 1) * 2 * SLAB], preferred_element_type=F32)

    z_next = project(0)
    for pair in range(n_norm):
        z = z_next
        if pair + 1 < n_norm:
            z_next = project(pair + 1)
        else:
            vt_ref[...] = lax.dot_general(wvt_ref[...], xb, NT_DIMS, preferred_element_type=F32).astype(BF16)
        halves = [(z[:, j * SLAB:j * SLAB + V7X_LANES], z[:, j * SLAB + V7X_LANES:(j + 1) * SLAB]) for j in range(2)]
        ss = jnp.concatenate([a * a + b * b for a, b in halves], axis=1)
        hi = ss.astype(BF16)
        lo = (ss - hi.astype(F32)).astype(BF16)
        ms = (jnp.dot(hi, ones, preferred_element_type=F32)
              + jnp.dot(lo, ones, preferred_element_type=F32)) * (1.0 / HEAD_DIM)
        rs = lax.rsqrt(ms + EPS)
        is_q = pair < n_norm // 2
        g_ref = gq_ref if is_q else gk_ref
        for j, (a, b) in enumerate(halves):
            r = rs[:, j * V7X_LANES:(j + 1) * V7X_LANES]
            an = a * r * g_ref[:, :V7X_LANES]
            bn = b * r * g_ref[:, V7X_LANES:]
            oa = an * cos - bn * sin
            ob = an * sin + bn * cos
            if is_q:
                oa = oa * (HEAD_DIM ** -0.5 * LOG2E)
                ob = ob * (HEAD_DIM ** -0.5 * LOG2E)
            c0 = (pair * 2 + j) * SLAB
            o_ref[:, c0:c0 + V7X_LANES] = oa.astype(BF16)
            o_ref[:, c0 + V7X_LANES:c0 + SLAB] = ob.astype(BF16)


def _gqa_weight_layout(w):
    d = w.shape[0]
    kvd = GQA_KV_HEADS * HEAD_DIM
    half = HEAD_DIM // 2
    wq = w[:, :D_MODEL].reshape(d, GQA_KV_HEADS, GQA_GROUP, half, 2).transpose(0, 1, 4, 2, 3)
    wk = w[:, D_MODEL:D_MODEL + kvd].reshape(d, GQA_KV_HEADS, half, 2).transpose(0, 1, 3, 2)
    wk = jnp.broadcast_to(wk[:, :, :, None, :], (d, GQA_KV_HEADS, 2, GQA_GROUP, half))
    return jnp.concatenate([wq.reshape(d, D_MODEL), wk.reshape(d, D_MODEL)], axis=1), w[:, D_MODEL + kvd:].T


def _gqa_gain_layout(g):
    half = HEAD_DIM // 2
    return jnp.broadcast_to(g.reshape(half, 2).T[:, None, :], (2, GQA_GROUP, half)).reshape(1, SLAB)


def _group_sum_matrix():
    blk = np.arange(SLAB) // (HEAD_DIM // 2)
    return jnp.asarray(blk[:, None] == blk[None, :], BF16)


def _rope_tables(seq):
    t = jnp.arange(seq)
    row = (t // GRID_W).astype(F32)
    col = (t % GRID_W).astype(F32)
    n_pairs = HEAD_DIM // 4
    inv = ROPE_THETA ** (-jnp.arange(n_pairs, dtype=F32) / n_pairs)
    ang = jnp.concatenate([row[:, None] * inv, col[:, None] * inv], -1)
    return jnp.tile(jnp.cos(ang), (1, GQA_GROUP)), jnp.tile(jnp.sin(ang), (1, GQA_GROUP))


def _gqa_proj(x16, w, wvt, cos4, sin4, gq, gk, ones, bsz, seq, tm=512):
    t, d = x16.shape
    n = w.shape[1]
    per_seq = seq // tm
    return pl.pallas_call(
        _gqa_proj_kernel,
        out_shape=(jax.ShapeDtypeStruct((t, n), BF16), jax.ShapeDtypeStruct((bsz, wvt.shape[0], seq), BF16)),
        grid=(t // tm,),
        in_specs=[pl.BlockSpec((tm, d), lambda i: (i, 0)), _const_spec((d, n)), _const_spec(wvt.shape),
                  pl.BlockSpec((tm, V7X_LANES), lambda i: (i % per_seq, 0)),
                  pl.BlockSpec((tm, V7X_LANES), lambda i: (i % per_seq, 0)),
                  _const_spec((1, SLAB)), _const_spec((1, SLAB)), _const_spec((SLAB, SLAB))],
        out_specs=(pl.BlockSpec((tm, n), lambda i: (i, 0)),
                   pl.BlockSpec((None, wvt.shape[0], tm), lambda i: (i // per_seq, 0, i % per_seq))),
        compiler_params=pltpu.CompilerParams(
            dimension_semantics=("parallel",),
            vmem_limit_bytes=_vmem_limit(2 * _nbytes((tm, d), BF16), _nbytes((d, n), BF16), _nbytes(wvt.shape, BF16),
                                         2 * _nbytes((tm, n), BF16), 6 * _nbytes((tm, 2 * SLAB), F32))),
        name="gqa_proj",
    )(x16, w, wvt, cos4, sin4, gq, gk, ones)


def _gqa_attn_kernel(q_ref, k_ref, vt_ref, o_ref, *, q_block):
    lane = lax.broadcasted_iota(jnp.int32, (1, SLAB), 1)
    q = q_ref[...]
    q_head = (lane % V7X_LANES) // (HEAD_DIM // 2)

    def scores(h):
        qm = jnp.where(q_head == h, q, jnp.zeros_like(q))
        return lax.dot_general(k_ref[...], qm, NT_DIMS, preferred_element_type=F32)

    vt = vt_ref[...]
    vt_ones = jnp.concatenate([vt, jnp.ones((BF16_SUBLANES, vt.shape[1]), BF16)], axis=0)

    def softmax(s):
        return (jnp.exp2(s - jnp.max(s, 0, keepdims=True)).astype(BF16),)

    def weighted_values(e):
        r = jnp.dot(vt_ones, e, preferred_element_type=F32)
        return r[:HEAD_DIM] * (1.0 / r[HEAD_DIM:HEAD_DIM + 1])

    n_qb = q_ref.shape[0] // q_block
    outs = [[None] * GQA_GROUP for _ in range(n_qb)]
    s_next = scores(0)
    pending = []
    for h in range(GQA_GROUP):
        s = s_next
        if h + 1 < GQA_GROUP:
            s_next = scores(h + 1)
        current = []
        for qb in range(n_qb):
            current.append(softmax(s[:, qb * q_block:(qb + 1) * q_block]))
            if pending:
                outs[qb][h - 1] = weighted_values(*pending[qb])
        pending = current
    for qb in range(n_qb):
        outs[qb][GQA_GROUP - 1] = weighted_values(*pending[qb])
        o_ref[qb * q_block:(qb + 1) * q_block, :] = jnp.concatenate(outs[qb], axis=0).T.astype(BF16)


def _gqa_attention(qk, vt, bsz, seq, tq=1024, q_block=256):
    n_slab = D_MODEL // SLAB
    nq = seq // tq
    return pl.pallas_call(
        functools.partial(_gqa_attn_kernel, q_block=q_block),
        out_shape=jax.ShapeDtypeStruct((bsz * seq, D_MODEL), BF16),
        grid=(bsz, n_slab, nq),
        in_specs=[pl.BlockSpec((tq, SLAB), lambda b, g, i: (b * nq + i, g)),
                  pl.BlockSpec((seq, SLAB), lambda b, g, i: (b, n_slab + g)),
                  pl.BlockSpec((None, HEAD_DIM, seq), lambda b, g, i: (b, g, 0))],
        out_specs=pl.BlockSpec((tq, SLAB), lambda b, g, i: (b * nq + i, g)),
        compiler_params=pltpu.CompilerParams(
            dimension_semantics=("parallel", "parallel", "parallel"),
            vmem_limit_bytes=_vmem_limit(4 * _nbytes((tq, SLAB), BF16), 4 * _nbytes((seq, SLAB), BF16),
                                         2 * _nbytes((HEAD_DIM, seq), BF16), 3 * _nbytes((tq, seq), F32))),
        name="gqa_attention",
    )(qk, qk, vt)


def _ml_proj_kernel(x_ref, w_ref, wkt_ref, wgt_ref, bg_ref, q_ref, k_ref, v_ref, o_ref, kt_ref, gt_ref):
    xb = x_ref[...]
    qk_w = ML_HEADS * ML_DQK
    q_ref[...] = (jnp.dot(xb, w_ref[:, :qk_w], preferred_element_type=F32) * (ML_DQK ** -0.5)).astype(BF16)
    k_ref[...] = jnp.dot(xb, w_ref[:, qk_w:2 * qk_w], preferred_element_type=F32).astype(BF16)
    v_ref[...] = jnp.dot(xb, w_ref[:, 2 * qk_w:2 * qk_w + D_MODEL], preferred_element_type=F32).astype(BF16)
    o_ref[...] = jax.nn.sigmoid(jnp.dot(xb, w_ref[:, 2 * qk_w + D_MODEL:], preferred_element_type=F32))
    kt = lax.dot_general(wkt_ref[...], xb, NT_DIMS, preferred_element_type=F32)
    for j in range(kt_ref.shape[1]):
        kt_ref[0, j] = kt[:, j * ML_BLOCK:(j + 1) * ML_BLOCK]
    gt_ref[0] = lax.dot_general(wgt_ref[...], xb, NT_DIMS, preferred_element_type=F32) + bg_ref[...]


def _ml_proj(x16, w, wkt, wgt, bg, bsz, seq, tm=1024):
    t, d = x16.shape
    qk_w = ML_HEADS * ML_DQK
    per_seq = seq // tm
    n_gate = wgt.shape[0]
    row = lambda i: (i, 0)
    return pl.pallas_call(
        _ml_proj_kernel,
        out_shape=(jax.ShapeDtypeStruct((t, qk_w), BF16), jax.ShapeDtypeStruct((t, qk_w), BF16),
                   jax.ShapeDtypeStruct((t, D_MODEL), BF16), jax.ShapeDtypeStruct((t, D_MODEL), F32),
                   jax.ShapeDtypeStruct((bsz, seq // ML_BLOCK, qk_w, ML_BLOCK), F32),
                   jax.ShapeDtypeStruct((bsz, n_gate, seq), F32)),
        grid=(t // tm,),
        in_specs=[pl.BlockSpec((tm, d), row), _const_spec(w.shape), _const_spec(wkt.shape),
                  _const_spec(wgt.shape), _const_spec(bg.shape)],
        out_specs=(pl.BlockSpec((tm, qk_w), row), pl.BlockSpec((tm, qk_w), row),
                   pl.BlockSpec((tm, D_MODEL), row), pl.BlockSpec((tm, D_MODEL), row),
                   pl.BlockSpec((1, tm // ML_BLOCK, qk_w, ML_BLOCK), lambda i: (i // per_seq, i % per_seq, 0, 0)),
                   pl.BlockSpec((1, n_gate, tm), lambda i: (i // per_seq, 0, i % per_seq))),
        compiler_params=pltpu.CompilerParams(
            dimension_semantics=("parallel",),
            vmem_limit_bytes=_vmem_limit(2 * _nbytes((tm, d), BF16), _nbytes(w.shape, BF16), _nbytes(wkt.shape, BF16),
                                         2 * _nbytes((tm, 2 * qk_w + D_MODEL), BF16), 2 * _nbytes((tm, D_MODEL), F32),
                                         4 * _nbytes((qk_w, tm), F32), 2 * _nbytes((tm, D_MODEL), F32))),
        name="mlstm_proj",
    )(x16, w, wkt, wgt, bg)


def _log_sigmoid(x):
    return jnp.minimum(x, 0.0) - jnp.log1p(jnp.exp(-jnp.abs(x)))


def _exact_dot(x, m):
    hi = x.astype(BF16)
    r1 = x - hi.astype(F32)
    mid = r1.astype(BF16)
    lo = (r1 - mid.astype(F32)).astype(BF16)
    return (jnp.dot(hi, m, preferred_element_type=F32) + jnp.dot(mid, m, preferred_element_type=F32)
            + jnp.dot(lo, m, preferred_element_type=F32))


def _lane_cummax(x, lane_pos, reverse):
    shift = 1
    while shift < ML_CHUNK:
        if reverse:
            moved, ok = pltpu.roll(x, x.shape[1] - shift, axis=1), lane_pos < ML_CHUNK - shift
        else:
            moved, ok = pltpu.roll(x, shift, axis=1), lane_pos >= shift
        x = jnp.maximum(x, jnp.where(ok, moved, -jnp.inf))
        shift *= 2
    return x


def _mlstm_block(q, k, kt, v, a_row, ge_row, decay_row, m_row, g_col, b_col, c_state, n_state, vis, same, row_chunk,
                 reverse):
    per = ML_BLOCK // ML_CHUNK
    v_ext = jnp.concatenate([v, jnp.ones_like(v)], axis=1)
    wkt = jnp.exp(a_row - ge_row) * kt
    wkt4 = jnp.where(same, jnp.concatenate([wkt] * per, axis=0), 0.0)
    delta = jnp.dot(wkt4.astype(BF16), v_ext, preferred_element_type=F32)
    yield
    g_rep = jnp.broadcast_to(g_col, (ML_BLOCK, ML_DV))
    b_rep = jnp.broadcast_to(b_col, (ML_BLOCK, ML_DV))
    m_rep = jnp.broadcast_to(m_row[:, (per - 1) * ML_CHUNK:(per - 1) * ML_CHUNK + 1], g_rep.shape)
    for i in range(per - 2, -1, -1):
        m_rep = jnp.where(row_chunk == i, m_row[:, i * ML_CHUNK:i * ML_CHUNK + 1], m_rep)
    s_inter = jnp.exp(m_rep - g_rep)
    floor_rep = jnp.exp(-(b_rep + g_rep))
    w = jnp.where(vis, jnp.exp(a_row - jnp.concatenate([g_rep, g_rep], axis=1)), 0.0)
    qk = lax.dot_general(q, k, NT_DIMS, preferred_element_type=F32) * w
    yield
    intra = jnp.dot(qk.astype(BF16), v_ext, preferred_element_type=F32)
    starts = [None] * per
    state = jnp.concatenate([c_state, n_state], axis=1)
    for i in (range(per - 1, -1, -1) if reverse else range(per)):
        starts[i] = state
        state = decay_row[:, i * ML_CHUNK:i * ML_CHUNK + 1] * state + delta[i * ML_DQK:(i + 1) * ML_DQK]
    yield
    q4 = jnp.where(same, jnp.concatenate([q] * per, axis=1), jnp.zeros((), q.dtype))
    inter = jnp.dot(q4, jnp.concatenate(starts, axis=0).astype(BF16), preferred_element_type=F32)
    num = s_inter * inter[:, :ML_DV] + intra[:, :ML_DV]
    den = s_inter * inter[:, ML_DV:] + intra[:, ML_DV:]
    h = num / jnp.maximum(jnp.abs(den), floor_rep)
    yield h, state[:, :ML_DV], state[:, ML_DV:]


def _mlstm_kernel(q_ref, k_ref, kt_ref, v_ref, o_ref, g_ref, ng_ref, out_ref, hfw_ref, hbw_ref, row_ref, col_ref,
                  *, n_block):
    pair = pl.program_id(1)
    per = ML_BLOCK // ML_CHUNK
    t_idx = lax.broadcasted_iota(jnp.int32, (ML_BLOCK, ML_BLOCK), 0)
    s_idx = lax.broadcasted_iota(jnp.int32, (ML_BLOCK, ML_BLOCK), 1)
    same = (t_idx // ML_CHUNK) == (s_idx // ML_CHUNK)
    masks = (same & (s_idx <= t_idx), same & (s_idx >= t_idx))
    lane = lax.broadcasted_iota(jnp.int32, (1, ML_BLOCK), 1)
    lane_chunk, lane_pos = lane // ML_CHUNK, lane % ML_CHUNK
    blk_row = lax.broadcasted_iota(jnp.int32, (n_block, 1), 0)
    row_chunk = lax.broadcasted_iota(jnp.int32, (ML_BLOCK, ML_DV), 0) // ML_CHUNK

    per_query = []
    for direction in range(2):
        cum = jnp.where(masks[1 - direction], 1.0, 0.0).astype(BF16)
        for hh in range(2):
            chain = direction * 2 + hh
            head = pair * 2 + hh
            lf = _log_sigmoid(g_ref[0, (direction * 2 + 1) * ML_HEADS + head])
            b = _exact_dot(lf, cum)
            a = g_ref[0, direction * 2 * ML_HEADS + head] - b
            a_max = [jnp.max(jnp.where(lane_chunk == i, a, -jnp.inf), axis=1, keepdims=True) for i in range(per)]
            f_sum = [jnp.sum(jnp.where(lane_chunk == i, lf, 0.0), axis=1, keepdims=True) for i in range(per)]
            m = jnp.zeros((1, 1), F32)
            m_row = jnp.zeros(a.shape, F32)
            ge_row = jnp.zeros(a.shape, F32)
            n_chunk = n_block * per
            for c in (range(n_chunk - 1, -1, -1) if direction else range(n_chunk)):
                blk, i = divmod(c, per)
                g_end = jnp.maximum(m, a_max[i][blk:blk + 1])
                here = (blk_row == blk) & (lane_chunk == i)
                m_row = jnp.where(here, m, m_row)
                ge_row = jnp.where(here, g_end, ge_row)
                m = f_sum[i][blk:blk + 1] + g_end
            g_row = jnp.maximum(m_row, _lane_cummax(a, lane_pos, reverse=bool(direction)))
            row_ref[chain, 0] = a
            row_ref[chain, 1] = ge_row
            row_ref[chain, 2] = jnp.exp(m_row - ge_row)
            row_ref[chain, 3] = m_row
            per_query += [g_row, b]
    flat = [jnp.concatenate([x[blk:blk + 1] for blk in range(n_block)], axis=1) for x in per_query]
    col_ref[...] = jnp.concatenate(flat, axis=0).T

    def body(step, carry):
        chains = []
        for direction in range(2):
            blk = step if direction == 0 else n_block - 1 - step
            rows = pl.ds(pl.multiple_of(blk * ML_BLOCK, ML_BLOCK), ML_BLOCK)
            for hh in range(2):
                chain = direction * 2 + hh
                c_state, n_state = carry[chain]
                stages = _mlstm_block(
                    q_ref[rows, hh * ML_DQK:(hh + 1) * ML_DQK], k_ref[rows, hh * ML_DQK:(hh + 1) * ML_DQK],
                    kt_ref[0, blk, hh * ML_DQK:(hh + 1) * ML_DQK, :], v_ref[rows, hh * ML_DV:(hh + 1) * ML_DV],
                    *[row_ref[chain, j, pl.ds(blk, 1), :] for j in range(4)],
                    col_ref[rows, 2 * chain:2 * chain + 1], col_ref[rows, 2 * chain + 1:2 * chain + 2],
                    c_state, n_state, masks[direction], same, row_chunk, reverse=bool(direction))
                chains.append((stages, hfw_ref if direction == 0 else hbw_ref, rows, hh))
        for _ in range(3):
            for stages, _, _, _ in chains:
                next(stages)
        new_carry = []
        for stages, dst, rows, hh in chains:
            h, c_new, n_new = next(stages)
            dst[rows, hh * ML_DV:(hh + 1) * ML_DV] = h
            new_carry.append((c_new, n_new))
        return tuple(new_carry)

    init = tuple((jnp.zeros((ML_DQK, ML_DV), F32), jnp.zeros((ML_DQK, ML_DV), F32)) for _ in range(4))
    lax.fori_loop(0, n_block, body, init)

    for hh in range(2):
        cols = slice(hh * ML_DV, (hh + 1) * ML_DV)
        h = hfw_ref[:, cols] + hbw_ref[:, cols]
        ms = jnp.mean(h * h, -1, keepdims=True)
        hn = h * lax.rsqrt(ms + EPS) * ng_ref[:, cols]
        out_ref[:, cols] = (o_ref[:, cols] * hn).astype(BF16)


def _mlstm(q, k, kt, v, o, gates, norm_g, bsz, seq):
    n_pair = ML_HEADS // 2
    n_block = seq // ML_BLOCK
    pair_w = 2 * ML_DV
    return pl.pallas_call(
        functools.partial(_mlstm_kernel, n_block=n_block),
        out_shape=jax.ShapeDtypeStruct((bsz * seq, D_MODEL), BF16),
        grid=(bsz, n_pair),
        in_specs=[pl.BlockSpec((seq, 2 * ML_DQK), lambda b, p: (b, p)),
                  pl.BlockSpec((seq, 2 * ML_DQK), lambda b, p: (b, p)),
                  pl.BlockSpec((1, n_block, 2 * ML_DQK, ML_BLOCK), lambda b, p: (b, 0, p, 0)),
                  pl.BlockSpec((seq, pair_w), lambda b, p: (b, p)),
                  pl.BlockSpec((seq, pair_w), lambda b, p: (b, p)),
                  pl.BlockSpec((1,) + gates.shape[1:], lambda b, p: (b, 0, 0, 0)),
                  pl.BlockSpec((1, pair_w), lambda b, p: (0, p))],
        out_specs=pl.BlockSpec((seq, pair_w), lambda b, p: (b, p)),
        scratch_shapes=[pltpu.VMEM((seq, pair_w), F32), pltpu.VMEM((seq, pair_w), F32),
                        pltpu.VMEM((4, 4, n_block, ML_BLOCK), F32), pltpu.VMEM((seq, 8), F32)],
        compiler_params=pltpu.CompilerParams(
            dimension_semantics=("parallel", "parallel"),
            vmem_limit_bytes=_vmem_limit(4 * _nbytes((seq, 2 * ML_DQK), BF16), 2 * _nbytes((seq, 2 * ML_DQK), F32),
                                         4 * _nbytes((seq, pair_w), BF16), 2 * _nbytes((seq, pair_w), F32),
                                         2 * _nbytes(gates.shape[1:], F32), 2 * _nbytes((seq, pair_w), F32))),
        name="mlstm_scan",
    )(q, k, kt, v, o, gates, norm_g)


def _block_tail_kernel(mix_ref, x_ref, p_ref, wo_ref, g1_ref, b1_ref, w1_ref, w2_ref, wg_ref, wp_ref, g2_ref, b2_ref,
                       o32_ref, o16_ref, *, ff_chunk, parts):
    tr = x_ref.shape[0] // parts
    rows = [slice(i * tr, (i + 1) * tr) for i in range(parts)]

    def out_proj(r):
        return jnp.dot(mix_ref[r, :], wo_ref[...], preferred_element_type=F32) + DN_ALPHA * x_ref[r, :]

    def ffn_chunk(xb, c):
        sl = slice(c * ff_chunk, (c + 1) * ff_chunk)
        h = jnp.maximum(jnp.dot(xb, w1_ref[:, sl], preferred_element_type=F32), 0.0)
        return jnp.dot((h * h).astype(BF16), w2_ref[sl, :], preferred_element_type=F32)

    def finish(x1, acc, r):
        z = _layer_norm(DN_ALPHA * x1 + acc, g2_ref[...], b2_ref[...])
        o32_ref[r, :] = z
        o16_ref[r, :] = z.astype(BF16)

    y = out_proj(rows[0])
    done = None
    for i in range(parts):
        y_next = out_proj(rows[i + 1]) if i + 1 < parts else None
        x1 = _layer_norm(y, g1_ref[...], b1_ref[...])
        xb = x1.astype(BF16)
        gate = jax.nn.sigmoid(jnp.dot(xb, wg_ref[...], preferred_element_type=F32))
        acc = gate * jnp.dot(p_ref[rows[i], :].astype(BF16), wp_ref[...], preferred_element_type=F32)
        for c in range(w1_ref.shape[1] // ff_chunk):
            acc = acc + ffn_chunk(xb, c)
            if c == 0 and done is not None:
                finish(*done)
        done = (x1, acc, rows[i])
        y = y_next
    finish(*done)


def _block_tail(mix, x32, wo, layer, p, g1, b1, w1, w2, wg, wp, g2, b2, tm=512, ff_chunk=1024, parts=2):
    t, d = x32.shape
    row = lambda i: (i, 0)

    def of_layer(a):
        nd = a.ndim - 1
        return pl.BlockSpec((None,) + a.shape[1:], lambda i: (layer,) + (0,) * nd, pipeline_mode=pl.Buffered(1))

    return pl.pallas_call(
        functools.partial(_block_tail_kernel, ff_chunk=ff_chunk, parts=parts),
        out_shape=(jax.ShapeDtypeStruct((t, d), F32), jax.ShapeDtypeStruct((t, d), BF16)),
        grid=(t // tm,),
        in_specs=[pl.BlockSpec((tm, d), row), pl.BlockSpec((tm, d), row),
                  pl.BlockSpec((None, tm, p.shape[2]), lambda i: (layer, i, 0)),
                  _const_spec(wo.shape), of_layer(g1), of_layer(b1), of_layer(w1), of_layer(w2),
                  of_layer(wg), of_layer(wp), of_layer(g2), of_layer(b2)],
        out_specs=(pl.BlockSpec((tm, d), row), pl.BlockSpec((tm, d), row)),
        compiler_params=pltpu.CompilerParams(
            dimension_semantics=("parallel",),
            vmem_limit_bytes=_vmem_limit(4 * _nbytes((tm, d), BF16), 4 * _nbytes((tm, d), F32),
                                         2 * _nbytes((tm, p.shape[2]), F32), _nbytes(wo.shape, BF16),
                                         _nbytes(w1.shape[1:], BF16), _nbytes(w2.shape[1:], BF16),
                                         _nbytes(wg.shape[1:], BF16), _nbytes(wp.shape[1:], BF16),
                                         2 * _nbytes((tm, ff_chunk), F32), 4 * _nbytes((tm, d), F32))),
        name="block_tail",
    )(mix, x32, p, wo, g1, b1, w1, w2, wg, wp, g2, b2)


def kernel(x, p, na_w_qkv, na_rpb, na_w_o, gq_w_qkv, gq_q_norm, gq_k_norm, gq_w_o, ml_w_in, ml_b_gates, ml_norm_g,
           ml_w_o, ln1_g, ln1_b, w_ff1, w_ff2, ln2_g, ln2_b, w_ple_gate, w_ple_proj):
    bsz, seq, d = x.shape
    assert d == D_MODEL and seq % (NA_KH * GRID_W) == 0 and p.shape == (DEPTH, bsz, seq, D_PLE)
    t = bsz * seq
    x32 = x.reshape(t, d)
    per_layer = (p.reshape(DEPTH, t, D_PLE), ln1_g[:, None, :], ln1_b[:, None, :], w_ff1.astype(BF16),
                 w_ff2.astype(BF16), w_ple_gate.astype(BF16), w_ple_proj.astype(BF16), ln2_g[:, None, :],
                 ln2_b[:, None, :])
    x16 = None
    qk_w = ML_HEADS * ML_DQK
    for i in range(DEPTH):
        kind, j = i % 3, i // 3
        if kind == 0:
            qkv = _proj(x32 if x16 is None else x16, na_w_qkv[j].astype(BF16), n_chunk=D_MODEL,
                        first_scale=HEAD_DIM ** -0.5 * LOG2E)
            mix = _na_attention(qkv, _na_bias_table(na_rpb[j]), bsz, seq)
            w_o = na_w_o[j]
        elif kind == 1:
            cos4, sin4 = _rope_tables(seq)
            w_qk, w_vt = _gqa_weight_layout(gq_w_qkv[j])
            qk, vt = _gqa_proj(x16, w_qk.astype(BF16), w_vt.astype(BF16), cos4, sin4, _gqa_gain_layout(gq_q_norm[j]),
                               _gqa_gain_layout(gq_k_norm[j]), _group_sum_matrix(), bsz, seq)
            mix = _gqa_attention(qk, vt, bsz, seq)
            w_o = gq_w_o[j]
        else:
            w_in = ml_w_in[j]
            n_main = 2 * qk_w + 2 * D_MODEL
            wkt = w_in[:, qk_w:2 * qk_w].T.astype(BF16)
            wgt = w_in[:, n_main:].T.astype(BF16)
            q, k, v, o, kt, gt = _ml_proj(x16, w_in[:, :n_main].astype(BF16), wkt, wgt, ml_b_gates[j][:, None], bsz, seq)
            gates = gt.reshape(bsz, gt.shape[1], seq // ML_BLOCK, ML_BLOCK)
            mix = _mlstm(q, k, kt, v, o, gates, ml_norm_g[j][None, :], bsz, seq)
            w_o = ml_w_o[j]
        x32, x16 = _block_tail(mix, x32, w_o.astype(BF16), i, *per_layer)
    return x32.reshape(bsz, seq, d)
```

```python
import functools

import jax
import jax.numpy as jnp
import numpy as np
from jax import lax
from jax.experimental import pallas as pl
from jax.experimental.pallas import tpu as pltpu

F32 = jnp.float32
BF16 = jnp.bfloat16

D_MODEL = 1024
DEPTH = 4
GRID_W = 64
HEAD_DIM = 64
D_PLE = 256
NA_HEADS = 16
NA_KH = 8
NA_KW = 16
GQA_KV_HEADS = 4
GQA_GROUP = 4
ROPE_THETA = 10000.0
ML_HEADS = 8
ML_DV = 128
ML_DQK = 64
ML_CHUNK = 64
ML_BLOCK = 256
DN_ALPHA = (2 * DEPTH) ** 0.25
EPS = 1e-6
LOG2E = 1.4426950408889634

V7X_VMEM_BYTES = 64 * 1024 * 1024
V7X_LANES = 128
BF16_SUBLANES = 16
SLAB = 2 * V7X_LANES

NT_DIMS = (((1,), (1,)), ((), ()))


def _vmem_limit(*byte_counts):
    est = int(sum(byte_counts) * 1.5) + (4 << 20)
    return min(est, V7X_VMEM_BYTES - (6 << 20))


def _nbytes(shape, dtype):
    return int(np.prod(shape)) * jnp.dtype(dtype).itemsize


def _const_spec(shape):
    nd = len(shape)
    return pl.BlockSpec(shape, lambda *_: (0,) * nd, pipeline_mode=pl.Buffered(1))


def _layer_norm(y, g, b):
    mu = jnp.mean(y, -1, keepdims=True)
    yc = y - mu
    var = jnp.mean(yc * yc, -1, keepdims=True)
    return yc * lax.rsqrt(var + EPS) * g + b


def _proj_kernel(x_ref, w_ref, o_ref, *, n_chunk, first_scale):
    xb = x_ref[...].astype(BF16)
    for j in range(o_ref.shape[1] // n_chunk):
        sl = slice(j * n_chunk, (j + 1) * n_chunk)
        y = jnp.dot(xb, w_ref[:, sl], preferred_element_type=F32)
        if j == 0 and first_scale != 1.0:
            y = y * first_scale
        o_ref[:, sl] = y.astype(BF16)


def _proj(x, w, tm=1024, n_chunk=1024, first_scale=1.0):
    t, d = x.shape
    n = w.shape[1]
    return pl.pallas_call(
        functools.partial(_proj_kernel, n_chunk=n_chunk, first_scale=first_scale),
        out_shape=jax.ShapeDtypeStruct((t, n), BF16),
        grid=(t // tm,),
        in_specs=[pl.BlockSpec((tm, d), lambda i: (i, 0)), _const_spec((d, n))],
        out_specs=pl.BlockSpec((tm, n), lambda i: (i, 0)),
        compiler_params=pltpu.CompilerParams(
            dimension_semantics=("parallel",),
            vmem_limit_bytes=_vmem_limit(2 * _nbytes((tm, d), x.dtype), _nbytes((d, n), BF16),
                                         2 * _nbytes((tm, n), BF16), _nbytes((tm, n_chunk), F32))),
        name="proj_plain",
    )(x, w)


def _na_kernel(q_ref, k_ref, v_ref, bias_ref, o_ref, s_ref, p_ref, *, rows, group):
    lane_head = lax.broadcasted_iota(jnp.int32, (1, SLAB), 1) // HEAD_DIM
    win = NA_KH * GRID_W

    def window(r):
        r = jnp.clip(r, 0, rows - 1)
        r0 = jnp.clip(r - NA_KH // 2, 0, rows - NA_KH)
        return pl.multiple_of(r * GRID_W, GRID_W), pl.multiple_of(r0 * GRID_W, GRID_W), r - r0

    def scores(r, slot):
        q0, k0, delta = window(r)
        q = q_ref[pl.ds(q0, GRID_W), :]
        qs = jnp.concatenate([jnp.where(lane_head == h, q, jnp.zeros_like(q)) for h in range(4)], axis=0)
        s = lax.dot_general(qs, k_ref[pl.ds(k0, win), :], NT_DIMS, preferred_element_type=F32)
        s_ref[slot] = s + bias_ref[0, delta]

    def softmax(slot):
        s = s_ref[slot]
        e = jnp.exp2(s - jnp.max(s, -1, keepdims=True))
        p_ref[slot] = (e * (1.0 / jnp.sum(e, -1, keepdims=True))).astype(BF16)

    def weighted_values(r, slot):
        q0, k0, _ = window(r)
        pv = jnp.dot(p_ref[slot], v_ref[pl.ds(k0, win), :], preferred_element_type=F32)
        acc = jnp.zeros((GRID_W, SLAB), F32)
        for h in range(4):
            acc = jnp.where(lane_head == h, pv[h * GRID_W:(h + 1) * GRID_W], acc)
        o_ref[pl.ds(q0, GRID_W), :] = acc.astype(BF16)

    def step(g, bank):
        for t in range(group):
            weighted_values((g - 1) * group + t, (1 - bank) * group + t)
        for t in range(group):
            scores((g + 1) * group + t, (1 - bank) * group + t)
        for t in range(group):
            softmax(bank * group + t)

    for t in range(group):
        scores(t, t)
        p_ref[group + t] = jnp.zeros(p_ref.shape[1:], BF16)

    def body(j, carry):
        step(2 * j, 0)
        step(2 * j + 1, 1)
        return carry

    n_step = rows // group
    lax.fori_loop(0, n_step // 2, body, 0)
    for t in range(group):
        weighted_values(rows - group + t, ((n_step - 1) % 2) * group + t)


def _na_bias_table(rpb):
    col = np.arange(GRID_W)
    c0 = np.clip(col - NA_KW // 2, 0, GRID_W - NA_KW)
    col_in = (col[None, :] >= c0[:, None]) & (col[None, :] < c0[:, None] + NA_KW)
    rpb = rpb.astype(F32)
    edge = GRID_W - NA_KW
    ext = jnp.concatenate([jnp.repeat(rpb[:, :, :1], edge, axis=2), rpb, jnp.repeat(rpb[:, :, -1:], edge, axis=2)], axis=2)
    by_col = jnp.stack([ext[:, :, GRID_W - 1 - q:2 * GRID_W - 1 - q] for q in range(GRID_W)], axis=2)
    by_col = jnp.where(col_in[None, None], by_col * LOG2E, -jnp.inf)
    per_delta = [by_col[:, NA_KH - 1 - dl:2 * NA_KH - 1 - dl].transpose(0, 2, 1, 3)
                 .reshape(NA_HEADS, GRID_W, NA_KH * GRID_W) for dl in range(NA_KH)]
    b = jnp.stack(per_delta, axis=1).reshape(NA_HEADS // 4, 4, NA_KH, GRID_W, NA_KH * GRID_W)
    return b.transpose(0, 2, 1, 3, 4).reshape(NA_HEADS // 4, NA_KH, 4 * GRID_W, NA_KH * GRID_W)


def _na_attention(qkv, bias, bsz, seq, group=2):
    n_slab = D_MODEL // SLAB
    rows = seq // GRID_W
    assert rows % (2 * group) == 0
    blk = (seq, SLAB)
    tile = (4 * GRID_W, NA_KH * GRID_W)
    return pl.pallas_call(
        functools.partial(_na_kernel, rows=rows, group=group),
        out_shape=jax.ShapeDtypeStruct((bsz * seq, D_MODEL), BF16),
        grid=(n_slab, bsz),
        in_specs=[pl.BlockSpec(blk, lambda s, b: (b, s)),
                  pl.BlockSpec(blk, lambda s, b: (b, n_slab + s)),
                  pl.BlockSpec(blk, lambda s, b: (b, 2 * n_slab + s)),
                  pl.BlockSpec((1,) + bias.shape[1:], lambda s, b: (s, 0, 0, 0))],
        out_specs=pl.BlockSpec(blk, lambda s, b: (b, s)),
        scratch_shapes=[pltpu.VMEM((2 * group,) + tile, F32), pltpu.VMEM((2 * group,) + tile, BF16)],
        compiler_params=pltpu.CompilerParams(
            dimension_semantics=("parallel", "parallel"),
            vmem_limit_bytes=_vmem_limit(8 * _nbytes(blk, BF16), 2 * _nbytes(bias.shape[1:], F32),
                                         (3 * group + 4) * _nbytes(tile, F32))),
        name="na_attention",
    )(qkv, qkv, qkv, bias)


def _gqa_proj_kernel(x_ref, w_ref, wvt_ref, cos_ref, sin_ref, gq_ref, gk_ref, ones_ref, o_ref, vt_ref):
    xb = x_ref[...]
    cos = cos_ref[...]
    sin = sin_ref[...]
    ones = ones_ref[...]
    n_norm = 2 * D_MODEL // (2 * SLAB)

    def project(pair):
        return jnp.dot(xb, w_ref[:, pair * 2 * SLAB:(pair + 1) * 2 * SLAB], preferred_element_type=F32)

    z_next = project(0)
    for pair in range(n_norm):
        z = z_next
        if pair + 1 < n_norm:
            z_next = project(pair + 1)
        else:
            vt_ref[...] = lax.dot_general(wvt_ref[...], xb, NT_DIMS, preferred_element_type=F32).astype(BF16)
        halves = [(z[:, j * SLAB:j * SLAB + V7X_LANES], z[:, j * SLAB + V7X_LANES:(j + 1) * SLAB]) for j in range(2)]
        ss = jnp.concatenate([a * a + b * b for a, b in halves], axis=1)
        hi = ss.astype(BF16)
        lo = (ss - hi.astype(F32)).astype(BF16)
        ms = (jnp.dot(hi, ones, preferred_element_type=F32)
              + jnp.dot(lo, ones, preferred_element_type=F32)) * (1.0 / HEAD_DIM)
        rs = lax.rsqrt(ms + EPS)
        is_q = pair < n_norm // 2
        g_ref = gq_ref if is_q else gk_ref
        for j, (a, b) in enumerate(halves):
            r = rs[:, j * V7X_LANES:(j + 1) * V7X_LANES]
            an = a * r * g_ref[:, :V7X_LANES]
            bn = b * r * g_ref[:, V7X_LANES:]
            oa = an * cos - bn * sin
            ob = an * sin + bn * cos
            if is_q:
                oa = oa * (HEAD_DIM ** -0.5 * LOG2E)
                ob = ob * (HEAD_DIM ** -0.5 * LOG2E)
            c0 = (pair * 2 + j) * SLAB
            o_ref[:, c0:c0 + V7X_LANES] = oa.astype(BF16)
            o_ref[:, c0 + V7X_LANES:c0 + SLAB] = ob.astype(BF16)


def _gqa_weight_layout(w):
    d = w.shape[0]
    kvd = GQA_KV_HEADS * HEAD_DIM
    half = HEAD_DIM // 2
    wq = w[:, :D_MODEL].reshape(d, GQA_KV_HEADS, GQA_GROUP, half, 2).transpose(0, 1, 4, 2, 3)
    wk = w[:, D_MODEL:D_MODEL + kvd].reshape(d, GQA_KV_HEADS, half, 2).transpose(0, 1, 3, 2)
    wk = jnp.broadcast_to(wk[:, :, :, None, :], (d, GQA_KV_HEADS, 2, GQA_GROUP, half))
    return jnp.concatenate([wq.reshape(d, D_MODEL), wk.reshape(d, D_MODEL)], axis=1), w[:, D_MODEL + kvd:].T


def _gqa_gain_layout(g):
    half = HEAD_DIM // 2
    return jnp.broadcast_to(g.reshape(half, 2).T[:, None, :], (2, GQA_GROUP, half)).reshape(1, SLAB)


def _group_sum_matrix():
    blk = np.arange(SLAB) // (HEAD_DIM // 2)
    return jnp.asarray(blk[:, None] == blk[None, :], BF16)


def _rope_tables(seq):
    t = jnp.arange(seq)
    row = (t // GRID_W).astype(F32)
    col = (t % GRID_W).astype(F32)
    n_pairs = HEAD_DIM // 4
    inv = ROPE_THETA ** (-jnp.arange(n_pairs, dtype=F32) / n_pairs)
    ang = jnp.concatenate([row[:, None] * inv, col[:, None] * inv], -1)
    return jnp.tile(jnp.cos(ang), (1, GQA_GROUP)), jnp.tile(jnp.sin(ang), (1, GQA_GROUP))


def _gqa_proj(x16, w, wvt, cos4, sin4, gq, gk, ones, bsz, seq, tm=1024):
    t, d = x16.shape
    n = w.shape[1]
    per_seq = seq // tm
    return pl.pallas_call(
        _gqa_proj_kernel,
        out_shape=(jax.ShapeDtypeStruct((t, n), BF16), jax.ShapeDtypeStruct((bsz, wvt.shape[0], seq), BF16)),
        grid=(t // tm,),
        in_specs=[pl.BlockSpec((tm, d), lambda i: (i, 0)), _const_spec((d, n)), _const_spec(wvt.shape),
                  pl.BlockSpec((tm, V7X_LANES), lambda i: (i % per_seq, 0)),
                  pl.BlockSpec((tm, V7X_LANES), lambda i: (i % per_seq, 0)),
                  _const_spec((1, SLAB)), _const_spec((1, SLAB)), _const_spec((SLAB, SLAB))],
        out_specs=(pl.BlockSpec((tm, n), lambda i: (i, 0)),
                   pl.BlockSpec((None, wvt.shape[0], tm), lambda i: (i // per_seq, 0, i % per_seq))),
        compiler_params=pltpu.CompilerParams(
            dimension_semantics=("parallel",),
            vmem_limit_bytes=_vmem_limit(2 * _nbytes((tm, d), BF16), _nbytes((d, n), BF16), _nbytes(wvt.shape, BF16),
                                         2 * _nbytes((tm, n), BF16), 6 * _nbytes((tm, 2 * SLAB), F32))),
        name="gqa_proj",
    )(x16, w, wvt, cos4, sin4, gq, gk, ones)


def _gqa_attn_kernel(q_ref, k_ref, vt_ref, o_ref, *, q_block):
    lane = lax.broadcasted_iota(jnp.int32, (1, SLAB), 1)
    q = q_ref[...]
    q_head = (lane % V7X_LANES) // (HEAD_DIM // 2)

    def scores(h):
        qm = jnp.where(q_head == h, q, jnp.zeros_like(q))
        return lax.dot_general(k_ref[...], qm, NT_DIMS, preferred_element_type=F32)

    vt = vt_ref[...]
    vt_ones = jnp.concatenate([vt, jnp.ones((BF16_SUBLANES, vt.shape[1]), BF16)], axis=0)

    def softmax(s):
        return (jnp.exp2(s - jnp.max(s, 0, keepdims=True)).astype(BF16),)

    def weighted_values(e):
        r = jnp.dot(vt_ones, e, preferred_element_type=F32)
        return r[:HEAD_DIM] * (1.0 / r[HEAD_DIM:HEAD_DIM + 1])

    n_qb = q_ref.shape[0] // q_block
    outs = [[None] * GQA_GROUP for _ in range(n_qb)]
    s_next = scores(0)
    pending = []
    for h in range(GQA_GROUP):
        s = s_next
        if h + 1 < GQA_GROUP:
            s_next = scores(h + 1)
        current = []
        for qb in range(n_qb):
            current.append(softmax(s[:, qb * q_block:(qb + 1) * q_block]))
            if pending:
                outs[qb][h - 1] = weighted_values(*pending[qb])
        pending = current
    for qb in range(n_qb):
        outs[qb][GQA_GROUP - 1] = weighted_values(*pending[qb])
        o_ref[qb * q_block:(qb + 1) * q_block, :] = jnp.concatenate(outs[qb], axis=0).T.astype(BF16)


def _gqa_attention(qk, vt, bsz, seq, tq=1024, q_block=256):
    n_slab = D_MODEL // SLAB
    nq = seq // tq
    return pl.pallas_call(
        functools.partial(_gqa_attn_kernel, q_block=q_block),
        out_shape=jax.ShapeDtypeStruct((bsz * seq, D_MODEL), BF16),
        grid=(bsz, n_slab, nq),
        in_specs=[pl.BlockSpec((tq, SLAB), lambda b, g, i: (b * nq + i, g)),
                  pl.BlockSpec((seq, SLAB), lambda b, g, i: (b, n_slab + g)),
                  pl.BlockSpec((None, HEAD_DIM, seq), lambda b, g, i: (b, g, 0))],
        out_specs=pl.BlockSpec((tq, SLAB), lambda b, g, i: (b * nq + i, g)),
        compiler_params=pltpu.CompilerParams(
            dimension_semantics=("parallel", "parallel", "parallel"),
            vmem_limit_bytes=_vmem_limit(4 * _nbytes((tq, SLAB), BF16), 4 * _nbytes((seq, SLAB), BF16),
                                         2 * _nbytes((HEAD_DIM, seq), BF16), 3 * _nbytes((tq, seq), F32))),
        name="gqa_attention",
    )(qk, qk, vt)


def _ml_proj_kernel(x_ref, w_ref, wkt_ref, wgt_ref, bg_ref, q_ref, k_ref, v_ref, o_ref, kt_ref, gt_ref):
    xb = x_ref[...]
    qk_w = ML_HEADS * ML_DQK
    q_ref[...] = (jnp.dot(xb, w_ref[:, :qk_w], preferred_element_type=F32) * (ML_DQK ** -0.5)).astype(BF16)
    k_ref[...] = jnp.dot(xb, w_ref[:, qk_w:2 * qk_w], preferred_element_type=F32).astype(BF16)
    v_ref[...] = jnp.dot(xb, w_ref[:, 2 * qk_w:2 * qk_w + D_MODEL], preferred_element_type=F32).astype(BF16)
    o_ref[...] = jax.nn.sigmoid(jnp.dot(xb, w_ref[:, 2 * qk_w + D_MODEL:], preferred_element_type=F32))
    kt = lax.dot_general(wkt_ref[...], xb, NT_DIMS, preferred_element_type=F32)
    for j in range(kt_ref.shape[1]):
        kt_ref[0, j] = kt[:, j * ML_BLOCK:(j + 1) * ML_BLOCK]
    gt_ref[0] = lax.dot_general(wgt_ref[...], xb, NT_DIMS, preferred_element_type=F32) + bg_ref[...]


def _ml_proj(x16, w, wkt, wgt, bg, bsz, seq, tm=1024):
    t, d = x16.shape
    qk_w = ML_HEADS * ML_DQK
    per_seq = seq // tm
    n_gate = wgt.shape[0]
    row = lambda i: (i, 0)
    return pl.pallas_call(
        _ml_proj_kernel,
        out_shape=(jax.ShapeDtypeStruct((t, qk_w), BF16), jax.ShapeDtypeStruct((t, qk_w), BF16),
                   jax.ShapeDtypeStruct((t, D_MODEL), BF16), jax.ShapeDtypeStruct((t, D_MODEL), F32),
                   jax.ShapeDtypeStruct((bsz, seq // ML_BLOCK, qk_w, ML_BLOCK), F32),
                   jax.ShapeDtypeStruct((bsz, n_gate, seq), F32)),
        grid=(t // tm,),
        in_specs=[pl.BlockSpec((tm, d), row), _const_spec(w.shape), _const_spec(wkt.shape),
                  _const_spec(wgt.shape), _const_spec(bg.shape)],
        out_specs=(pl.BlockSpec((tm, qk_w), row), pl.BlockSpec((tm, qk_w), row),
                   pl.BlockSpec((tm, D_MODEL), row), pl.BlockSpec((tm, D_MODEL), row),
                   pl.BlockSpec((1, tm // ML_BLOCK, qk_w, ML_BLOCK), lambda i: (i // per_seq, i % per_seq, 0, 0)),
                   pl.BlockSpec((1, n_gate, tm), lambda i: (i // per_seq, 0, i % per_seq))),
        compiler_params=pltpu.CompilerParams(
            dimension_semantics=("parallel",),
            vmem_limit_bytes=_vmem_limit(2 * _nbytes((tm, d), BF16), _nbytes(w.shape, BF16), _nbytes(wkt.shape, BF16),
                                         2 * _nbytes((tm, 2 * qk_w + D_MODEL), BF16), 2 * _nbytes((tm, D_MODEL), F32),
                                         4 * _nbytes((qk_w, tm), F32), 2 * _nbytes((tm, D_MODEL), F32))),
        name="mlstm_proj",
    )(x16, w, wkt, wgt, bg)


def _log_sigmoid(x):
    return jnp.minimum(x, 0.0) - jnp.log1p(jnp.exp(-jnp.abs(x)))


def _exact_dot(x, m):
    hi = x.astype(BF16)
    r1 = x - hi.astype(F32)
    mid = r1.astype(BF16)
    lo = (r1 - mid.astype(F32)).astype(BF16)
    return (jnp.dot(hi, m, preferred_element_type=F32) + jnp.dot(mid, m, preferred_element_type=F32)
            + jnp.dot(lo, m, preferred_element_type=F32))


def _lane_cummax(x, lane_pos, reverse):
    shift = 1
    while shift < ML_CHUNK:
        if reverse:
            moved, ok = pltpu.roll(x, x.shape[1] - shift, axis=1), lane_pos < ML_CHUNK - shift
        else:
            moved, ok = pltpu.roll(x, shift, axis=1), lane_pos >= shift
        x = jnp.maximum(x, jnp.where(ok, moved, -jnp.inf))
        shift *= 2
    return x


def _mlstm_block(q, k, kt, v, a_row, ge_row, decay_row, m_row, g_col, b_col, c_state, n_state, vis, same, row_chunk,
                 reverse):
    per = ML_BLOCK // ML_CHUNK
    v_ext = jnp.concatenate([v, jnp.ones_like(v)], axis=1)
    wkt = jnp.exp(a_row - ge_row) * kt
    wkt4 = jnp.where(same, jnp.concatenate([wkt] * per, axis=0), 0.0)
    delta = jnp.dot(wkt4.astype(BF16), v_ext, preferred_element_type=F32)
    yield
    g_rep = jnp.broadcast_to(g_col, (ML_BLOCK, ML_DV))
    b_rep = jnp.broadcast_to(b_col, (ML_BLOCK, ML_DV))
    m_rep = jnp.broadcast_to(m_row[:, (per - 1) * ML_CHUNK:(per - 1) * ML_CHUNK + 1], g_rep.shape)
    for i in range(per - 2, -1, -1):
        m_rep = jnp.where(row_chunk == i, m_row[:, i * ML_CHUNK:i * ML_CHUNK + 1], m_rep)
    s_inter = jnp.exp(m_rep - g_rep)
    floor_rep = jnp.exp(-(b_rep + g_rep))
    w = jnp.where(vis, jnp.exp(a_row - jnp.concatenate([g_rep, g_rep], axis=1)), 0.0)
    qk = lax.dot_general(q, k, NT_DIMS, preferred_element_type=F32) * w
    yield
    intra = jnp.dot(qk.astype(BF16), v_ext, preferred_element_type=F32)
    starts = [None] * per
    state = jnp.concatenate([c_state, n_state], axis=1)
    for i in (range(per - 1, -1, -1) if reverse else range(per)):
        starts[i] = state
        state = decay_row[:, i * ML_CHUNK:i * ML_CHUNK + 1] * state + delta[i * ML_DQK:(i + 1) * ML_DQK]
    yield
    q4 = jnp.where(same, jnp.concatenate([q] * per, axis=1), jnp.zeros((), q.dtype))
    inter = jnp.dot(q4, jnp.concatenate(starts, axis=0).astype(BF16), preferred_element_type=F32)
    num = s_inter * inter[:, :ML_DV] + intra[:, :ML_DV]
    den = s_inter * inter[:, ML_DV:] + intra[:, ML_DV:]
    h = num / jnp.maximum(jnp.abs(den), floor_rep)
    yield h, state[:, :ML_DV], state[:, ML_DV:]


def _mlstm_kernel(q_ref, k_ref, kt_ref, v_ref, o_ref, g_ref, ng_ref, out_ref, hfw_ref, hbw_ref, row_ref, col_ref,
                  *, n_block):
    pair = pl.program_id(1)
    per = ML_BLOCK // ML_CHUNK
    t_idx = lax.broadcasted_iota(jnp.int32, (ML_BLOCK, ML_BLOCK), 0)
    s_idx = lax.broadcasted_iota(jnp.int32, (ML_BLOCK, ML_BLOCK), 1)
    same = (t_idx // ML_CHUNK) == (s_idx // ML_CHUNK)
    masks = (same & (s_idx <= t_idx), same & (s_idx >= t_idx))
    lane = lax.broadcasted_iota(jnp.int32, (1, ML_BLOCK), 1)
    lane_chunk, lane_pos = lane // ML_CHUNK, lane % ML_CHUNK
    blk_row = lax.broadcasted_iota(jnp.int32, (n_block, 1), 0)
    row_chunk = lax.broadcasted_iota(jnp.int32, (ML_BLOCK, ML_DV), 0) // ML_CHUNK

    per_query = []
    for direction in range(2):
        cum = jnp.where(masks[1 - direction], 1.0, 0.0).astype(BF16)
        for hh in range(2):
            chain = direction * 2 + hh
            head = pair * 2 + hh
            lf = _log_sigmoid(g_ref[0, (direction * 2 + 1) * ML_HEADS + head])
            b = _exact_dot(lf, cum)
            a = g_ref[0, direction * 2 * ML_HEADS + head] - b
            a_max = [jnp.max(jnp.where(lane_chunk == i, a, -jnp.inf), axis=1, keepdims=True) for i in range(per)]
            f_sum = [jnp.sum(jnp.where(lane_chunk == i, lf, 0.0), axis=1, keepdims=True) for i in range(per)]
            m = jnp.zeros((1, 1), F32)
            m_row = jnp.zeros(a.shape, F32)
            ge_row = jnp.zeros(a.shape, F32)
            n_chunk = n_block * per
            for c in (range(n_chunk - 1, -1, -1) if direction else range(n_chunk)):
                blk, i = divmod(c, per)
                g_end = jnp.maximum(m, a_max[i][blk:blk + 1])
                here = (blk_row == blk) & (lane_chunk == i)
                m_row = jnp.where(here, m, m_row)
                ge_row = jnp.where(here, g_end, ge_row)
                m = f_sum[i][blk:blk + 1] + g_end
            g_row = jnp.maximum(m_row, _lane_cummax(a, lane_pos, reverse=bool(direction)))
            row_ref[chain, 0] = a
            row_ref[chain, 1] = ge_row
            row_ref[chain, 2] = jnp.exp(m_row - ge_row)
            row_ref[chain, 3] = m_row
            per_query += [g_row, b]
    flat = [jnp.concatenate([x[blk:blk + 1] for blk in range(n_block)], axis=1) for x in per_query]
    col_ref[...] = jnp.concatenate(flat, axis=0).T

    def body(step, carry):
        chains = []
        for direction in range(2):
            blk = step if direction == 0 else n_block - 1 - step
            rows = pl.ds(pl.multiple_of(blk * ML_BLOCK, ML_BLOCK), ML_BLOCK)
            for hh in range(2):
                chain = direction * 2 + hh
                c_state, n_state = carry[chain]
                stages = _mlstm_block(
                    q_ref[rows, hh * ML_DQK:(hh + 1) * ML_DQK], k_ref[rows, hh * ML_DQK:(hh + 1) * ML_DQK],
                    kt_ref[0, blk, hh * ML_DQK:(hh + 1) * ML_DQK, :], v_ref[rows, hh * ML_DV:(hh + 1) * ML_DV],
                    *[row_ref[chain, j, pl.ds(blk, 1), :] for j in range(4)],
                    col_ref[rows, 2 * chain:2 * chain + 1], col_ref[rows, 2 * chain + 1:2 * chain + 2],
                    c_state, n_state, masks[direction], same, row_chunk, reverse=bool(direction))
                chains.append((stages, hfw_ref if direction == 0 else hbw_ref, rows, hh))
        for _ in range(3):
            for stages, _, _, _ in chains:
                next(stages)
        new_carry = []
        for stages, dst, rows, hh in chains:
            h, c_new, n_new = next(stages)
            dst[rows, hh * ML_DV:(hh + 1) * ML_DV] = h
            new_carry.append((c_new, n_new))
        return tuple(new_carry)

    init = tuple((jnp.zeros((ML_DQK, ML_DV), F32), jnp.zeros((ML_DQK, ML_DV), F32)) for _ in range(4))
    lax.fori_loop(0, n_block, body, init)

    for hh in range(2):
        cols = slice(hh * ML_DV, (hh + 1) * ML_DV)
        h = hfw_ref[:, cols] + hbw_ref[:, cols]
        ms = jnp.mean(h * h, -1, keepdims=True)
        hn = h * lax.rsqrt(ms + EPS) * ng_ref[:, cols]
        out_ref[:, cols] = (o_ref[:, cols] * hn).astype(BF16)


def _mlstm(q, k, kt, v, o, gates, norm_g, bsz, seq):
    n_pair = ML_HEADS // 2
    n_block = seq // ML_BLOCK
    pair_w = 2 * ML_DV
    return pl.pallas_call(
        functools.partial(_mlstm_kernel, n_block=n_block),
        out_shape=jax.ShapeDtypeStruct((bsz * seq, D_MODEL), BF16),
        grid=(bsz, n_pair),
        in_specs=[pl.BlockSpec((seq, 2 * ML_DQK), lambda b, p: (b, p)),
                  pl.BlockSpec((seq, 2 * ML_DQK), lambda b, p: (b, p)),
                  pl.BlockSpec((1, n_block, 2 * ML_DQK, ML_BLOCK), lambda b, p: (b, 0, p, 0)),
                  pl.BlockSpec((seq, pair_w), lambda b, p: (b, p)),
                  pl.BlockSpec((seq, pair_w), lambda b, p: (b, p)),
                  pl.BlockSpec((1,) + gates.shape[1:], lambda b, p: (b, 0, 0, 0)),
                  pl.BlockSpec((1, pair_w), lambda b, p: (0, p))],
        out_specs=pl.BlockSpec((seq, pair_w), lambda b, p: (b, p)),
        scratch_shapes=[pltpu.VMEM((seq, pair_w), F32), pltpu.VMEM((seq, pair_w), F32),
                        pltpu.VMEM((4, 4, n_block, ML_BLOCK), F32), pltpu.VMEM((seq, 8), F32)],
        compiler_params=pltpu.CompilerParams(
            dimension_semantics=("parallel", "parallel"),
            vmem_limit_bytes=_vmem_limit(4 * _nbytes((seq, 2 * ML_DQK), BF16), 2 * _nbytes((seq, 2 * ML_DQK), F32),
                                         4 * _nbytes((seq, pair_w), BF16), 2 * _nbytes((seq, pair_w), F32),
                                         2 * _nbytes(gates.shape[1:], F32), 2 * _nbytes((seq, pair_w), F32))),
        name="mlstm_scan",
    )(q, k, kt, v, o, gates, norm_g)


def _block_tail_kernel(mix_ref, x_ref, p_ref, wo_ref, g1_ref, b1_ref, w1_ref, w2_ref, wg_ref, wp_ref, g2_ref, b2_ref,
                       o32_ref, o16_ref, *, ff_chunk, parts):
    tr = x_ref.shape[0] // parts
    rows = [slice(i * tr, (i + 1) * tr) for i in range(parts)]

    def out_proj(r):
        return jnp.dot(mix_ref[r, :], wo_ref[...], preferred_element_type=F32) + DN_ALPHA * x_ref[r, :]

    def ffn_chunk(xb, c):
        sl = slice(c * ff_chunk, (c + 1) * ff_chunk)
        h = jnp.maximum(jnp.dot(xb, w1_ref[:, sl], preferred_element_type=F32), 0.0)
        return jnp.dot((h * h).astype(BF16), w2_ref[sl, :], preferred_element_type=F32)

    def finish(x1, acc, r):
        z = _layer_norm(DN_ALPHA * x1 + acc, g2_ref[...], b2_ref[...])
        o32_ref[r, :] = z
        o16_ref[r, :] = z.astype(BF16)

    y = out_proj(rows[0])
    done = None
    for i in range(parts):
        y_next = out_proj(rows[i + 1]) if i + 1 < parts else None
        x1 = _layer_norm(y, g1_ref[...], b1_ref[...])
        xb = x1.astype(BF16)
        gate = jax.nn.sigmoid(jnp.dot(xb, wg_ref[...], preferred_element_type=F32))
        acc = gate * jnp.dot(p_ref[rows[i], :].astype(BF16), wp_ref[...], preferred_element_type=F32)
        for c in range(w1_ref.shape[1] // ff_chunk):
            acc = acc + ffn_chunk(xb, c)
            if c == 0 and done is not None:
                finish(*done)
        done = (x1, acc, rows[i])
        y = y_next
    finish(*done)


def _block_tail(mix, x32, wo, layer, p, g1, b1, w1, w2, wg, wp, g2, b2, tm=512, ff_chunk=1024, parts=2):
    t, d = x32.shape
    row = lambda i: (i, 0)

    def of_layer(a):
        nd = a.ndim - 1
        return pl.BlockSpec((None,) + a.shape[1:], lambda i: (layer,) + (0,) * nd, pipeline_mode=pl.Buffered(1))

    return pl.pallas_call(
        functools.partial(_block_tail_kernel, ff_chunk=ff_chunk, parts=parts),
        out_shape=(jax.ShapeDtypeStruct((t, d), F32), jax.ShapeDtypeStruct((t, d), BF16)),
        grid=(t // tm,),
        in_specs=[pl.BlockSpec((tm, d), row), pl.BlockSpec((tm, d), row),
                  pl.BlockSpec((None, tm, p.shape[2]), lambda i: (layer, i, 0)),
                  _const_spec(wo.shape), of_layer(g1), of_layer(b1), of_layer(w1), of_layer(w2),
                  of_layer(wg), of_layer(wp), of_layer(g2), of_layer(b2)],
        out_specs=(pl.BlockSpec((tm, d), row), pl.BlockSpec((tm, d), row)),
        compiler_params=pltpu.CompilerParams(
            dimension_semantics=("parallel",),
            vmem_limit_bytes=_vmem_limit(4 * _nbytes((tm, d), BF16), 4 * _nbytes((tm, d), F32),
                                         2 * _nbytes((tm, p.shape[2]), F32), _nbytes(wo.shape, BF16),
                                         _nbytes(w1.shape[1:], BF16), _nbytes(w2.shape[1:], BF16),
                                         _nbytes(wg.shape[1:], BF16), _nbytes(wp.shape[1:], BF16),
                                         2 * _nbytes((tm, ff_chunk), F32), 4 * _nbytes((tm, d), F32))),
        name="block_tail",
    )(mix, x32, p, wo, g1, b1, w1, w2, wg, wp, g2, b2)


def kernel(x, p, na_w_qkv, na_rpb, na_w_o, gq_w_qkv, gq_q_norm, gq_k_norm, gq_w_o, ml_w_in, ml_b_gates, ml_norm_g,
           ml_w_o, ln1_g, ln1_b, w_ff1, w_ff2, ln2_g, ln2_b, w_ple_gate, w_ple_proj):
    bsz, seq, d = x.shape
    assert d == D_MODEL and seq % (NA_KH * GRID_W) == 0 and p.shape == (DEPTH, bsz, seq, D_PLE)
    t = bsz * seq
    x32 = x.reshape(t, d)
    per_layer = (p.reshape(DEPTH, t, D_PLE), ln1_g[:, None, :], ln1_b[:, None, :], w_ff1.astype(BF16),
                 w_ff2.astype(BF16), w_ple_gate.astype(BF16), w_ple_proj.astype(BF16), ln2_g[:, None, :],
                 ln2_b[:, None, :])
    x16 = None
    qk_w = ML_HEADS * ML_DQK
    for i in range(DEPTH):
        kind, j = i % 3, i // 3
        if kind == 0:
            qkv = _proj(x32 if x16 is None else x16, na_w_qkv[j].astype(BF16), n_chunk=D_MODEL,
                        first_scale=HEAD_DIM ** -0.5 * LOG2E)
            mix = _na_attention(qkv, _na_bias_table(na_rpb[j]), bsz, seq)
            w_o = na_w_o[j]
        elif kind == 1:
            cos4, sin4 = _rope_tables(seq)
            w_qk, w_vt = _gqa_weight_layout(gq_w_qkv[j])
            qk, vt = _gqa_proj(x16, w_qk.astype(BF16), w_vt.astype(BF16), cos4, sin4, _gqa_gain_layout(gq_q_norm[j]),
                               _gqa_gain_layout(gq_k_norm[j]), _group_sum_matrix(), bsz, seq)
            mix = _gqa_attention(qk, vt, bsz, seq)
            w_o = gq_w_o[j]
        else:
            w_in = ml_w_in[j]
            n_main = 2 * qk_w + 2 * D_MODEL
            wkt = w_in[:, qk_w:2 * qk_w].T.astype(BF16)
            wgt = w_in[:, n_main:].T.astype(BF16)
            q, k, v, o, kt, gt = _ml_proj(x16, w_in[:, :n_main].astype(BF16), wkt, wgt, ml_b_gates[j][:, None], bsz, seq)
            gates = gt.reshape(bsz, gt.shape[1], seq // ML_BLOCK, ML_BLOCK)
            mix = _mlstm(q, k, kt, v, o, gates, ml_norm_g[j][None, :], bsz, seq)
            w_o = ml_w_o[j]
        x32, x16 = _block_tail(mix, x32, w_o.astype(BF16), i, *per_layer)
    return x32.reshape(bsz, seq, d)
```

```python
import functools

import jax
import jax.numpy as jnp
import numpy as np
from jax import lax
from jax.experimental import pallas as pl
from jax.experimental.pallas import tpu as pltpu

F32 = jnp.float32
BF16 = jnp.bfloat16

D_MODEL = 1024
DEPTH = 4
GRID_W = 64
HEAD_DIM = 64
D_PLE = 256
NA_HEADS = 16
NA_KH = 8
NA_KW = 16
GQA_KV_HEADS = 4
GQA_GROUP = 4
ROPE_THETA = 10000.0
ML_HEADS = 8
ML_DV = 128
ML_DQK = 64
ML_CHUNK = 64
ML_BLOCK = 256
DN_ALPHA = (2 * DEPTH) ** 0.25
EPS = 1e-6
LOG2E = 1.4426950408889634

V7X_VMEM_BYTES = 64 * 1024 * 1024
V7X_LANES = 128
BF16_SUBLANES = 16
SLAB = 2 * V7X_LANES

NT_DIMS = (((1,), (1,)), ((), ()))


def _vmem_limit(*byte_counts):
    est = int(sum(byte_counts) * 1.5) + (4 << 20)
    return min(est, V7X_VMEM_BYTES - (6 << 20))


def _nbytes(shape, dtype):
    return int(np.prod(shape)) * jnp.dtype(dtype).itemsize


def _const_spec(shape):
    nd = len(shape)
    return pl.BlockSpec(shape, lambda *_: (0,) * nd, pipeline_mode=pl.Buffered(1))


def _layer_norm(y, g, b):
    mu = jnp.mean(y, -1, keepdims=True)
    yc = y - mu
    var = jnp.mean(yc * yc, -1, keepdims=True)
    return yc * lax.rsqrt(var + EPS) * g + b


def _proj_kernel(x_ref, w_ref, o_ref, *, n_chunk, first_scale):
    xb = x_ref[...].astype(BF16)
    for j in range(o_ref.shape[1] // n_chunk):
        sl = slice(j * n_chunk, (j + 1) * n_chunk)
        y = jnp.dot(xb, w_ref[:, sl], preferred_element_type=F32)
        if j == 0 and first_scale != 1.0:
            y = y * first_scale
        o_ref[:, sl] = y.astype(BF16)


def _proj(x, w, tm=1024, n_chunk=1024, first_scale=1.0):
    t, d = x.shape
    n = w.shape[1]
    return pl.pallas_call(
        functools.partial(_proj_kernel, n_chunk=n_chunk, first_scale=first_scale),
        out_shape=jax.ShapeDtypeStruct((t, n), BF16),
        grid=(t // tm,),
        in_specs=[pl.BlockSpec((tm, d), lambda i: (i, 0)), _const_spec((d, n))],
        out_specs=pl.BlockSpec((tm, n), lambda i: (i, 0)),
        compiler_params=pltpu.CompilerParams(
            dimension_semantics=("parallel",),
            vmem_limit_bytes=_vmem_limit(2 * _nbytes((tm, d), x.dtype), _nbytes((d, n), BF16),
                                         2 * _nbytes((tm, n), BF16), _nbytes((tm, n_chunk), F32))),
        name="proj_plain",
    )(x, w)


def _na_kernel(q_ref, k_ref, v_ref, bias_ref, o_ref, s_ref, p_ref, *, rows, group):
    lane_head = lax.broadcasted_iota(jnp.int32, (1, SLAB), 1) // HEAD_DIM
    win = NA_KH * GRID_W

    def window(r):
        r = jnp.clip(r, 0, rows - 1)
        r0 = jnp.clip(r - NA_KH // 2, 0, rows - NA_KH)
        return pl.multiple_of(r * GRID_W, GRID_W), pl.multiple_of(r0 * GRID_W, GRID_W), r - r0

    def scores(r, slot):
        q0, k0, delta = window(r)
        q = q_ref[pl.ds(q0, GRID_W), :]
        qs = jnp.concatenate([jnp.where(lane_head == h, q, jnp.zeros_like(q)) for h in range(4)], axis=0)
        s = lax.dot_general(qs, k_ref[pl.ds(k0, win), :], NT_DIMS, preferred_element_type=F32)
        s_ref[slot] = s + bias_ref[0, delta]

    def softmax(slot):
        s = s_ref[slot]
        e = jnp.exp2(s - jnp.max(s, -1, keepdims=True))
        p_ref[slot] = (e * (1.0 / jnp.sum(e, -1, keepdims=True))).astype(BF16)

    def weighted_values(r, slot):
        q0, k0, _ = window(r)
        pv = jnp.dot(p_ref[slot], v_ref[pl.ds(k0, win), :], preferred_element_type=F32)
        acc = jnp.zeros((GRID_W, SLAB), F32)
        for h in range(4):
            acc = jnp.where(lane_head == h, pv[h * GRID_W:(h + 1) * GRID_W], acc)
        o_ref[pl.ds(q0, GRID_W), :] = acc.astype(BF16)

    def step(g, bank):
        for t in range(group):
            weighted_values((g - 1) * group + t, (1 - bank) * group + t)
        for t in range(group):
            scores((g + 1) * group + t, (1 - bank) * group + t)
        for t in range(group):
            softmax(bank * group + t)

    for t in range(group):
        scores(t, t)
        p_ref[group + t] = jnp.zeros(p_ref.shape[1:], BF16)

    def body(j, carry):
        step(2 * j, 0)
        step(2 * j + 1, 1)
        return carry

    n_step = rows // group
    lax.fori_loop(0, n_step // 2, body, 0)
    for t in range(group):
        weighted_values(rows - group + t, ((n_step - 1) % 2) * group + t)


def _na_bias_table(rpb):
    col = np.arange(GRID_W)
    c0 = np.clip(col - NA_KW // 2, 0, GRID_W - NA_KW)
    col_in = (col[None, :] >= c0[:, None]) & (col[None, :] < c0[:, None] + NA_KW)
    rpb = rpb.astype(F32)
    edge = GRID_W - NA_KW
    ext = jnp.concatenate([jnp.repeat(rpb[:, :, :1], edge, axis=2), rpb, jnp.repeat(rpb[:, :, -1:], edge, axis=2)], axis=2)
    by_col = jnp.stack([ext[:, :, GRID_W - 1 - q:2 * GRID_W - 1 - q] for q in range(GRID_W)], axis=2)
    by_col = jnp.where(col_in[None, None], by_col * LOG2E, -jnp.inf)
    per_delta = [by_col[:, NA_KH - 1 - dl:2 * NA_KH - 1 - dl].transpose(0, 2, 1, 3)
                 .reshape(NA_HEADS, GRID_W, NA_KH * GRID_W) for dl in range(NA_KH)]
    b = jnp.stack(per_delta, axis=1).reshape(NA_HEADS // 4, 4, NA_KH, GRID_W, NA_KH * GRID_W)
    return b.transpose(0, 2, 1, 3, 4).reshape(NA_HEADS // 4, NA_KH, 4 * GRID_W, NA_KH * GRID_W)


def _na_attention(qkv, bias, bsz, seq, group=2):
    n_slab = D_MODEL // SLAB
    rows = seq // GRID_W
    assert rows % (2 * group) == 0
    blk = (seq, SLAB)
    tile = (4 * GRID_W, NA_KH * GRID_W)
    return pl.pallas_call(
        functools.partial(_na_kernel, rows=rows, group=group),
        out_shape=jax.ShapeDtypeStruct((bsz * seq, D_MODEL), BF16),
        grid=(n_slab, bsz),
        in_specs=[pl.BlockSpec(blk, lambda s, b: (b, s)),
                  pl.BlockSpec(blk, lambda s, b: (b, n_slab + s)),
                  pl.BlockSpec(blk, lambda s, b: (b, 2 * n_slab + s)),
                  pl.BlockSpec((1,) + bias.shape[1:], lambda s, b: (s, 0, 0, 0))],
        out_specs=pl.BlockSpec(blk, lambda s, b: (b, s)),
        scratch_shapes=[pltpu.VMEM((2 * group,) + tile, F32), pltpu.VMEM((2 * group,) + tile, BF16)],
        compiler_params=pltpu.CompilerParams(
            dimension_semantics=("parallel", "parallel"),
            vmem_limit_bytes=_vmem_limit(8 * _nbytes(blk, BF16), 2 * _nbytes(bias.shape[1:], F32),
                                         (3 * group + 4) * _nbytes(tile, F32))),
        name="na_attention",
    )(qkv, qkv, qkv, bias)


def _gqa_proj_kernel(x_ref, w_ref, wvt_ref, cos_ref, sin_ref, gq_ref, gk_ref, ones_ref, o_ref, vt_ref):
    xb = x_ref[...]
    cos = cos_ref[...]
    sin = sin_ref[...]
    ones = ones_ref[...]
    n_norm = 2 * D_MODEL // (2 * SLAB)

    def project(pair):
        return jnp.dot(xb, w_ref[:, pair * 2 * SLAB:(pair + 1) * 2 * SLAB], preferred_element_type=F32)

    z_next = project(0)
    for pair in range(n_norm):
        z = z_next
        if pair + 1 < n_norm:
            z_next = project(pair + 1)
        else:
            vt_ref[...] = lax.dot_general(wvt_ref[...], xb, NT_DIMS, preferred_element_type=F32).astype(BF16)
        halves = [(z[:, j * SLAB:j * SLAB + V7X_LANES], z[:, j * SLAB + V7X_LANES:(j + 1) * SLAB]) for j in range(2)]
        ss = jnp.concatenate([a * a + b * b for a, b in halves], axis=1)
        hi = ss.astype(BF16)
        lo = (ss - hi.astype(F32)).astype(BF16)
        ms = (jnp.dot(hi, ones, preferred_element_type=F32)
              + jnp.dot(lo, ones, preferred_element_type=F32)) * (1.0 / HEAD_DIM)
        rs = lax.rsqrt(ms + EPS)
        is_q = pair < n_norm // 2
        g_ref = gq_ref if is_q else gk_ref
        for j, (a, b) in enumerate(halves):
            r = rs[:, j * V7X_LANES:(j + 1) * V7X_LANES]
            an = a * r * g_ref[:, :V7X_LANES]
            bn = b * r * g_ref[:, V7X_LANES:]
            oa = an * cos - bn * sin
            ob = an * sin + bn * cos
            if is_q:
                oa = oa * (HEAD_DIM ** -0.5 * LOG2E)
                ob = ob * (HEAD_DIM ** -0.5 * LOG2E)
            c0 = (pair * 2 + j) * SLAB
            o_ref[:, c0:c0 + V7X_LANES] = oa.astype(BF16)
            o_ref[:, c0 + V7X_LANES:c0 + SLAB] = ob.astype(BF16)


def _gqa_weight_layout(w):
    d = w.shape[0]
    kvd = GQA_KV_HEADS * HEAD_DIM
    half = HEAD_DIM // 2
    wq = w[:, :D_MODEL].reshape(d, GQA_KV_HEADS, GQA_GROUP, half, 2).transpose(0, 1, 4, 2, 3)
    wk = w[:, D_MODEL:D_MODEL + kvd].reshape(d, GQA_KV_HEADS, half, 2).transpose(0, 1, 3, 2)
    wk = jnp.broadcast_to(wk[:, :, :, None, :], (d, GQA_KV_HEADS, 2, GQA_GROUP, half))
    return jnp.concatenate([wq.reshape(d, D_MODEL), wk.reshape(d, D_MODEL)], axis=1), w[:, D_MODEL + kvd:].T


def _gqa_gain_layout(g):
    half = HEAD_DIM // 2
    return jnp.broadcast_to(g.reshape(half, 2).T[:, None, :], (2, GQA_GROUP, half)).reshape(1, SLAB)


def _group_sum_matrix():
    blk = np.arange(SLAB) // (HEAD_DIM // 2)
    return jnp.asarray(blk[:, None] == blk[None, :], BF16)


def _rope_tables(seq):
    t = jnp.arange(seq)
    row = (t // GRID_W).astype(F32)
    col = (t % GRID_W).astype(F32)
    n_pairs = HEAD_DIM // 4
    inv = ROPE_THETA ** (-jnp.arange(n_pairs, dtype=F32) / n_pairs)
    ang = jnp.concatenate([row[:, None] * inv, col[:, None] * inv], -1)
    return jnp.tile(jnp.cos(ang), (1, GQA_GROUP)), jnp.tile(jnp.sin(ang), (1, GQA_GROUP))


def _gqa_proj(x16, w, wvt, cos4, sin4, gq, gk, ones, bsz, seq, tm=1024):
    t, d = x16.shape
    n = w.shape[1]
    per_seq = seq // tm
    return pl.pallas_call(
        _gqa_proj_kernel,
        out_shape=(jax.ShapeDtypeStruct((t, n), BF16), jax.ShapeDtypeStruct((bsz, wvt.shape[0], seq), BF16)),
        grid=(t // tm,),
        in_specs=[pl.BlockSpec((tm, d), lambda i: (i, 0)), _const_spec((d, n)), _const_spec(wvt.shape),
                  pl.BlockSpec((tm, V7X_LANES), lambda i: (i % per_seq, 0)),
                  pl.BlockSpec((tm, V7X_LANES), lambda i: (i % per_seq, 0)),
                  _const_spec((1, SLAB)), _const_spec((1, SLAB)), _const_spec((SLAB, SLAB))],
        out_specs=(pl.BlockSpec((tm, n), lambda i: (i, 0)),
                   pl.BlockSpec((None, wvt.shape[0], tm), lambda i: (i // per_seq, 0, i % per_seq))),
        compiler_params=pltpu.CompilerParams(
            dimension_semantics=("parallel",),
            vmem_limit_bytes=_vmem_limit(2 * _nbytes((tm, d), BF16), _nbytes((d, n), BF16), _nbytes(wvt.shape, BF16),
                                         2 * _nbytes((tm, n), BF16), 6 * _nbytes((tm, 2 * SLAB), F32))),
        name="gqa_proj",
    )(x16, w, wvt, cos4, sin4, gq, gk, ones)


def _gqa_attn_kernel(q_ref, k_ref, vt_ref, o_ref, *, q_block):
    lane = lax.broadcasted_iota(jnp.int32, (1, SLAB), 1)
    q = q_ref[...]
    q_head = (lane % V7X_LANES) // (HEAD_DIM // 2)

    def scores(h):
        qm = jnp.where(q_head == h, q, jnp.zeros_like(q))
        return lax.dot_general(k_ref[...], qm, NT_DIMS, preferred_element_type=F32)

    vt = vt_ref[...]
    vt_ones = jnp.concatenate([vt, jnp.ones((BF16_SUBLANES, vt.shape[1]), BF16)], axis=0)

    def softmax(s):
        return (jnp.exp2(s - jnp.max(s, 0, keepdims=True)).astype(BF16),)

    def weighted_values(e):
        r = jnp.dot(vt_ones, e, preferred_element_type=F32)
        return r[:HEAD_DIM] * (1.0 / r[HEAD_DIM:HEAD_DIM + 1])

    n_qb = q_ref.shape[0] // q_block
    outs = [[None] * GQA_GROUP for _ in range(n_qb)]
    s_next = scores(0)
    pending = []
    for h in range(GQA_GROUP):
        s = s_next
        if h + 1 < GQA_GROUP:
            s_next = scores(h + 1)
        current = []
        for qb in range(n_qb):
            current.append(softmax(s[:, qb * q_block:(qb + 1) * q_block]))
            if pending:
                outs[qb][h - 1] = weighted_values(*pending[qb])
        pending = current
    for qb in range(n_qb):
        outs[qb][GQA_GROUP - 1] = weighted_values(*pending[qb])
        o_ref[qb * q_block:(qb + 1) * q_block, :] = jnp.concatenate(outs[qb], axis=0).T.astype(BF16)


def _gqa_attention(qk, vt, bsz, seq, tq=1024, q_block=256):
    n_slab = D_MODEL // SLAB
    nq = seq // tq
    return pl.pallas_call(
        functools.partial(_gqa_attn_kernel, q_block=q_block),
        out_shape=jax.ShapeDtypeStruct((bsz * seq, D_MODEL), BF16),
        grid=(bsz, n_slab, nq),
        in_specs=[pl.BlockSpec((tq, SLAB), lambda b, g, i: (b * nq + i, g)),
                  pl.BlockSpec((seq, SLAB), lambda b, g, i: (b, n_slab + g)),
                  pl.BlockSpec((None, HEAD_DIM, seq), lambda b, g, i: (b, g, 0))],
        out_specs=pl.BlockSpec((tq, SLAB), lambda b, g, i: (b * nq + i, g)),
        compiler_params=pltpu.CompilerParams(
            dimension_semantics=("parallel", "parallel", "parallel"),
            vmem_limit_bytes=_vmem_limit(4 * _nbytes((tq, SLAB), BF16), 4 * _nbytes((seq, SLAB), BF16),
                                         2 * _nbytes((HEAD_DIM, seq), BF16), 3 * _nbytes((tq, seq), F32))),
        name="gqa_attention",
    )(qk, qk, vt)


def _ml_proj_kernel(x_ref, w_ref, wkt_ref, wgt_ref, bg_ref, q_ref, k_ref, v_ref, o_ref, kt_ref, gt_ref):
    xb = x_ref[...]
    qk_w = ML_HEADS * ML_DQK
    q_ref[...] = (jnp.dot(xb, w_ref[:, :qk_w], preferred_element_type=F32) * (ML_DQK ** -0.5)).astype(BF16)
    k_ref[...] = jnp.dot(xb, w_ref[:, qk_w:2 * qk_w], preferred_element_type=F32).astype(BF16)
    v_ref[...] = jnp.dot(xb, w_ref[:, 2 * qk_w:2 * qk_w + D_MODEL], preferred_element_type=F32).astype(BF16)
    o_ref[...] = jax.nn.sigmoid(jnp.dot(xb, w_ref[:, 2 * qk_w + D_MODEL:], preferred_element_type=F32)).astype(BF16)
    kt = lax.dot_general(wkt_ref[...], xb, NT_DIMS, preferred_element_type=F32)
    for j in range(kt_ref.shape[1]):
        kt_ref[0, j] = kt[:, j * ML_BLOCK:(j + 1) * ML_BLOCK].astype(BF16)
    gt_ref[0] = lax.dot_general(wgt_ref[...], xb, NT_DIMS, preferred_element_type=F32) + bg_ref[...]


def _ml_proj(x16, w, wkt, wgt, bg, bsz, seq, tm=1024):
    t, d = x16.shape
    qk_w = ML_HEADS * ML_DQK
    per_seq = seq // tm
    n_gate = wgt.shape[0]
    row = lambda i: (i, 0)
    return pl.pallas_call(
        _ml_proj_kernel,
        out_shape=(jax.ShapeDtypeStruct((t, qk_w), BF16), jax.ShapeDtypeStruct((t, qk_w), BF16),
                   jax.ShapeDtypeStruct((t, D_MODEL), BF16), jax.ShapeDtypeStruct((t, D_MODEL), BF16),
                   jax.ShapeDtypeStruct((bsz, seq // ML_BLOCK, qk_w, ML_BLOCK), BF16),
                   jax.ShapeDtypeStruct((bsz, n_gate, seq), F32)),
        grid=(t // tm,),
        in_specs=[pl.BlockSpec((tm, d), row), _const_spec(w.shape), _const_spec(wkt.shape),
                  _const_spec(wgt.shape), _const_spec(bg.shape)],
        out_specs=(pl.BlockSpec((tm, qk_w), row), pl.BlockSpec((tm, qk_w), row),
                   pl.BlockSpec((tm, D_MODEL), row), pl.BlockSpec((tm, D_MODEL), row),
                   pl.BlockSpec((1, tm // ML_BLOCK, qk_w, ML_BLOCK), lambda i: (i // per_seq, i % per_seq, 0, 0)),
                   pl.BlockSpec((1, n_gate, tm), lambda i: (i // per_seq, 0, i % per_seq))),
        compiler_params=pltpu.CompilerParams(
            dimension_semantics=("parallel",),
            vmem_limit_bytes=_vmem_limit(2 * _nbytes((tm, d), BF16), _nbytes(w.shape, BF16), _nbytes(wkt.shape, BF16),
                                         2 * _nbytes((tm, 2 * qk_w + D_MODEL), BF16), 2 * _nbytes((tm, D_MODEL), F32),
                                         4 * _nbytes((qk_w, tm), F32), 2 * _nbytes((tm, D_MODEL), F32))),
        name="mlstm_proj",
    )(x16, w, wkt, wgt, bg)


def _log_sigmoid(x):
    return jnp.minimum(x, 0.0) - jnp.log1p(jnp.exp(-jnp.abs(x)))


def _exact_dot(x, m):
    hi = x.astype(BF16)
    r1 = x - hi.astype(F32)
    mid = r1.astype(BF16)
    lo = (r1 - mid.astype(F32)).astype(BF16)
    return (jnp.dot(hi, m, preferred_element_type=F32) + jnp.dot(mid, m, preferred_element_type=F32)
            + jnp.dot(lo, m, preferred_element_type=F32))


def _lane_cummax(x, lane_pos, reverse):
    shift = 1
    while shift < ML_CHUNK:
        if reverse:
            moved, ok = pltpu.roll(x, x.shape[1] - shift, axis=1), lane_pos < ML_CHUNK - shift
        else:
            moved, ok = pltpu.roll(x, shift, axis=1), lane_pos >= shift
        x = jnp.maximum(x, jnp.where(ok, moved, -jnp.inf))
        shift *= 2
    return x


def _mlstm_block(q, k, kt, v, a_row, ge_row, decay_row, m_row, g_col, b_col, c_state, n_state, vis, same, row_chunk,
                 reverse):
    per = ML_BLOCK // ML_CHUNK
    v_ext = jnp.concatenate([v, jnp.ones_like(v)], axis=1)
    wkt = jnp.exp(a_row - ge_row) * kt.astype(F32)
    wkt4 = jnp.where(same, jnp.concatenate([wkt] * per, axis=0), 0.0)
    delta = jnp.dot(wkt4.astype(BF16), v_ext, preferred_element_type=F32)
    yield
    g_rep = jnp.broadcast_to(g_col, (ML_BLOCK, ML_DV))
    b_rep = jnp.broadcast_to(b_col, (ML_BLOCK, ML_DV))
    m_rep = jnp.broadcast_to(m_row[:, (per - 1) * ML_CHUNK:(per - 1) * ML_CHUNK + 1], g_rep.shape)
    for i in range(per - 2, -1, -1):
        m_rep = jnp.where(row_chunk == i, m_row[:, i * ML_CHUNK:i * ML_CHUNK + 1], m_rep)
    s_inter = jnp.exp(m_rep - g_rep)
    floor_rep = jnp.exp(-(b_rep + g_rep))
    w = jnp.where(vis, jnp.exp(a_row - jnp.concatenate([g_rep, g_rep], axis=1)), 0.0)
    qk = lax.dot_general(q, k, NT_DIMS, preferred_element_type=F32) * w
    yield
    intra = jnp.dot(qk.astype(BF16), v_ext, preferred_element_type=F32)
    starts = [None] * per
    state = jnp.concatenate([c_state, n_state], axis=1)
    for i in (range(per - 1, -1, -1) if reverse else range(per)):
        starts[i] = state
        state = decay_row[:, i * ML_CHUNK:i * ML_CHUNK + 1] * state + delta[i * ML_DQK:(i + 1) * ML_DQK]
    yield
    q4 = jnp.where(same, jnp.concatenate([q] * per, axis=1), jnp.zeros((), q.dtype))
    inter = jnp.dot(q4, jnp.concatenate(starts, axis=0).astype(BF16), preferred_element_type=F32)
    num = s_inter * inter[:, :ML_DV] + intra[:, :ML_DV]
    den = s_inter * inter[:, ML_DV:] + intra[:, ML_DV:]
    h = num / jnp.maximum(jnp.abs(den), floor_rep)
    yield h, state[:, :ML_DV], state[:, ML_DV:]


def _mlstm_kernel(q_ref, k_ref, kt_ref, v_ref, o_ref, g_ref, ng_ref, out_ref, hfw_ref, hbw_ref, row_ref, col_ref,
                  *, n_block):
    pair = pl.program_id(1)
    per = ML_BLOCK // ML_CHUNK
    t_idx = lax.broadcasted_iota(jnp.int32, (ML_BLOCK, ML_BLOCK), 0)
    s_idx = lax.broadcasted_iota(jnp.int32, (ML_BLOCK, ML_BLOCK), 1)
    same = (t_idx // ML_CHUNK) == (s_idx // ML_CHUNK)
    masks = (same & (s_idx <= t_idx), same & (s_idx >= t_idx))
    lane = lax.broadcasted_iota(jnp.int32, (1, ML_BLOCK), 1)
    lane_chunk, lane_pos = lane // ML_CHUNK, lane % ML_CHUNK
    blk_row = lax.broadcasted_iota(jnp.int32, (n_block, 1), 0)
    row_chunk = lax.broadcasted_iota(jnp.int32, (ML_BLOCK, ML_DV), 0) // ML_CHUNK

    per_query = []
    for direction in range(2):
        cum = jnp.where(masks[1 - direction], 1.0, 0.0).astype(BF16)
        for hh in range(2):
            chain = direction * 2 + hh
            head = pair * 2 + hh
            lf = _log_sigmoid(g_ref[0, (direction * 2 + 1) * ML_HEADS + head])
            b = _exact_dot(lf, cum)
            a = g_ref[0, direction * 2 * ML_HEADS + head] - b
            a_max = [jnp.max(jnp.where(lane_chunk == i, a, -jnp.inf), axis=1, keepdims=True) for i in range(per)]
            f_sum = [jnp.sum(jnp.where(lane_chunk == i, lf, 0.0), axis=1, keepdims=True) for i in range(per)]
            m = jnp.zeros((1, 1), F32)
            m_row = jnp.zeros(a.shape, F32)
            ge_row = jnp.zeros(a.shape, F32)
            n_chunk = n_block * per
            for c in (range(n_chunk - 1, -1, -1) if direction else range(n_chunk)):
                blk, i = divmod(c, per)
                g_end = jnp.maximum(m, a_max[i][blk:blk + 1])
                here = (blk_row == blk) & (lane_chunk == i)
                m_row = jnp.where(here, m, m_row)
                ge_row = jnp.where(here, g_end, ge_row)
                m = f_sum[i][blk:blk + 1] + g_end
            g_row = jnp.maximum(m_row, _lane_cummax(a, lane_pos, reverse=bool(direction)))
            row_ref[chain, 0] = a
            row_ref[chain, 1] = ge_row
            row_ref[chain, 2] = jnp.exp(m_row - ge_row)
            row_ref[chain, 3] = m_row
            per_query += [g_row, b]
    flat = [jnp.concatenate([x[blk:blk + 1] for blk in range(n_block)], axis=1) for x in per_query]
    col_ref[...] = jnp.concatenate(flat, axis=0).T

    def body(step, carry):
        chains = []
        for direction in range(2):
            blk = step if direction == 0 else n_block - 1 - step
            rows = pl.ds(pl.multiple_of(blk * ML_BLOCK, ML_BLOCK), ML_BLOCK)
            for hh in range(2):
                chain = direction * 2 + hh
                c_state, n_state = carry[chain]
                stages = _mlstm_block(
                    q_ref[rows, hh * ML_DQK:(hh + 1) * ML_DQK], k_ref[rows, hh * ML_DQK:(hh + 1) * ML_DQK],
                    kt_ref[0, blk, hh * ML_DQK:(hh + 1) * ML_DQK, :], v_ref[rows, hh * ML_DV:(hh + 1) * ML_DV],
                    *[row_ref[chain, j, pl.ds(blk, 1), :] for j in range(4)],
                    col_ref[rows, 2 * chain:2 * chain + 1], col_ref[rows, 2 * chain + 1:2 * chain + 2],
                    c_state, n_state, masks[direction], same, row_chunk, reverse=bool(direction))
                chains.append((stages, hfw_ref if direction == 0 else hbw_ref, rows, hh))
        for _ in range(3):
            for stages, _, _, _ in chains:
                next(stages)
        new_carry = []
        for stages, dst, rows, hh in chains:
            h, c_new, n_new = next(stages)
            dst[rows, hh * ML_DV:(hh + 1) * ML_DV] = h
            new_carry.append((c_new, n_new))
        return tuple(new_carry)

    init = tuple((jnp.zeros((ML_DQK, ML_DV), F32), jnp.zeros((ML_DQK, ML_DV), F32)) for _ in range(4))
    lax.fori_loop(0, n_block, body, init)

    for hh in range(2):
        cols = slice(hh * ML_DV, (hh + 1) * ML_DV)
        h = hfw_ref[:, cols] + hbw_ref[:, cols]
        ms = jnp.mean(h * h, -1, keepdims=True)
        hn = h * lax.rsqrt(ms + EPS) * ng_ref[:, cols]
        out_ref[:, cols] = (o_ref[:, cols].astype(F32) * hn).astype(BF16)


def _mlstm(q, k, kt, v, o, gates, norm_g, bsz, seq):
    n_pair = ML_HEADS // 2
    n_block = seq // ML_BLOCK
    pair_w = 2 * ML_DV
    return pl.pallas_call(
        functools.partial(_mlstm_kernel, n_block=n_block),
        out_shape=jax.ShapeDtypeStruct((bsz * seq, D_MODEL), BF16),
        grid=(bsz, n_pair),
        in_specs=[pl.BlockSpec((seq, 2 * ML_DQK), lambda b, p: (b, p)),
                  pl.BlockSpec((seq, 2 * ML_DQK), lambda b, p: (b, p)),
                  pl.BlockSpec((1, n_block, 2 * ML_DQK, ML_BLOCK), lambda b, p: (b, 0, p, 0)),
                  pl.BlockSpec((seq, pair_w), lambda b, p: (b, p)),
                  pl.BlockSpec((seq, pair_w), lambda b, p: (b, p)),
                  pl.BlockSpec((1,) + gates.shape[1:], lambda b, p: (b, 0, 0, 0)),
                  pl.BlockSpec((1, pair_w), lambda b, p: (0, p))],
        out_specs=pl.BlockSpec((seq, pair_w), lambda b, p: (b, p)),
        scratch_shapes=[pltpu.VMEM((seq, pair_w), F32), pltpu.VMEM((seq, pair_w), F32),
                        pltpu.VMEM((4, 4, n_block, ML_BLOCK), F32), pltpu.VMEM((seq, 8), F32)],
        compiler_params=pltpu.CompilerParams(
            dimension_semantics=("parallel", "parallel"),
            vmem_limit_bytes=_vmem_limit(4 * _nbytes((seq, 2 * ML_DQK), BF16), 2 * _nbytes((seq, 2 * ML_DQK), F32),
                                         4 * _nbytes((seq, pair_w), BF16), 2 * _nbytes((seq, pair_w), F32),
                                         2 * _nbytes(gates.shape[1:], F32), 2 * _nbytes((seq, pair_w), F32))),
        name="mlstm_scan",
    )(q, k, kt, v, o, gates, norm_g)


def _block_tail_kernel(mix_ref, x_ref, p_ref, wo_ref, g1_ref, b1_ref, w1_ref, w2_ref, wg_ref, wp_ref, g2_ref, b2_ref,
                       o32_ref, o16_ref, *, ff_chunk, parts):
    tr = x_ref.shape[0] // parts
    rows = [slice(i * tr, (i + 1) * tr) for i in range(parts)]

    def out_proj(r):
        return jnp.dot(mix_ref[r, :], wo_ref[...], preferred_element_type=F32) + DN_ALPHA * x_ref[r, :]

    def ffn_chunk(xb, c):
        sl = slice(c * ff_chunk, (c + 1) * ff_chunk)
        h = jnp.maximum(jnp.dot(xb, w1_ref[:, sl], preferred_element_type=F32), 0.0)
        return jnp.dot((h * h).astype(BF16), w2_ref[sl, :], preferred_element_type=F32)

    def finish(x1, acc, r):
        z = _layer_norm(DN_ALPHA * x1 + acc, g2_ref[...], b2_ref[...])
        o32_ref[r, :] = z
        o16_ref[r, :] = z.astype(BF16)

    y = out_proj(rows[0])
    done = None
    for i in range(parts):
        y_next = out_proj(rows[i + 1]) if i + 1 < parts else None
        x1 = _layer_norm(y, g1_ref[...], b1_ref[...])
        xb = x1.astype(BF16)
        gate = jax.nn.sigmoid(jnp.dot(xb, wg_ref[...], preferred_element_type=F32))
        acc = gate * jnp.dot(p_ref[rows[i], :].astype(BF16), wp_ref[...], preferred_element_type=F32)
        for c in range(w1_ref.shape[1] // ff_chunk):
            acc = acc + ffn_chunk(xb, c)
            if c == 0 and done is not None:
                finish(*done)
        done = (x1, acc, rows[i])
        y = y_next
    finish(*done)


def _block_tail(mix, x32, wo, layer, p, g1, b1, w1, w2, wg, wp, g2, b2, tm=512, ff_chunk=1024, parts=2):
    t, d = x32.shape
    row = lambda i: (i, 0)

    def of_layer(a):
        nd = a.ndim - 1
        return pl.BlockSpec((None,) + a.shape[1:], lambda i: (layer,) + (0,) * nd, pipeline_mode=pl.Buffered(1))

    return pl.pallas_call(
        functools.partial(_block_tail_kernel, ff_chunk=ff_chunk, parts=parts),
        out_shape=(jax.ShapeDtypeStruct((t, d), F32), jax.ShapeDtypeStruct((t, d), BF16)),
        grid=(t // tm,),
        in_specs=[pl.BlockSpec((tm, d), row), pl.BlockSpec((tm, d), row),
                  pl.BlockSpec((None, tm, p.shape[2]), lambda i: (layer, i, 0)),
                  _const_spec(wo.shape), of_layer(g1), of_layer(b1), of_layer(w1), of_layer(w2),
                  of_layer(wg), of_layer(wp), of_layer(g2), of_layer(b2)],
        out_specs=(pl.BlockSpec((tm, d), row), pl.BlockSpec((tm, d), row)),
        compiler_params=pltpu.CompilerParams(
            dimension_semantics=("parallel",),
            vmem_limit_bytes=_vmem_limit(4 * _nbytes((tm, d), BF16), 4 * _nbytes((tm, d), F32),
                                         2 * _nbytes((tm, p.shape[2]), F32), _nbytes(wo.shape, BF16),
                                         _nbytes(w1.shape[1:], BF16), _nbytes(w2.shape[1:], BF16),
                                         _nbytes(wg.shape[1:], BF16), _nbytes(wp.shape[1:], BF16),
                                         2 * _nbytes((tm, ff_chunk), F32), 4 * _nbytes((tm, d), F32))),
        name="block_tail",
    )(mix, x32, p, wo, g1, b1, w1, w2, wg, wp, g2, b2)


def kernel(x, p, na_w_qkv, na_rpb, na_w_o, gq_w_qkv, gq_q_norm, gq_k_norm, gq_w_o, ml_w_in, ml_b_gates, ml_norm_g,
           ml_w_o, ln1_g, ln1_b, w_ff1, w_ff2, ln2_g, ln2_b, w_ple_gate, w_ple_proj):
    bsz, seq, d = x.shape
    assert d == D_MODEL and seq % (NA_KH * GRID_W) == 0 and p.shape == (DEPTH, bsz, seq, D_PLE)
    t = bsz * seq
    x32 = x.reshape(t, d)
    per_layer = (p.reshape(DEPTH, t, D_PLE), ln1_g[:, None, :], ln1_b[:, None, :], w_ff1.astype(BF16),
                 w_ff2.astype(BF16), w_ple_gate.astype(BF16), w_ple_proj.astype(BF16), ln2_g[:, None, :],
                 ln2_b[:, None, :])
    x16 = None
    qk_w = ML_HEADS * ML_DQK
    for i in range(DEPTH):
        kind, j = i % 3, i // 3
        if kind == 0:
            qkv = _proj(x32 if x16 is None else x16, na_w_qkv[j].astype(BF16), n_chunk=D_MODEL,
                        first_scale=HEAD_DIM ** -0.5 * LOG2E)
            mix = _na_attention(qkv, _na_bias_table(na_rpb[j]), bsz, seq)
            w_o = na_w_o[j]
        elif kind == 1:
            cos4, sin4 = _rope_tables(seq)
            w_qk, w_vt = _gqa_weight_layout(gq_w_qkv[j])
            qk, vt = _gqa_proj(x16, w_qk.astype(BF16), w_vt.astype(BF16), cos4, sin4, _gqa_gain_layout(gq_q_norm[j]),
                               _gqa_gain_layout(gq_k_norm[j]), _group_sum_matrix(), bsz, seq)
            mix = _gqa_attention(qk, vt, bsz, seq)
            w_o = gq_w_o[j]
        else:
            w_in = ml_w_in[j]
            n_main = 2 * qk_w + 2 * D_MODEL
            wkt = w_in[:, qk_w:2 * qk_w].T.astype(BF16)
            wgt = w_in[:, n_main:].T.astype(BF16)
            q, k, v, o, kt, gt = _ml_proj(x16, w_in[:, :n_main].astype(BF16), wkt, wgt, ml_b_gates[j][:, None], bsz, seq)
            gates = gt.reshape(bsz, gt.shape[1], seq // ML_BLOCK, ML_BLOCK)
            mix = _mlstm(q, k, kt, v, o, gates, ml_norm_g[j][None, :], bsz, seq)
            w_o = ml_w_o[j]
        x32, x16 = _block_tail(mix, x32, w_o.astype(BF16), i, *per_layer)
    return x32.reshape(bsz, seq, d)
```
